```python
import jax, jax.numpy as jnp
from jax import lax
import numpy as np

D_MODEL = 1024
BATCH = 4
SEQ = 4096
DEPTH = 2

ATT_HEADS = 4
D_ATT = D_MODEL // 2
ATT_DV = D_ATT // ATT_HEADS
ATT_DK = ATT_DV // 2
CC_CH = D_MODEL // 4
CC_GROUPS = 4
CC_K = 31
SC_CH = D_MODEL // 4
SC_K = 3
D_MIX = D_ATT + CC_CH + SC_CH
Q_COLS = ATT_HEADS * 2 * ATT_DK
K_COLS = ATT_HEADS * 2 * ATT_DK
V_COLS = ATT_HEADS * ATT_DV
CC_COLS = 2 * CC_CH
SC_COLS = 3 * SC_CH
D_IN = Q_COLS + K_COLS + V_COLS + CC_COLS + SC_COLS
D_FF = ((8 * D_MODEL // 3 + 255) // 256) * 256
MEM_LEN = 256
XA_HEADS = 4
XA_HD = D_MODEL // XA_HEADS
Q_BLOCK = 128
EPS = 1e-6

kernel_name = "hymba_style_diffattn_conformer_shortconv_hybrid"


def rms_norm(x, g):
    xf = x.astype(jnp.float32)
    y = xf * lax.rsqrt(jnp.mean(xf * xf, axis=-1, keepdims=True) + EPS)
    return (y * g.astype(jnp.float32)).astype(x.dtype)


def layer_norm(x, g, b):
    xf = x.astype(jnp.float32)
    mu = jnp.mean(xf, axis=-1, keepdims=True)
    var = jnp.mean(jnp.square(xf - mu), axis=-1, keepdims=True)
    y = (xf - mu) * lax.rsqrt(var + EPS)
    return (y * g.astype(jnp.float32) + b.astype(jnp.float32)).astype(x.dtype)


def swiglu(h, w_gate, w_up, w_down):
    return (jax.nn.silu(h @ w_gate) * (h @ w_up)) @ w_down


def causal_depthwise_conv(u, w):
    k_w, c = w.shape
    return lax.conv_general_dilated(
        u, w[:, None, :].astype(u.dtype), window_strides=(1,), padding=[(k_w - 1, 0)],
        dimension_numbers=("NWC", "WIO", "NWC"), feature_group_count=c)


def diff_attention(q, k, v, lam, slopes):
    b, t = q.shape[0], q.shape[1]
    n_blocks = t // Q_BLOCK
    kpos = jnp.arange(t)
    scale = ATT_DK ** -0.5

    def block(i):
        start = i * Q_BLOCK
        qb = lax.dynamic_slice_in_dim(q, start, Q_BLOCK, axis=1)
        s = jnp.einsum("bqhcd,bkhcd->bhcqk", qb, k).astype(jnp.float32) * scale
        dist = (start + jnp.arange(Q_BLOCK))[:, None] - kpos[None, :]
        bias = -slopes[:, None, None] * dist.astype(jnp.float32)[None]
        s = jnp.where((dist >= 0)[None, None, None], s + bias[None, :, None], -jnp.inf)
        p = jax.nn.softmax(s, axis=-1)
        a = p[:, :, 0] - lam * p[:, :, 1]
        return jnp.einsum("bhqk,bkhd->bqhd", a.astype(v.dtype), v)

    out = lax.map(block, jnp.arange(n_blocks))
    return jnp.moveaxis(out, 0, 1).reshape(b, t, ATT_HEADS, ATT_DV)


def setup_inputs(seed: int = 0) -> dict:
    key = jax.random.key(seed)
    ks = iter(jax.random.split(key, 40))

    def nrm(shape, scale):
        return jax.random.normal(next(ks), shape, jnp.float32) * scale

    def gain(shape):
        return 1.0 + nrm(shape, 0.02)

    L, D, F = DEPTH, D_MODEL, D_FF
    return {
        "x": nrm((BATCH, SEQ, D), 1.0),
        "mem": nrm((BATCH, MEM_LEN, D), 1.0),
        "ffn1_norm": gain((L, D)),
        "ffn1_w_gate": nrm((L, D, F), D ** -0.5),
        "ffn1_w_up": nrm((L, D, F), D ** -0.5),
        "ffn1_w_down": nrm((L, F, D), F ** -0.5),
        "mix_norm": gain((L, D)),
        "w_in": nrm((L, D, D_IN), D ** -0.5),
        "lam_q1": nrm((L, ATT_DK), 0.1),
        "lam_k1": nrm((L, ATT_DK), 0.1),
        "lam_q2": nrm((L, ATT_DK), 0.1),
        "lam_k2": nrm((L, ATT_DK), 0.1),
        "diff_subln": gain((L, ATT_DV)),
        "cc_dw": nrm((L, CC_K, CC_CH), CC_K ** -0.5),
        "cc_dw_b": nrm((L, CC_CH), 0.01),
        "cc_ln_g": gain((L, CC_CH)),
        "cc_ln_b": nrm((L, CC_CH), 0.01),
        "sc_dw": nrm((L, SC_K, SC_CH), SC_K ** -0.5),
        "w_out": nrm((L, D_MIX, D), D_MIX ** -0.5),
        "xa_norm": gain((L, D)),
        "mem_norm": gain((L, D)),
        "xa_wq": nrm((L, D, XA_HEADS * XA_HD), D ** -0.5),
        "xa_wkv": nrm((L, D, 2 * XA_HEADS * XA_HD), D ** -0.5),
        "xa_wo": nrm((L, XA_HEADS * XA_HD, D), D ** -0.5),
        "ffn2_norm": gain((L, D)),
        "ffn2_w_gate": nrm((L, D, F), D ** -0.5),
        "ffn2_w_up": nrm((L, D, F), D ** -0.5),
        "ffn2_w_down": nrm((L, F, D), F ** -0.5),
        "final_norm": gain((D,)),
    }


def reference(x, mem, ffn1_norm, ffn1_w_gate, ffn1_w_up, ffn1_w_down, mix_norm, w_in,
              lam_q1, lam_k1, lam_q2, lam_k2, diff_subln, cc_dw, cc_dw_b, cc_ln_g, cc_ln_b,
              sc_dw, w_out, xa_norm, mem_norm, xa_wq, xa_wkv, xa_wo, ffn2_norm, ffn2_w_gate,
              ffn2_w_up, ffn2_w_down, final_norm):
    b, t, _ = x.shape
    slopes = 2.0 ** (-8.0 * jnp.arange(1, ATT_HEADS + 1, dtype=jnp.float32) / ATT_HEADS)

    for l in range(DEPTH):
        x = x + 0.5 * swiglu(rms_norm(x, ffn1_norm[l]), ffn1_w_gate[l], ffn1_w_up[l], ffn1_w_down[l])

        z = rms_norm(x, mix_norm[l]) @ w_in[l]
        zq, zk, zv, zc, zs = jnp.split(
            z, np.cumsum([Q_COLS, K_COLS, V_COLS, CC_COLS])[:].tolist(), axis=-1)

        lam_init = 0.8 - 0.6 * float(np.exp(-0.3 * l))
        lam = (jnp.exp(jnp.sum(lam_q1[l].astype(jnp.float32) * lam_k1[l].astype(jnp.float32)))
               - jnp.exp(jnp.sum(lam_q2[l].astype(jnp.float32) * lam_k2[l].astype(jnp.float32)))
               + lam_init)
        q = zq.reshape(b, t, ATT_HEADS, 2, ATT_DK)
        k = zk.reshape(b, t, ATT_HEADS, 2, ATT_DK)
        v = zv.reshape(b, t, ATT_HEADS, ATT_DV)
        o_att = diff_attention(q, k, v, lam, slopes)
        o_att = (rms_norm(o_att, diff_subln[l]) * (1.0 - lam_init)).reshape(b, t, D_ATT)

        ca, cg = jnp.split(zc, 2, axis=-1)
        u = ca * jax.nn.sigmoid(cg)
        u = causal_depthwise_conv(u, cc_dw[l]) + cc_dw_b[l]
        o_cc = jax.nn.silu(layer_norm(u, cc_ln_g[l], cc_ln_b[l]))

        gb, gc, hs = jnp.split(zs, 3, axis=-1)
        o_sc = gb * causal_depthwise_conv(gc * hs, sc_dw[l])

        x = x + jnp.concatenate([o_att, o_cc, o_sc], axis=-1) @ w_out[l]

        hq = (rms_norm(x, xa_norm[l]) @ xa_wq[l]).reshape(b, t, XA_HEADS, XA_HD)
        kv = rms_norm(mem, mem_norm[l]) @ xa_wkv[l]
        mk, mv = jnp.split(kv.reshape(b, MEM_LEN, 2, XA_HEADS, XA_HD), 2, axis=2)
        mk, mv = mk[:, :, 0], mv[:, :, 0]
        s = jnp.einsum("bqhd,bmhd->bhqm", hq, mk).astype(jnp.float32) * (XA_HD ** -0.5)
        p = jax.nn.softmax(s, axis=-1).astype(mv.dtype)
        o = jnp.einsum("bhqm,bmhd->bqhd", p, mv).reshape(b, t, XA_HEADS * XA_HD)
        x = x + o @ xa_wo[l]

        x = x + 0.5 * swiglu(rms_norm(x, ffn2_norm[l]), ffn2_w_gate[l], ffn2_w_up[l], ffn2_w_down[l])

    return rms_norm(x, final_norm)
```

```python
import functools
import math

import jax
import jax.numpy as jnp
from jax import lax
from jax.experimental import pallas as pl
from jax.experimental.pallas import tpu as pltpu

D_MODEL = 1024
DEPTH = 2
ATT_HEADS = 4
ATT_DV = 128
ATT_DK = 64
CC_CH = 256
CC_K = 31
SC_CH = 256
SC_K = 3
D_FF = 2816
MEM_LEN = 256
XA_HEADS = 4
XA_HD = 256
EPS = 1e-6

F32 = jnp.float32
BF16 = jnp.bfloat16

TOKEN_TILE = 512
ATT_TILE = 512
CONV_ROWS = 64
CC_HALO = 32
SC_HALO = 8
VMEM_LIMIT = 56 * 1024 * 1024
MASK_VALUE = -1e30

_NT = (((1,), (1,)), ((), ()))


def _rms(x, g):
    return x * lax.rsqrt(jnp.mean(x * x, axis=-1, keepdims=True) + EPS) * g


def _const_spec(shape):
    nd = len(shape)
    return pl.BlockSpec(shape, lambda *_: (0,) * nd, pipeline_mode=pl.Buffered(1))


def _params(n_axes):
    return pltpu.CompilerParams(
        dimension_semantics=("arbitrary",) * n_axes, vmem_limit_bytes=VMEM_LIMIT)


def _ffn_kernel(x_ref, g_ref, wg_ref, wu_ref, wd_ref, *rest, final):
    if final:
        fg_ref, o_ref = rest
    else:
        (o_ref,) = rest
    x = x_ref[...]
    h = _rms(x, g_ref[...]).astype(BF16)
    gate = jnp.dot(h, wg_ref[...], preferred_element_type=F32)
    up = jnp.dot(h, wu_ref[...], preferred_element_type=F32)
    act = (gate * jax.nn.sigmoid(gate) * up).astype(BF16)
    y = x + 0.5 * jnp.dot(act, wd_ref[...], preferred_element_type=F32)
    if final:
        y = _rms(y, fg_ref[...])
    o_ref[...] = y


def _ffn(x, g, wg, wu, wd, final_g=None):
    n, d = x.shape
    tm = TOKEN_TILE
    final = final_g is not None
    row_spec = pl.BlockSpec((tm, d), lambda i: (i, 0))
    in_specs = [row_spec, _const_spec((1, d)), _const_spec(wg.shape), _const_spec(wu.shape),
                _const_spec(wd.shape)]
    args = [x, g, wg, wu, wd]
    if final:
        in_specs.append(_const_spec((1, d)))
        args.append(final_g)
    return pl.pallas_call(
        functools.partial(_ffn_kernel, final=final),
        grid=(n // tm,),
        in_specs=in_specs,
        out_specs=row_spec,
        out_shape=jax.ShapeDtypeStruct((n, d), F32),
        compiler_params=_params(1),
        name="ffn_final" if final else "ffn",
    )(*args)


def _dw_conv(ext_ref, w_ref, out_ref, *, taps, halo, rows):
    base = halo - (taps - 1)
    for r in range(rows // CONV_ROWS):
        acc = None
        for j in range(taps):
            term = ext_ref[pl.ds(base + j + r * CONV_ROWS, CONV_ROWS), :] * w_ref[pl.ds(j, 1), :]
            acc = term if acc is None else acc + term
        out_ref[pl.ds(r * CONV_ROWS, CONV_ROWS), :] = acc


def _mix_in_kernel(x_ref, g_ref, wq_ref, wk_ref, wvt_ref, wc_ref, ws_ref,
                   ccw_ref, ccb_ref, lng_ref, lnb_ref, scw_ref,
                   q_ref, k_ref, vt_ref, oc_ref,
                   ccx_ref, scx_ref, ccy_ref, scy_ref):
    tm = x_ref.shape[0]
    first = pl.program_id(1) == 0

    @pl.when(first)
    def _():
        ccx_ref[pl.ds(0, CC_HALO), :] = jnp.zeros((CC_HALO, CC_CH), F32)
        scx_ref[pl.ds(0, SC_HALO), :] = jnp.zeros((SC_HALO, SC_CH), F32)

    @pl.when(jnp.logical_not(first))
    def _():
        ccx_ref[pl.ds(0, CC_HALO), :] = ccx_ref[pl.ds(tm, CC_HALO), :]
        scx_ref[pl.ds(0, SC_HALO), :] = scx_ref[pl.ds(tm, SC_HALO), :]

    h = _rms(x_ref[...], g_ref[...]).astype(BF16)
    q_ref[...] = jnp.dot(h, wq_ref[...], preferred_element_type=F32).astype(BF16)
    k_ref[...] = jnp.dot(h, wk_ref[...], preferred_element_type=F32).astype(BF16)
    vt_ref[...] = lax.dot_general(wvt_ref[...], h, _NT, preferred_element_type=F32).astype(BF16)

    zc = jnp.dot(h, wc_ref[...], preferred_element_type=F32)
    ccx_ref[pl.ds(CC_HALO, tm), :] = zc[:, :CC_CH] * jax.nn.sigmoid(zc[:, CC_CH:])
    zs = jnp.dot(h, ws_ref[...], preferred_element_type=F32)
    scx_ref[pl.ds(SC_HALO, tm), :] = zs[:, SC_CH:2 * SC_CH] * zs[:, 2 * SC_CH:]

    _dw_conv(ccx_ref, ccw_ref, ccy_ref, taps=CC_K, halo=CC_HALO, rows=tm)
    _dw_conv(scx_ref, scw_ref, scy_ref, taps=SC_K, halo=SC_HALO, rows=tm)

    u = ccy_ref[...] + ccb_ref[...]
    mu = jnp.mean(u, axis=-1, keepdims=True)
    var = jnp.mean(jnp.square(u - mu), axis=-1, keepdims=True)
    y = (u - mu) * lax.rsqrt(var + EPS) * lng_ref[...] + lnb_ref[...]
    oc_ref[:, :CC_CH] = (y * jax.nn.sigmoid(y)).astype(BF16)
    oc_ref[:, CC_CH:] = (zs[:, :SC_CH] * scy_ref[...]).astype(BF16)


def _mix_in(x, g, wq, wk, wvt, wc, ws, ccw, ccb, lng, lnb, scw):
    b, t, d = x.shape
    tm = ATT_TILE
    nt = t // tm
    qk_cols = wq.shape[1]
    v_cols = wvt.shape[0]
    row = lambda cols, dt: (pl.BlockSpec((None, tm, cols), lambda i, j: (i, j, 0)),
                            jax.ShapeDtypeStruct((b, t, cols), dt))
    q_spec, q_shape = row(qk_cols, BF16)
    oc_spec, oc_shape = row(CC_CH + SC_CH, BF16)
    vt_spec = pl.BlockSpec((None, None, v_cols, tm), lambda i, j: (i, j, 0, 0))
    vt_shape = jax.ShapeDtypeStruct((b, nt, v_cols, tm), BF16)
    consts = [g, wq, wk, wvt, wc, ws, ccw, ccb, lng, lnb, scw]
    return pl.pallas_call(
        _mix_in_kernel,
        grid=(b, nt),
        in_specs=[pl.BlockSpec((None, tm, d), lambda i, j: (i, j, 0))]
        + [_const_spec(c.shape) for c in consts],
        out_specs=[q_spec, q_spec, vt_spec, oc_spec],
        out_shape=[q_shape, q_shape, vt_shape, oc_shape],
        scratch_shapes=[
            pltpu.VMEM((CC_HALO + tm, CC_CH), F32),
            pltpu.VMEM((SC_HALO + tm, SC_CH), F32),
            pltpu.VMEM((tm, CC_CH), F32),
            pltpu.VMEM((tm, SC_CH), F32),
        ],
        compiler_params=_params(2),
        name="mix_in",
    )(x, *consts)


def _diff_attn_kernel(lam_ref, subln_ref, q_ref, k_ref, vt_ref, o_ref,
                      qs_ref, bias_ref, dbias_ref, m_ref, l_ref, acc_ref, *, lam_init):
    tq = q_ref.shape[0]
    tk = tq
    head = pl.program_id(1)
    qi = pl.program_id(2)
    slope = jnp.left_shift(1, 2 * (ATT_HEADS - 1 - head)).astype(F32) * (1.0 / 256.0)

    @pl.when(qi == 0)
    def _():
        jj = lax.broadcasted_iota(jnp.int32, (tk, 2 * tq), 0)
        ii = lax.broadcasted_iota(jnp.int32, (tk, 2 * tq), 1)
        ii = jnp.where(ii >= tq, ii - tq, ii)
        bias = jj.astype(F32) * slope
        bias_ref[...] = bias
        dbias_ref[...] = jnp.where(jj <= ii, bias, MASK_VALUE)

    q = q_ref[...]
    lane = lax.broadcasted_iota(jnp.int32, q.shape, 1)
    zero = jnp.zeros_like(q)
    scale = ATT_DK ** -0.5
    qs_ref[pl.ds(0, tq), :] = jnp.where(lane < ATT_DK, q, zero) * scale
    qs_ref[pl.ds(tq, tq), :] = jnp.where(lane >= ATT_DK, q, zero) * scale

    m_ref[...] = jnp.full(m_ref.shape, MASK_VALUE, F32)
    l_ref[...] = jnp.zeros(l_ref.shape, F32)
    acc_ref[...] = jnp.zeros(acc_ref.shape, F32)

    def step(c, bias_tile_ref):
        j0 = pl.multiple_of(c * tk, tk)
        s = lax.dot_general(k_ref[pl.ds(j0, tk), :], qs_ref[...], _NT,
                            preferred_element_type=F32)
        s = s + bias_tile_ref[...]
        shift = slope * j0.astype(F32)
        m_prev = m_ref[...] - shift
        m_new = jnp.maximum(m_prev, jnp.max(s, axis=0, keepdims=True))
        alpha = jnp.exp(m_prev - m_new)
        p = jnp.exp(s - m_new)
        l_ref[...] = alpha * l_ref[...] + jnp.sum(p, axis=0, keepdims=True)
        m_ref[...] = m_new + shift
        pv = jnp.dot(vt_ref[c], p.astype(BF16), preferred_element_type=F32)
        acc_ref[...] = alpha * acc_ref[...] + pv

    def body(c, carry):
        step(c, bias_ref)
        return carry

    lax.fori_loop(0, qi, body, 0)
    step(qi, dbias_ref)

    lam_p = lam_ref[...]
    lam = (jnp.exp(jnp.sum(lam_p[0:1] * lam_p[1:2], axis=-1, keepdims=True))
           - jnp.exp(jnp.sum(lam_p[2:3] * lam_p[3:4], axis=-1, keepdims=True)) + lam_init)
    inv_l = 1.0 / l_ref[...]
    acc = acc_ref[...]
    o = acc[:, :tq] * inv_l[:, :tq] - lam * (acc[:, tq:] * inv_l[:, tq:])
    o = o.T
    o_ref[...] = (_rms(o, subln_ref[...]) * (1.0 - lam_init)).astype(BF16)


def _diff_attn(lam_p, subln, q, k, vt, lam_init):
    b, t, _ = q.shape
    tq = ATT_TILE
    nt = t // tq
    return pl.pallas_call(
        functools.partial(_diff_attn_kernel, lam_init=lam_init),
        grid=(b, ATT_HEADS, nt),
        in_specs=[
            _const_spec(lam_p.shape),
            _const_spec(subln.shape),
            pl.BlockSpec((None, tq, ATT_DV), lambda i, h, j: (i, j, h)),
            pl.BlockSpec((None, t, ATT_DV), lambda i, h, j: (i, 0, h)),
            pl.BlockSpec((None, nt, ATT_DV, tq), lambda i, h, j: (i, 0, h, 0)),
        ],
        out_specs=pl.BlockSpec((None, tq, ATT_DV), lambda i, h, j: (i, j, h)),
        out_shape=jax.ShapeDtypeStruct((b, t, ATT_HEADS * ATT_DV), BF16),
        scratch_shapes=[
            pltpu.VMEM((2 * tq, ATT_DV), BF16),
            pltpu.VMEM((tq, 2 * tq), F32),
            pltpu.VMEM((tq, 2 * tq), F32),
            pltpu.VMEM((1, 2 * tq), F32),
            pltpu.VMEM((1, 2 * tq), F32),
            pltpu.VMEM((ATT_DV, 2 * tq), F32),
        ],
        compiler_params=_params(3),
        name="diff_attn",
    )(lam_p, subln, q, k, vt)


def _mem_kv_kernel(mem_ref, g_ref, w_ref, kv_ref):
    h = _rms(mem_ref[...], g_ref[...]).astype(BF16)
    kv_ref[...] = jnp.dot(h, w_ref[...], preferred_element_type=F32).astype(BF16)


def _mem_kv(mem, g, w):
    b, m, d = mem.shape
    n = w.shape[1]
    return pl.pallas_call(
        _mem_kv_kernel,
        grid=(b,),
        in_specs=[pl.BlockSpec((None, m, d), lambda i: (i, 0, 0)), _const_spec(g.shape),
                  _const_spec(w.shape)],
        out_specs=pl.BlockSpec((None, m, n), lambda i: (i, 0, 0)),
        out_shape=jax.ShapeDtypeStruct((b, m, n), BF16),
        compiler_params=_params(1),
        name="mem_kv",
    )(mem, g, w)


def _mix_out_kernel(x_ref, oa_ref, oc_ref, wo_ref, g_ref, wq_ref, kv_ref, wxo_ref, o_ref):
    n_att = oa_ref.shape[1]
    x = x_ref[...]
    x = x + jnp.dot(oa_ref[...], wo_ref[pl.ds(0, n_att), :], preferred_element_type=F32)
    x = x + jnp.dot(oc_ref[...], wo_ref[pl.ds(n_att, oc_ref.shape[1]), :],
                    preferred_element_type=F32)

    hq = jnp.dot(_rms(x, g_ref[...]).astype(BF16), wq_ref[...], preferred_element_type=F32)
    hq = (hq * (XA_HD ** -0.5)).astype(BF16)
    kd = XA_HEADS * XA_HD
    heads = []
    for h in range(XA_HEADS):
        sl = slice(h * XA_HD, (h + 1) * XA_HD)
        s = lax.dot_general(hq[:, sl], kv_ref[:, sl], _NT, preferred_element_type=F32)
        p = jnp.exp(s - jnp.max(s, axis=-1, keepdims=True))
        inv = 1.0 / jnp.sum(p, axis=-1, keepdims=True)
        o = jnp.dot(p.astype(BF16), kv_ref[:, kd + h * XA_HD:kd + (h + 1) * XA_HD],
                    preferred_element_type=F32)
        heads.append((o * inv).astype(BF16))
    o = jnp.concatenate(heads, axis=-1)
    o_ref[...] = x + jnp.dot(o, wxo_ref[...], preferred_element_type=F32)


def _mix_out(x, oa, oc, wo, g, wq, kv, wxo):
    b, t, d = x.shape
    tm = TOKEN_TILE
    row = lambda cols: pl.BlockSpec((None, tm, cols), lambda i, j: (i, j, 0))
    return pl.pallas_call(
        _mix_out_kernel,
        grid=(b, t // tm),
        in_specs=[row(d), row(oa.shape[2]), row(oc.shape[2]), _const_spec(wo.shape),
                  _const_spec(g.shape), _const_spec(wq.shape),
                  pl.BlockSpec((None,) + kv.shape[1:], lambda i, j: (i, 0, 0)),
                  _const_spec(wxo.shape)],
        out_specs=row(d),
        out_shape=jax.ShapeDtypeStruct((b, t, d), F32),
        compiler_params=_params(2),
        name="mix_out",
    )(x, oa, oc, wo, g, wq, kv, wxo)


def kernel(x, mem, ffn1_norm, ffn1_w_gate, ffn1_w_up, ffn1_w_down, mix_norm, w_in, lam_q1, lam_k1, lam_q2, lam_k2, diff_subln, cc_dw, cc_dw_b, cc_ln_g, cc_ln_b, sc_dw, w_out, xa_norm, mem_norm, xa_wq, xa_wkv, xa_wo, ffn2_norm, ffn2_w_gate, ffn2_w_up, ffn2_w_down, final_norm):
    b, t, d = x.shape
    n_qk = ATT_HEADS * 2 * ATT_DK
    n_v = ATT_HEADS * ATT_DV
    bf = lambda w: w.astype(BF16)
    vec = lambda v: v.reshape(1, -1).astype(F32)

    for l in range(DEPTH):
        x = _ffn(x.reshape(b * t, d), vec(ffn1_norm[l]), bf(ffn1_w_gate[l]), bf(ffn1_w_up[l]),
                 bf(ffn1_w_down[l])).reshape(b, t, d)

        wi = w_in[l]
        q, k, vt, oc = _mix_in(
            x, vec(mix_norm[l]),
            bf(wi[:, :n_qk]), bf(wi[:, n_qk:2 * n_qk]), bf(wi[:, 2 * n_qk:2 * n_qk + n_v].T),
            bf(wi[:, 2 * n_qk + n_v:2 * n_qk + n_v + 2 * CC_CH]), bf(wi[:, 2 * n_qk + n_v + 2 * CC_CH:]),
            cc_dw[l].astype(F32), vec(cc_dw_b[l]), vec(cc_ln_g[l]), vec(cc_ln_b[l]),
            sc_dw[l].astype(F32))

        lam_init = 0.8 - 0.6 * math.exp(-0.3 * l)
        lam_p = jnp.stack([lam_q1[l], lam_k1[l], lam_q2[l], lam_k2[l]]).astype(F32)
        oa = _diff_attn(lam_p, vec(diff_subln[l]), q, k, vt, lam_init)

        kv = _mem_kv(mem, vec(mem_norm[l]), bf(xa_wkv[l]))
        x = _mix_out(x, oa, oc, bf(w_out[l]), vec(xa_norm[l]), bf(xa_wq[l]), kv, bf(xa_wo[l]))

        last = l == DEPTH - 1
        x = _ffn(x.reshape(b * t, d), vec(ffn2_norm[l]), bf(ffn2_w_gate[l]), bf(ffn2_w_up[l]),
                 bf(ffn2_w_down[l]), vec(final_norm) if last else None).reshape(b, t, d)
    return x
```

```python
import functools
import math

import jax
import jax.numpy as jnp
from jax import lax
from jax.experimental import pallas as pl
from jax.experimental.pallas import tpu as pltpu

D_MODEL = 1024
DEPTH = 2
ATT_HEADS = 4
ATT_DV = 128
ATT_DK = 64
CC_CH = 256
CC_K = 31
SC_CH = 256
SC_K = 3
D_FF = 2816
MEM_LEN = 256
XA_HEADS = 4
XA_HD = 256
EPS = 1e-6

F32 = jnp.float32
BF16 = jnp.bfloat16

TOKEN_TILE = 512
ATT_TILE = 512
VT_ROWS = ATT_DV + 16
LOG2E = math.log2(math.e)
ATT_LANES = 256
CONV_ROWS = 64
CC_HALO = 32
SC_HALO = 8
VMEM_LIMIT = 56 * 1024 * 1024
MASK_VALUE = -1e30

_NT = (((1,), (1,)), ((), ()))


def _rms(x, g):
    return x * lax.rsqrt(jnp.mean(x * x, axis=-1, keepdims=True) + EPS) * g


def _const_spec(shape):
    nd = len(shape)
    return pl.BlockSpec(shape, lambda *_: (0,) * nd, pipeline_mode=pl.Buffered(1))


def _params(n_axes):
    return pltpu.CompilerParams(
        dimension_semantics=("arbitrary",) * n_axes, vmem_limit_bytes=VMEM_LIMIT)


def _ffn_kernel(x_ref, g_ref, wg_ref, wu_ref, wd_ref, *rest, final):
    if final:
        fg_ref, o_ref = rest
    else:
        (o_ref,) = rest
    x = x_ref[...]
    h = _rms(x, g_ref[...]).astype(BF16)
    gate = jnp.dot(h, wg_ref[...], preferred_element_type=F32)
    up = jnp.dot(h, wu_ref[...], preferred_element_type=F32)
    act = (gate * jax.nn.sigmoid(gate) * up).astype(BF16)
    y = x + 0.5 * jnp.dot(act, wd_ref[...], preferred_element_type=F32)
    if final:
        y = _rms(y, fg_ref[...])
    o_ref[...] = y


def _ffn(x, g, wg, wu, wd, final_g=None):
    n, d = x.shape
    tm = TOKEN_TILE
    final = final_g is not None
    row_spec = pl.BlockSpec((tm, d), lambda i: (i, 0))
    in_specs = [row_spec, _const_spec((1, d)), _const_spec(wg.shape), _const_spec(wu.shape),
                _const_spec(wd.shape)]
    args = [x, g, wg, wu, wd]
    if final:
        in_specs.append(_const_spec((1, d)))
        args.append(final_g)
    return pl.pallas_call(
        functools.partial(_ffn_kernel, final=final),
        grid=(n // tm,),
        in_specs=in_specs,
        out_specs=row_spec,
        out_shape=jax.ShapeDtypeStruct((n, d), F32),
        compiler_params=_params(1),
        name="ffn_final" if final else "ffn",
    )(*args)


def _dw_conv(ext_ref, w_ref, out_ref, *, taps, halo, rows):
    base = halo - (taps - 1)
    for r in range(rows // CONV_ROWS):
        acc = None
        for j in range(taps):
            term = ext_ref[pl.ds(base + j + r * CONV_ROWS, CONV_ROWS), :] * w_ref[pl.ds(j, 1), :]
            acc = term if acc is None else acc + term
        out_ref[pl.ds(r * CONV_ROWS, CONV_ROWS), :] = acc


def _mix_in_kernel(x_ref, g_ref, wq_ref, wk_ref, wvt_ref, wc_ref, ws_ref,
                   ccw_ref, ccb_ref, lng_ref, lnb_ref, scw_ref,
                   q_ref, k_ref, vt_ref, oc_ref,
                   ccx_ref, scx_ref, ccy_ref, scy_ref):
    tm = x_ref.shape[0]
    first = pl.program_id(1) == 0

    @pl.when(first)
    def _():
        ccx_ref[pl.ds(0, CC_HALO), :] = jnp.zeros((CC_HALO, CC_CH), F32)
        scx_ref[pl.ds(0, SC_HALO), :] = jnp.zeros((SC_HALO, SC_CH), F32)

    @pl.when(jnp.logical_not(first))
    def _():
        ccx_ref[pl.ds(0, CC_HALO), :] = ccx_ref[pl.ds(tm, CC_HALO), :]
        scx_ref[pl.ds(0, SC_HALO), :] = scx_ref[pl.ds(tm, SC_HALO), :]

    h = _rms(x_ref[...], g_ref[...]).astype(BF16)
    q = jnp.dot(h, wq_ref[...], preferred_element_type=F32)
    q_ref[...] = (q * (LOG2E * ATT_DK ** -0.5)).astype(BF16)
    k_ref[...] = jnp.dot(h, wk_ref[...], preferred_element_type=F32).astype(BF16)
    vt = lax.dot_general(wvt_ref[...], h, _NT, preferred_element_type=F32).astype(BF16)
    for hd in range(ATT_HEADS):
        vt_ref[pl.ds(hd * VT_ROWS, ATT_DV), :] = vt[hd * ATT_DV:(hd + 1) * ATT_DV]
        vt_ref[pl.ds(hd * VT_ROWS + ATT_DV, VT_ROWS - ATT_DV), :] = jnp.ones(
            (VT_ROWS - ATT_DV, tm), BF16)

    zc = jnp.dot(h, wc_ref[...], preferred_element_type=F32)
    ccx_ref[pl.ds(CC_HALO, tm), :] = zc[:, :CC_CH] * jax.nn.sigmoid(zc[:, CC_CH:])
    zs = jnp.dot(h, ws_ref[...], preferred_element_type=F32)
    scx_ref[pl.ds(SC_HALO, tm), :] = zs[:, SC_CH:2 * SC_CH] * zs[:, 2 * SC_CH:]

    _dw_conv(ccx_ref, ccw_ref, ccy_ref, taps=CC_K, halo=CC_HALO, rows=tm)
    _dw_conv(scx_ref, scw_ref, scy_ref, taps=SC_K, halo=SC_HALO, rows=tm)

    u = ccy_ref[...] + ccb_ref[...]
    mu = jnp.mean(u, axis=-1, keepdims=True)
    var = jnp.mean(jnp.square(u - mu), axis=-1, keepdims=True)
    y = (u - mu) * lax.rsqrt(var + EPS) * lng_ref[...] + lnb_ref[...]
    oc_ref[:, :CC_CH] = (y * jax.nn.sigmoid(y)).astype(BF16)
    oc_ref[:, CC_CH:] = (zs[:, :SC_CH] * scy_ref[...]).astype(BF16)


def _mix_in(x, g, wq, wk, wvt, wc, ws, ccw, ccb, lng, lnb, scw):
    b, t, d = x.shape
    tm = ATT_TILE
    nt = t // tm
    qk_cols = wq.shape[1]
    v_cols = ATT_HEADS * VT_ROWS
    row = lambda cols, dt: (pl.BlockSpec((None, tm, cols), lambda i, j: (i, j, 0)),
                            jax.ShapeDtypeStruct((b, t, cols), dt))
    q_spec, q_shape = row(qk_cols, BF16)
    oc_spec, oc_shape = row(CC_CH + SC_CH, BF16)
    vt_spec = pl.BlockSpec((None, None, v_cols, tm), lambda i, j: (i, j, 0, 0))
    vt_shape = jax.ShapeDtypeStruct((b, nt, v_cols, tm), BF16)
    consts = [g, wq, wk, wvt, wc, ws, ccw, ccb, lng, lnb, scw]
    return pl.pallas_call(
        _mix_in_kernel,
        grid=(b, nt),
        in_specs=[pl.BlockSpec((None, tm, d), lambda i, j: (i, j, 0))]
        + [_const_spec(c.shape) for c in consts],
        out_specs=[q_spec, q_spec, vt_spec, oc_spec],
        out_shape=[q_shape, q_shape, vt_shape, oc_shape],
        scratch_shapes=[
            pltpu.VMEM((CC_HALO + tm, CC_CH), F32),
            pltpu.VMEM((SC_HALO + tm, SC_CH), F32),
            pltpu.VMEM((tm, CC_CH), F32),
            pltpu.VMEM((tm, SC_CH), F32),
        ],
        compiler_params=_params(2),
        name="mix_in",
    )(x, *consts)


def _diff_attn_kernel(lam_ref, subln_ref, q_ref, k_ref, vt_ref, o_ref,
                      qs_ref, bias_ref, sa_ref, sb_ref, ca_ref, cb_ref, m_ref, acc_ref,
                      *, lam_init):
    tq = q_ref.shape[0]
    tk = tq
    head = pl.program_id(1)
    qi = pl.program_id(2)
    slope = jnp.left_shift(1, 2 * (ATT_HEADS - 1 - head)).astype(F32) * (LOG2E / 256.0)

    @pl.when(qi == 0)
    def _():
        jj = lax.broadcasted_iota(jnp.int32, (tk, 2 * tq), 0)
        ii = lax.broadcasted_iota(jnp.int32, (tk, 2 * tq), 1)
        ii = jnp.where(ii >= tq, ii - tq, ii)
        bias = jj.astype(F32) * slope
        bias_ref[0] = bias
        bias_ref[1] = jnp.where(jj <= ii, bias, MASK_VALUE)

    q = q_ref[...]
    lane = lax.broadcasted_iota(jnp.int32, q.shape, 1)
    zero = jnp.zeros_like(q)
    qs_ref[pl.ds(0, tq), :] = jnp.where(lane < ATT_DK, q, zero)
    qs_ref[pl.ds(tq, tq), :] = jnp.where(lane >= ATT_DK, q, zero)

    m_ref[...] = jnp.full(m_ref.shape, MASK_VALUE, F32)
    acc_ref[...] = jnp.zeros(acc_ref.shape, F32)

    def scores(c, s_ref, cmax_ref):
        j0 = pl.multiple_of(c * tk, tk)
        s = lax.dot_general(k_ref[pl.ds(j0, tk), :], qs_ref[...], _NT,
                            preferred_element_type=F32)
        s = s + bias_ref[(c == qi).astype(jnp.int32)]
        s_ref[...] = s
        cmax_ref[...] = jnp.max(s, axis=0, keepdims=True)

    def accumulate(c, s_ref, cmax_ref):
        shift = slope * (c * tk).astype(F32)
        vtc = vt_ref[c]
        for nb in range(2 * tq // ATT_LANES):
            cols = pl.ds(nb * ATT_LANES, ATT_LANES)
            m_prev = m_ref[:, cols] - shift
            m_new = jnp.maximum(m_prev, cmax_ref[:, cols])
            alpha = jnp.exp2(m_prev - m_new)
            p = jnp.exp2(s_ref[:, cols] - m_new).astype(BF16)
            m_ref[:, cols] = m_new + shift
            pv = jnp.dot(vtc, p, preferred_element_type=F32)
            acc_ref[:, cols] = alpha * acc_ref[:, cols] + pv

    n_chunks = qi + 1
    scores(0, sa_ref, ca_ref)

    def pair(t, carry):
        c = 2 * t
        scores(c + 1, sb_ref, cb_ref)
        accumulate(c, sa_ref, ca_ref)
        scores(jnp.minimum(c + 2, qi), sa_ref, ca_ref)
        accumulate(c + 1, sb_ref, cb_ref)
        return carry

    lax.fori_loop(0, n_chunks // 2, pair, 0)

    @pl.when(n_chunks % 2 == 1)
    def _():
        accumulate(qi, sa_ref, ca_ref)

    lam_p = lam_ref[...]
    lam = (jnp.exp(jnp.sum(lam_p[0:1] * lam_p[1:2], axis=-1, keepdims=True))
           - jnp.exp(jnp.sum(lam_p[2:3] * lam_p[3:4], axis=-1, keepdims=True)) + lam_init)
    acc = acc_ref[pl.ds(0, ATT_DV), :]
    inv_l = 1.0 / acc_ref[pl.ds(ATT_DV, 1), :]
    o = acc[:, :tq] * inv_l[:, :tq] - lam * (acc[:, tq:] * inv_l[:, tq:])
    o = o.T
    o_ref[...] = (_rms(o, subln_ref[...]) * (1.0 - lam_init)).astype(BF16)


def _diff_attn(lam_p, subln, q, k, vt, lam_init):
    b, t, _ = q.shape
    tq = ATT_TILE
    nt = t // tq
    return pl.pallas_call(
        functools.partial(_diff_attn_kernel, lam_init=lam_init),
        grid=(b, ATT_HEADS, nt),
        in_specs=[
            _const_spec(lam_p.shape),
            _const_spec(subln.shape),
            pl.BlockSpec((None, tq, ATT_DV), lambda i, h, j: (i, j, h)),
            pl.BlockSpec((None, t, ATT_DV), lambda i, h, j: (i, 0, h)),
            pl.BlockSpec((None, nt, VT_ROWS, tq), lambda i, h, j: (i, 0, h, 0)),
        ],
        out_specs=pl.BlockSpec((None, tq, ATT_DV), lambda i, h, j: (i, j, h)),
        out_shape=jax.ShapeDtypeStruct((b, t, ATT_HEADS * ATT_DV), BF16),
        scratch_shapes=[
            pltpu.VMEM((2 * tq, ATT_DV), BF16),
            pltpu.VMEM((2, tq, 2 * tq), F32),
            pltpu.VMEM((tq, 2 * tq), F32),
            pltpu.VMEM((tq, 2 * tq), F32),
            pltpu.VMEM((1, 2 * tq), F32),
            pltpu.VMEM((1, 2 * tq), F32),
            pltpu.VMEM((1, 2 * tq), F32),
            pltpu.VMEM((VT_ROWS, 2 * tq), F32),
        ],
        compiler_params=_params(3),
        name="diff_attn",
    )(lam_p, subln, q, k, vt)


def _mem_kv_kernel(mem_ref, g_ref, w_ref, kv_ref):
    h = _rms(mem_ref[...], g_ref[...]).astype(BF16)
    kv_ref[...] = jnp.dot(h, w_ref[...], preferred_element_type=F32).astype(BF16)


def _mem_kv(mem, g, w):
    b, m, d = mem.shape
    n = w.shape[1]
    return pl.pallas_call(
        _mem_kv_kernel,
        grid=(b,),
        in_specs=[pl.BlockSpec((None, m, d), lambda i: (i, 0, 0)), _const_spec(g.shape),
                  _const_spec(w.shape)],
        out_specs=pl.BlockSpec((None, m, n), lambda i: (i, 0, 0)),
        out_shape=jax.ShapeDtypeStruct((b, m, n), BF16),
        compiler_params=_params(1),
        name="mem_kv",
    )(mem, g, w)


def _mix_out_kernel(x_ref, oa_ref, oc_ref, wo_ref, g_ref, wq_ref, kv_ref, wxo_ref, o_ref):
    n_att = oa_ref.shape[1]
    x = x_ref[...]
    x = x + jnp.dot(oa_ref[...], wo_ref[pl.ds(0, n_att), :], preferred_element_type=F32)
    x = x + jnp.dot(oc_ref[...], wo_ref[pl.ds(n_att, oc_ref.shape[1]), :],
                    preferred_element_type=F32)

    hq = jnp.dot(_rms(x, g_ref[...]).astype(BF16), wq_ref[...], preferred_element_type=F32)
    hq = (hq * (XA_HD ** -0.5)).astype(BF16)
    kd = XA_HEADS * XA_HD
    heads = []
    for h in range(XA_HEADS):
        sl = slice(h * XA_HD, (h + 1) * XA_HD)
        s = lax.dot_general(hq[:, sl], kv_ref[:, sl], _NT, preferred_element_type=F32)
        p = jnp.exp(s - jnp.max(s, axis=-1, keepdims=True))
        inv = 1.0 / jnp.sum(p, axis=-1, keepdims=True)
        o = jnp.dot(p.astype(BF16), kv_ref[:, kd + h * XA_HD:kd + (h + 1) * XA_HD],
                    preferred_element_type=F32)
        heads.append((o * inv).astype(BF16))
    o = jnp.concatenate(heads, axis=-1)
    o_ref[...] = x + jnp.dot(o, wxo_ref[...], preferred_element_type=F32)


def _mix_out(x, oa, oc, wo, g, wq, kv, wxo):
    b, t, d = x.shape
    tm = TOKEN_TILE
    row = lambda cols: pl.BlockSpec((None, tm, cols), lambda i, j: (i, j, 0))
    return pl.pallas_call(
        _mix_out_kernel,
        grid=(b, t // tm),
        in_specs=[row(d), row(oa.shape[2]), row(oc.shape[2]), _const_spec(wo.shape),
                  _const_spec(g.shape), _const_spec(wq.shape),
                  pl.BlockSpec((None,) + kv.shape[1:], lambda i, j: (i, 0, 0)),
                  _const_spec(wxo.shape)],
        out_specs=row(d),
        out_shape=jax.ShapeDtypeStruct((b, t, d), F32),
        compiler_params=_params(2),
        name="mix_out",
    )(x, oa, oc, wo, g, wq, kv, wxo)


def kernel(x, mem, ffn1_norm, ffn1_w_gate, ffn1_w_up, ffn1_w_down, mix_norm, w_in, lam_q1, lam_k1, lam_q2, lam_k2, diff_subln, cc_dw, cc_dw_b, cc_ln_g, cc_ln_b, sc_dw, w_out, xa_norm, mem_norm, xa_wq, xa_wkv, xa_wo, ffn2_norm, ffn2_w_gate, ffn2_w_up, ffn2_w_down, final_norm):
    b, t, d = x.shape
    n_qk = ATT_HEADS * 2 * ATT_DK
    n_v = ATT_HEADS * ATT_DV
    bf = lambda w: w.astype(BF16)
    vec = lambda v: v.reshape(1, -1).astype(F32)

    for l in range(DEPTH):
        x = _ffn(x.reshape(b * t, d), vec(ffn1_norm[l]), bf(ffn1_w_gate[l]), bf(ffn1_w_up[l]),
                 bf(ffn1_w_down[l])).reshape(b, t, d)

        wi = w_in[l]
        q, k, vt, oc = _mix_in(
            x, vec(mix_norm[l]),
            bf(wi[:, :n_qk]), bf(wi[:, n_qk:2 * n_qk]), bf(wi[:, 2 * n_qk:2 * n_qk + n_v].T),
            bf(wi[:, 2 * n_qk + n_v:2 * n_qk + n_v + 2 * CC_CH]), bf(wi[:, 2 * n_qk + n_v + 2 * CC_CH:]),
            cc_dw[l].astype(F32), vec(cc_dw_b[l]), vec(cc_ln_g[l]), vec(cc_ln_b[l]),
            sc_dw[l].astype(F32))

        lam_init = 0.8 - 0.6 * math.exp(-0.3 * l)
        lam_p = jnp.stack([lam_q1[l], lam_k1[l], lam_q2[l], lam_k2[l]]).astype(F32)
        oa = _diff_attn(lam_p, vec(diff_subln[l]), q, k, vt, lam_init)

        kv = _mem_kv(mem, vec(mem_norm[l]), bf(xa_wkv[l]))
        x = _mix_out(x, oa, oc, bf(w_out[l]), vec(xa_norm[l]), bf(xa_wq[l]), kv, bf(xa_wo[l]))

        last = l == DEPTH - 1
        x = _ffn(x.reshape(b * t, d), vec(ffn2_norm[l]), bf(ffn2_w_gate[l]), bf(ffn2_w_up[l]),
                 bf(ffn2_w_down[l]), vec(final_norm) if last else None).reshape(b, t, d)
    return x
```

```python
import functools
import math

import jax
import jax.numpy as jnp
from jax import lax
from jax.experimental import pallas as pl
from jax.experimental.pallas import tpu as pltpu

D_MODEL = 1024
DEPTH = 2
ATT_HEADS = 4
ATT_DV = 128
ATT_DK = 64
CC_CH = 256
CC_K = 31
SC_CH = 256
SC_K = 3
D_FF = 2816
MEM_LEN = 256
XA_HEADS = 4
XA_HD = 256
EPS = 1e-6

F32 = jnp.float32
BF16 = jnp.bfloat16

TOKEN_TILE = 512
ATT_TILE = 512
VT_ROWS = ATT_DV + 16
LOG2E = math.log2(math.e)
ATT_LANES = 256
SUBLANES = 8
CONV_ROWS = 64
CC_HALO = 32
SC_HALO = 8
VMEM_LIMIT = 56 * 1024 * 1024
MASK_VALUE = -1e30

_NT = (((1,), (1,)), ((), ()))


def _rms(x, g):
    return x * lax.rsqrt(jnp.mean(x * x, axis=-1, keepdims=True) + EPS) * g


def _const_spec(shape):
    nd = len(shape)
    return pl.BlockSpec(shape, lambda *_: (0,) * nd, pipeline_mode=pl.Buffered(1))


def _params(n_axes):
    return pltpu.CompilerParams(
        dimension_semantics=("arbitrary",) * n_axes, vmem_limit_bytes=VMEM_LIMIT)


def _ffn_kernel(x_ref, g_ref, wg_ref, wu_ref, wd_ref, *rest, final):
    if final:
        fg_ref, o_ref = rest
    else:
        (o_ref,) = rest
    x = x_ref[...]
    h = _rms(x, g_ref[...]).astype(BF16)
    gate = jnp.dot(h, wg_ref[...], preferred_element_type=F32)
    up = jnp.dot(h, wu_ref[...], preferred_element_type=F32)
    act = (gate * jax.nn.sigmoid(gate) * up).astype(BF16)
    y = x + 0.5 * jnp.dot(act, wd_ref[...], preferred_element_type=F32)
    if final:
        y = _rms(y, fg_ref[...])
    o_ref[...] = y


def _ffn(x, g, wg, wu, wd, final_g=None):
    n, d = x.shape
    tm = TOKEN_TILE
    final = final_g is not None
    row_spec = pl.BlockSpec((tm, d), lambda i: (i, 0))
    in_specs = [row_spec, _const_spec((1, d)), _const_spec(wg.shape), _const_spec(wu.shape),
                _const_spec(wd.shape)]
    args = [x, g, wg, wu, wd]
    if final:
        in_specs.append(_const_spec((1, d)))
        args.append(final_g)
    return pl.pallas_call(
        functools.partial(_ffn_kernel, final=final),
        grid=(n // tm,),
        in_specs=in_specs,
        out_specs=row_spec,
        out_shape=jax.ShapeDtypeStruct((n, d), F32),
        compiler_params=_params(1),
        name="ffn_final" if final else "ffn",
    )(*args)


def _shifted_copies(ext_ref, shift_ref):
    rows = shift_ref.shape[1]
    for b in range(1, SUBLANES):
        shift_ref[b - 1] = ext_ref[pl.ds(b, rows), :]


def _dw_conv(ext_ref, w_ref, out_ref, *, taps, halo, blocks, shift_ref=None):
    base = halo - (taps - 1)
    folded = None
    for r in blocks:
        acc = None
        for j in range(taps):
            off = base + j
            if shift_ref is None or off % SUBLANES == 0:
                rows = ext_ref[pl.ds(off + r * CONV_ROWS, CONV_ROWS), :]
            else:
                rows = shift_ref[off % SUBLANES - 1,
                                 pl.ds(off - off % SUBLANES + r * CONV_ROWS, CONV_ROWS), :]
            term = rows * w_ref[pl.ds(j, 1), :]
            acc = term if acc is None else acc + term
        out_ref[pl.ds(r * CONV_ROWS, CONV_ROWS), :] = acc
        bits = pltpu.bitcast(acc, jnp.uint32)
        for i in range(0, CONV_ROWS, 8):
            for c in range(0, bits.shape[1], 128):
                tile = bits[i:i + 8, c:c + 128]
                folded = tile if folded is None else folded | tile
    return folded


CC_BLOCK_SPLIT = ((0,), (1,), (2, 3), (4, 5), (6, 7))


def _order_after(dst_ref, bits):
    half = jnp.uint32(16)
    zero = pltpu.bitcast(lax.shift_right_logical(lax.shift_right_logical(bits, half), half), F32)
    zero = jnp.concatenate([zero, zero], axis=0).astype(dst_ref.dtype)
    dst_ref[...] = dst_ref[...] + zero


def _mix_in_kernel(x_ref, g_ref, wq_ref, wk_ref, wvt_ref, wc_ref, ws_ref,
                   ccw_ref, ccb_ref, lng_ref, lnb_ref, scw_ref,
                   q_ref, k_ref, vt_ref, oc_ref,
                   h_ref, ccx_ref, scx_ref, gate_ref, ccn_ref, scn_ref, gaten_ref, ccs_ref,
                   ccy_ref, scy_ref, *, tiles_per_seq):
    tm = x_ref.shape[0]
    g = pl.program_id(0)
    cc_conv = functools.partial(_dw_conv, ccx_ref, ccw_ref, ccy_ref, taps=CC_K, halo=CC_HALO,
                                shift_ref=ccs_ref)

    def cc_conv_then_next_projection(i):
        done = cc_conv(blocks=CC_BLOCK_SPLIT[i])
        _order_after(h_ref.at[pl.ds(0, 16), pl.ds(0, 128)], done)

    @pl.when(g == 0)
    def _():
        ccx_ref[...] = jnp.zeros(ccx_ref.shape, F32)
        scx_ref[...] = jnp.zeros(scx_ref.shape, F32)
        gate_ref[...] = jnp.zeros(gate_ref.shape, F32)

    h_ref[...] = _rms(x_ref[...], g_ref[...]).astype(BF16)
    _shifted_copies(ccx_ref, ccs_ref)

    q = jnp.dot(h_ref[...], wq_ref[...], preferred_element_type=F32)
    q_ref[...] = (q * (LOG2E * ATT_DK ** -0.5)).astype(BF16)
    cc_conv_then_next_projection(0)

    k_ref[...] = jnp.dot(h_ref[...], wk_ref[...], preferred_element_type=F32).astype(BF16)
    cc_conv_then_next_projection(1)

    vt = lax.dot_general(wvt_ref[...], h_ref[...], _NT, preferred_element_type=F32).astype(BF16)
    for hd in range(ATT_HEADS):
        vt_ref[pl.ds(hd * VT_ROWS, ATT_DV), :] = vt[hd * ATT_DV:(hd + 1) * ATT_DV]
        vt_ref[pl.ds(hd * VT_ROWS + ATT_DV, VT_ROWS - ATT_DV), :] = jnp.ones(
            (VT_ROWS - ATT_DV, tm), BF16)
    cc_conv_then_next_projection(2)

    zc = jnp.dot(h_ref[...], wc_ref[...], preferred_element_type=F32)
    ccn_ref[...] = zc[:, :CC_CH] * jax.nn.sigmoid(zc[:, CC_CH:])
    cc_conv_then_next_projection(3)

    zs = jnp.dot(h_ref[...], ws_ref[...], preferred_element_type=F32)
    scn_ref[...] = zs[:, SC_CH:2 * SC_CH] * zs[:, 2 * SC_CH:]
    gaten_ref[...] = zs[:, :SC_CH]
    cc_conv(blocks=CC_BLOCK_SPLIT[4])
    _dw_conv(scx_ref, scw_ref, scy_ref, taps=SC_K, halo=SC_HALO, blocks=range(tm // CONV_ROWS))
    oc_ref[:, CC_CH:] = (gate_ref[...] * scy_ref[...]).astype(BF16)

    u = ccy_ref[...] + ccb_ref[...]
    mu = jnp.mean(u, axis=-1, keepdims=True)
    var = jnp.mean(jnp.square(u - mu), axis=-1, keepdims=True)
    y = (u - mu) * lax.rsqrt(var + EPS) * lng_ref[...] + lnb_ref[...]
    oc_ref[:, :CC_CH] = (y * jax.nn.sigmoid(y)).astype(BF16)

    seq_start = lax.rem(g, tiles_per_seq) == 0
    ccx_ref[pl.ds(0, CC_HALO), :] = jnp.where(seq_start, 0.0, ccx_ref[pl.ds(tm, CC_HALO), :])
    ccx_ref[pl.ds(CC_HALO, tm), :] = ccn_ref[...]
    scx_ref[pl.ds(0, SC_HALO), :] = jnp.where(seq_start, 0.0, scx_ref[pl.ds(tm, SC_HALO), :])
    scx_ref[pl.ds(SC_HALO, tm), :] = scn_ref[...]
    gate_ref[...] = gaten_ref[...]


def _mix_in(x, g, wq, wk, wvt, wc, ws, ccw, ccb, lng, lnb, scw):
    b, t, d = x.shape
    tm = ATT_TILE
    nt = t // tm
    n_tiles = b * nt
    qk_cols = wq.shape[1]
    v_cols = ATT_HEADS * VT_ROWS
    proj = lambda i: jnp.minimum(i, n_tiles - 1)
    conv = lambda i: jnp.maximum(i - 1, 0)
    row = lambda cols, dt, tile: (
        pl.BlockSpec((None, tm, cols), lambda i: (tile(i) // nt, tile(i) % nt, 0)),
        jax.ShapeDtypeStruct((b, t, cols), dt))
    q_spec, q_shape = row(qk_cols, BF16, proj)
    oc_spec, oc_shape = row(CC_CH + SC_CH, BF16, conv)
    vt_spec = pl.BlockSpec((None, None, v_cols, tm),
                           lambda i: (proj(i) // nt, proj(i) % nt, 0, 0))
    vt_shape = jax.ShapeDtypeStruct((b, nt, v_cols, tm), BF16)
    consts = [g, wq, wk, wvt, wc, ws, ccw, ccb, lng, lnb, scw]
    return pl.pallas_call(
        functools.partial(_mix_in_kernel, tiles_per_seq=nt),
        grid=(n_tiles + 1,),
        in_specs=[pl.BlockSpec((None, tm, d), lambda i: (proj(i) // nt, proj(i) % nt, 0))]
        + [_const_spec(c.shape) for c in consts],
        out_specs=[q_spec, q_spec, vt_spec, oc_spec],
        out_shape=[q_shape, q_shape, vt_shape, oc_shape],
        scratch_shapes=[
            pltpu.VMEM((tm, d), BF16),
            pltpu.VMEM((CC_HALO + tm, CC_CH), F32),
            pltpu.VMEM((SC_HALO + tm, SC_CH), F32),
            pltpu.VMEM((tm, SC_CH), F32),
            pltpu.VMEM((tm, CC_CH), F32),
            pltpu.VMEM((tm, SC_CH), F32),
            pltpu.VMEM((tm, SC_CH), F32),
            pltpu.VMEM((SUBLANES - 1, CC_HALO + tm - SUBLANES, CC_CH), F32),
            pltpu.VMEM((tm, CC_CH), F32),
            pltpu.VMEM((tm, SC_CH), F32),
        ],
        compiler_params=_params(1),
        name="mix_in",
    )(x, *consts)


def _diff_attn_kernel(lam_ref, subln_ref, q_ref, k_ref, vt_ref, o_ref,
                      qs_ref, bias_ref, sa_ref, sb_ref, ca_ref, cb_ref, m_ref, acc_ref,
                      *, lam_init):
    tq = q_ref.shape[0]
    tk = tq
    head = pl.program_id(1)
    qi = pl.program_id(2)
    slope = jnp.left_shift(1, 2 * (ATT_HEADS - 1 - head)).astype(F32) * (LOG2E / 256.0)

    @pl.when(qi == 0)
    def _():
        jj = lax.broadcasted_iota(jnp.int32, (tk, 2 * tq), 0)
        ii = lax.broadcasted_iota(jnp.int32, (tk, 2 * tq), 1)
        ii = jnp.where(ii >= tq, ii - tq, ii)
        bias = jj.astype(F32) * slope
        bias_ref[0] = bias
        bias_ref[1] = jnp.where(jj <= ii, bias, MASK_VALUE)

    q = q_ref[...]
    lane = lax.broadcasted_iota(jnp.int32, q.shape, 1)
    zero = jnp.zeros_like(q)
    qs_ref[pl.ds(0, tq), :] = jnp.where(lane < ATT_DK, q, zero)
    qs_ref[pl.ds(tq, tq), :] = jnp.where(lane >= ATT_DK, q, zero)

    m_ref[...] = jnp.full(m_ref.shape, MASK_VALUE, F32)
    acc_ref[...] = jnp.zeros(acc_ref.shape, F32)

    def scores(c, s_ref, cmax_ref):
        j0 = pl.multiple_of(c * tk, tk)
        s = lax.dot_general(k_ref[pl.ds(j0, tk), :], qs_ref[...], _NT,
                            preferred_element_type=F32)
        s = s + bias_ref[(c == qi).astype(jnp.int32)]
        s_ref[...] = s
        cmax_ref[...] = jnp.max(s, axis=0, keepdims=True)

    def accumulate(c, s_ref, cmax_ref):
        shift = slope * (c * tk).astype(F32)
        vtc = vt_ref[c]
        for nb in range(2 * tq // ATT_LANES):
            cols = pl.ds(nb * ATT_LANES, ATT_LANES)
            m_prev = m_ref[:, cols] - shift
            m_new = jnp.maximum(m_prev, cmax_ref[:, cols])
            alpha = jnp.exp2(m_prev - m_new)
            p = jnp.exp2(s_ref[:, cols] - m_new).astype(BF16)
            m_ref[:, cols] = m_new + shift
            pv = jnp.dot(vtc, p, preferred_element_type=F32)
            acc_ref[:, cols] = alpha * acc_ref[:, cols] + pv

    n_chunks = qi + 1
    scores(0, sa_ref, ca_ref)

    def pair(t, carry):
        c = 2 * t
        scores(c + 1, sb_ref, cb_ref)
        accumulate(c, sa_ref, ca_ref)
        scores(jnp.minimum(c + 2, qi), sa_ref, ca_ref)
        accumulate(c + 1, sb_ref, cb_ref)
        return carry

    lax.fori_loop(0, n_chunks // 2, pair, 0)

    @pl.when(n_chunks % 2 == 1)
    def _():
        accumulate(qi, sa_ref, ca_ref)

    lam_p = lam_ref[...]
    lam = (jnp.exp(jnp.sum(lam_p[0:1] * lam_p[1:2], axis=-1, keepdims=True))
           - jnp.exp(jnp.sum(lam_p[2:3] * lam_p[3:4], axis=-1, keepdims=True)) + lam_init)
    acc = acc_ref[pl.ds(0, ATT_DV), :]
    inv_l = 1.0 / acc_ref[pl.ds(ATT_DV, 1), :]
    o = acc[:, :tq] * inv_l[:, :tq] - lam * (acc[:, tq:] * inv_l[:, tq:])
    o = o.T
    o_ref[...] = (_rms(o, subln_ref[...]) * (1.0 - lam_init)).astype(BF16)


def _diff_attn(lam_p, subln, q, k, vt, lam_init):
    b, t, _ = q.shape
    tq = ATT_TILE
    nt = t // tq
    return pl.pallas_call(
        functools.partial(_diff_attn_kernel, lam_init=lam_init),
        grid=(b, ATT_HEADS, nt),
        in_specs=[
            _const_spec(lam_p.shape),
            _const_spec(subln.shape),
            pl.BlockSpec((None, tq, ATT_DV), lambda i, h, j: (i, j, h)),
            pl.BlockSpec((None, t, ATT_DV), lambda i, h, j: (i, 0, h)),
            pl.BlockSpec((None, nt, VT_ROWS, tq), lambda i, h, j: (i, 0, h, 0)),
        ],
        out_specs=pl.BlockSpec((None, tq, ATT_DV), lambda i, h, j: (i, j, h)),
        out_shape=jax.ShapeDtypeStruct((b, t, ATT_HEADS * ATT_DV), BF16),
        scratch_shapes=[
            pltpu.VMEM((2 * tq, ATT_DV), BF16),
            pltpu.VMEM((2, tq, 2 * tq), F32),
            pltpu.VMEM((tq, 2 * tq), F32),
            pltpu.VMEM((tq, 2 * tq), F32),
            pltpu.VMEM((1, 2 * tq), F32),
            pltpu.VMEM((1, 2 * tq), F32),
            pltpu.VMEM((1, 2 * tq), F32),
            pltpu.VMEM((VT_ROWS, 2 * tq), F32),
        ],
        compiler_params=_params(3),
        name="diff_attn",
    )(lam_p, subln, q, k, vt)


def _mem_kv_kernel(mem_ref, g_ref, w_ref, kv_ref):
    h = _rms(mem_ref[...], g_ref[...]).astype(BF16)
    kv_ref[...] = jnp.dot(h, w_ref[...], preferred_element_type=F32).astype(BF16)


def _mem_kv(mem, g, w):
    b, m, d = mem.shape
    n = w.shape[1]
    return pl.pallas_call(
        _mem_kv_kernel,
        grid=(b,),
        in_specs=[pl.BlockSpec((None, m, d), lambda i: (i, 0, 0)), _const_spec(g.shape),
                  _const_spec(w.shape)],
        out_specs=pl.BlockSpec((None, m, n), lambda i: (i, 0, 0)),
        out_shape=jax.ShapeDtypeStruct((b, m, n), BF16),
        compiler_params=_params(1),
        name="mem_kv",
    )(mem, g, w)


def _mix_out_kernel(x_ref, oa_ref, oc_ref, wo_ref, g_ref, wq_ref, kv_ref, wxo_ref, o_ref):
    n_att = oa_ref.shape[1]
    x = x_ref[...]
    x = x + jnp.dot(oa_ref[...], wo_ref[pl.ds(0, n_att), :], preferred_element_type=F32)
    x = x + jnp.dot(oc_ref[...], wo_ref[pl.ds(n_att, oc_ref.shape[1]), :],
                    preferred_element_type=F32)

    hq = jnp.dot(_rms(x, g_ref[...]).astype(BF16), wq_ref[...], preferred_element_type=F32)
    hq = (hq * (XA_HD ** -0.5)).astype(BF16)
    kd = XA_HEADS * XA_HD
    heads = []
    for h in range(XA_HEADS):
        sl = slice(h * XA_HD, (h + 1) * XA_HD)
        s = lax.dot_general(hq[:, sl], kv_ref[:, sl], _NT, preferred_element_type=F32)
        p = jnp.exp(s - jnp.max(s, axis=-1, keepdims=True))
        inv = 1.0 / jnp.sum(p, axis=-1, keepdims=True)
        o = jnp.dot(p.astype(BF16), kv_ref[:, kd + h * XA_HD:kd + (h + 1) * XA_HD],
                    preferred_element_type=F32)
        heads.append((o * inv).astype(BF16))
    o = jnp.concatenate(heads, axis=-1)
    o_ref[...] = x + jnp.dot(o, wxo_ref[...], preferred_element_type=F32)


def _mix_out(x, oa, oc, wo, g, wq, kv, wxo):
    b, t, d = x.shape
    tm = TOKEN_TILE
    row = lambda cols: pl.BlockSpec((None, tm, cols), lambda i, j: (i, j, 0))
    return pl.pallas_call(
        _mix_out_kernel,
        grid=(b, t // tm),
        in_specs=[row(d), row(oa.shape[2]), row(oc.shape[2]), _const_spec(wo.shape),
                  _const_spec(g.shape), _const_spec(wq.shape),
                  pl.BlockSpec((None,) + kv.shape[1:], lambda i, j: (i, 0, 0)),
                  _const_spec(wxo.shape)],
        out_specs=row(d),
        out_shape=jax.ShapeDtypeStruct((b, t, d), F32),
        compiler_params=_params(2),
        name="mix_out",
    )(x, oa, oc, wo, g, wq, kv, wxo)


def kernel(x, mem, ffn1_norm, ffn1_w_gate, ffn1_w_up, ffn1_w_down, mix_norm, w_in, lam_q1, lam_k1, lam_q2, lam_k2, diff_subln, cc_dw, cc_dw_b, cc_ln_g, cc_ln_b, sc_dw, w_out, xa_norm, mem_norm, xa_wq, xa_wkv, xa_wo, ffn2_norm, ffn2_w_gate, ffn2_w_up, ffn2_w_down, final_norm):
    b, t, d = x.shape
    n_qk = ATT_HEADS * 2 * ATT_DK
    n_v = ATT_HEADS * ATT_DV
    bf = lambda w: w.astype(BF16)
    vec = lambda v: v.reshape(1, -1).astype(F32)

    for l in range(DEPTH):
        x = _ffn(x.reshape(b * t, d), vec(ffn1_norm[l]), bf(ffn1_w_gate[l]), bf(ffn1_w_up[l]),
                 bf(ffn1_w_down[l])).reshape(b, t, d)

        wi = w_in[l]
        q, k, vt, oc = _mix_in(
            x, vec(mix_norm[l]),
            bf(wi[:, :n_qk]), bf(wi[:, n_qk:2 * n_qk]), bf(wi[:, 2 * n_qk:2 * n_qk + n_v].T),
            bf(wi[:, 2 * n_qk + n_v:2 * n_qk + n_v + 2 * CC_CH]), bf(wi[:, 2 * n_qk + n_v + 2 * CC_CH:]),
            cc_dw[l].astype(F32), vec(cc_dw_b[l]), vec(cc_ln_g[l]), vec(cc_ln_b[l]),
            sc_dw[l].astype(F32))

        lam_init = 0.8 - 0.6 * math.exp(-0.3 * l)
        lam_p = jnp.stack([lam_q1[l], lam_k1[l], lam_q2[l], lam_k2[l]]).astype(F32)
        oa = _diff_attn(lam_p, vec(diff_subln[l]), q, k, vt, lam_init)

        kv = _mem_kv(mem, vec(mem_norm[l]), bf(xa_wkv[l]))
        x = _mix_out(x, oa, oc, bf(w_out[l]), vec(xa_norm[l]), bf(xa_wq[l]), kv, bf(xa_wo[l]))

        last = l == DEPTH - 1
        x = _ffn(x.reshape(b * t, d), vec(ffn2_norm[l]), bf(ffn2_w_gate[l]), bf(ffn2_w_up[l]),
                 bf(ffn2_w_down[l]), vec(final_norm) if last else None).reshape(b, t, d)
    return x
```

```python
import functools
import math

import jax
import jax.numpy as jnp
from jax import lax
from jax.experimental import pallas as pl
from jax.experimental.pallas import tpu as pltpu

D_MODEL = 1024
DEPTH = 2
ATT_HEADS = 4
ATT_DV = 128
ATT_DK = 64
CC_CH = 256
CC_K = 31
SC_CH = 256
SC_K = 3
D_FF = 2816
MEM_LEN = 256
XA_HEADS = 4
XA_HD = 256
EPS = 1e-6

F32 = jnp.float32
BF16 = jnp.bfloat16

TOKEN_TILE = 512
ATT_TILE = 512
VT_ROWS = ATT_DV + 16
LOG2E = math.log2(math.e)
ATT_LANES = 256
SUBLANES = 8
CONV_ROWS = 64
CC_HALO = 32
SC_HALO = 8
VMEM_LIMIT = 56 * 1024 * 1024
MASK_VALUE = -1e30

_NT = (((1,), (1,)), ((), ()))


def _rms(x, g):
    return x * lax.rsqrt(jnp.mean(x * x, axis=-1, keepdims=True) + EPS) * g


def _const_spec(shape):
    nd = len(shape)
    return pl.BlockSpec(shape, lambda *_: (0,) * nd, pipeline_mode=pl.Buffered(1))


def _params(n_axes):
    return pltpu.CompilerParams(
        dimension_semantics=("arbitrary",) * n_axes, vmem_limit_bytes=VMEM_LIMIT)


def _ffn_kernel(x_ref, g_ref, wg_ref, wu_ref, wd_ref, *rest, final):
    if final:
        fg_ref, o_ref = rest
    else:
        (o_ref,) = rest
    x = x_ref[...]
    h = _rms(x, g_ref[...]).astype(BF16)
    gate = jnp.dot(h, wg_ref[...], preferred_element_type=F32)
    up = jnp.dot(h, wu_ref[...], preferred_element_type=F32)
    act = (gate * jax.nn.sigmoid(gate) * up).astype(BF16)
    y = x + 0.5 * jnp.dot(act, wd_ref[...], preferred_element_type=F32)
    if final:
        y = _rms(y, fg_ref[...])
    o_ref[...] = y


def _ffn(x, g, wg, wu, wd, final_g=None):
    n, d = x.shape
    tm = TOKEN_TILE
    final = final_g is not None
    row_spec = pl.BlockSpec((tm, d), lambda i: (i, 0))
    in_specs = [row_spec, _const_spec((1, d)), _const_spec(wg.shape), _const_spec(wu.shape),
                _const_spec(wd.shape)]
    args = [x, g, wg, wu, wd]
    if final:
        in_specs.append(_const_spec((1, d)))
        args.append(final_g)
    return pl.pallas_call(
        functools.partial(_ffn_kernel, final=final),
        grid=(n // tm,),
        in_specs=in_specs,
        out_specs=row_spec,
        out_shape=jax.ShapeDtypeStruct((n, d), F32),
        compiler_params=_params(1),
        name="ffn_final" if final else "ffn",
    )(*args)


def _shifted_copies(ext_ref, shift_ref):
    rows = shift_ref.shape[1]
    for b in range(1, SUBLANES):
        shift_ref[b - 1] = ext_ref[pl.ds(b, rows), :]


def _dw_conv(ext_ref, w_ref, out_ref, *, taps, halo, blocks, shift_ref=None):
    base = halo - (taps - 1)
    folded = None
    for r in blocks:
        acc = None
        for j in range(taps):
            off = base + j
            if shift_ref is None or off % SUBLANES == 0:
                rows = ext_ref[pl.ds(off + r * CONV_ROWS, CONV_ROWS), :]
            else:
                rows = shift_ref[off % SUBLANES - 1,
                                 pl.ds(off - off % SUBLANES + r * CONV_ROWS, CONV_ROWS), :]
            term = rows * w_ref[pl.ds(j, 1), :]
            acc = term if acc is None else acc + term
        out_ref[pl.ds(r * CONV_ROWS, CONV_ROWS), :] = acc
        bits = pltpu.bitcast(acc, jnp.uint32)
        for i in range(0, CONV_ROWS, 8):
            for c in range(0, bits.shape[1], 128):
                tile = bits[i:i + 8, c:c + 128]
                folded = tile if folded is None else folded | tile
    return folded


CC_BLOCK_SPLIT = ((0,), (1,), (2, 3), (4, 5), (6, 7))


def _order_after(dst_ref, bits):
    half = jnp.uint32(16)
    zero = pltpu.bitcast(lax.shift_right_logical(lax.shift_right_logical(bits, half), half), F32)
    zero = jnp.concatenate([zero, zero], axis=0).astype(dst_ref.dtype)
    dst_ref[...] = dst_ref[...] + zero


def _mix_in_kernel(x_ref, g_ref, wqt_ref, wk_ref, wvt_ref, wc_ref, ws_ref,
                   ccw_ref, ccb_ref, lng_ref, lnb_ref, scw_ref,
                   qt_ref, k_ref, vt_ref, oc_ref,
                   h_ref, ccx_ref, scx_ref, gate_ref, ccn_ref, scn_ref, gaten_ref, ccs_ref,
                   ccy_ref, scy_ref, *, tiles_per_seq):
    tm = x_ref.shape[0]
    g = pl.program_id(0)
    cc_conv = functools.partial(_dw_conv, ccx_ref, ccw_ref, ccy_ref, taps=CC_K, halo=CC_HALO,
                                shift_ref=ccs_ref)

    def cc_conv_then_next_projection(i):
        done = cc_conv(blocks=CC_BLOCK_SPLIT[i])
        _order_after(h_ref.at[pl.ds(0, 16), pl.ds(0, 128)], done)

    @pl.when(g == 0)
    def _():
        ccx_ref[...] = jnp.zeros(ccx_ref.shape, F32)
        scx_ref[...] = jnp.zeros(scx_ref.shape, F32)
        gate_ref[...] = jnp.zeros(gate_ref.shape, F32)

    h_ref[...] = _rms(x_ref[...], g_ref[...]).astype(BF16)
    _shifted_copies(ccx_ref, ccs_ref)

    qt = lax.dot_general(wqt_ref[...], h_ref[...], _NT, preferred_element_type=F32)
    qt_ref[...] = (qt * (LOG2E * ATT_DK ** -0.5)).astype(BF16)
    cc_conv_then_next_projection(0)

    k_ref[...] = jnp.dot(h_ref[...], wk_ref[...], preferred_element_type=F32).astype(BF16)
    cc_conv_then_next_projection(1)

    vt = lax.dot_general(wvt_ref[...], h_ref[...], _NT, preferred_element_type=F32).astype(BF16)
    for hd in range(ATT_HEADS):
        vt_ref[pl.ds(hd * VT_ROWS, ATT_DV), :] = vt[hd * ATT_DV:(hd + 1) * ATT_DV]
        vt_ref[pl.ds(hd * VT_ROWS + ATT_DV, VT_ROWS - ATT_DV), :] = jnp.ones(
            (VT_ROWS - ATT_DV, tm), BF16)
    cc_conv_then_next_projection(2)

    zc = jnp.dot(h_ref[...], wc_ref[...], preferred_element_type=F32)
    ccn_ref[...] = zc[:, :CC_CH] * jax.nn.sigmoid(zc[:, CC_CH:])
    cc_conv_then_next_projection(3)

    zs = jnp.dot(h_ref[...], ws_ref[...], preferred_element_type=F32)
    scn_ref[...] = zs[:, SC_CH:2 * SC_CH] * zs[:, 2 * SC_CH:]
    gaten_ref[...] = zs[:, :SC_CH]
    cc_conv(blocks=CC_BLOCK_SPLIT[4])
    _dw_conv(scx_ref, scw_ref, scy_ref, taps=SC_K, halo=SC_HALO, blocks=range(tm // CONV_ROWS))
    oc_ref[:, CC_CH:] = (gate_ref[...] * scy_ref[...]).astype(BF16)

    u = ccy_ref[...] + ccb_ref[...]
    mu = jnp.mean(u, axis=-1, keepdims=True)
    var = jnp.mean(jnp.square(u - mu), axis=-1, keepdims=True)
    y = (u - mu) * lax.rsqrt(var + EPS) * lng_ref[...] + lnb_ref[...]
    oc_ref[:, :CC_CH] = (y * jax.nn.sigmoid(y)).astype(BF16)

    seq_start = lax.rem(g, tiles_per_seq) == 0
    ccx_ref[pl.ds(0, CC_HALO), :] = jnp.where(seq_start, 0.0, ccx_ref[pl.ds(tm, CC_HALO), :])
    ccx_ref[pl.ds(CC_HALO, tm), :] = ccn_ref[...]
    scx_ref[pl.ds(0, SC_HALO), :] = jnp.where(seq_start, 0.0, scx_ref[pl.ds(tm, SC_HALO), :])
    scx_ref[pl.ds(SC_HALO, tm), :] = scn_ref[...]
    gate_ref[...] = gaten_ref[...]


def _mix_in(x, g, wqt, wk, wvt, wc, ws, ccw, ccb, lng, lnb, scw):
    b, t, d = x.shape
    tm = ATT_TILE
    nt = t // tm
    n_tiles = b * nt
    qk_cols = wk.shape[1]
    v_cols = ATT_HEADS * VT_ROWS
    proj = lambda i: jnp.minimum(i, n_tiles - 1)
    conv = lambda i: jnp.maximum(i - 1, 0)
    row = lambda cols, dt, tile: (
        pl.BlockSpec((None, tm, cols), lambda i: (tile(i) // nt, tile(i) % nt, 0)),
        jax.ShapeDtypeStruct((b, t, cols), dt))
    k_spec, k_shape = row(qk_cols, BF16, proj)
    oc_spec, oc_shape = row(CC_CH + SC_CH, BF16, conv)
    slab = lambda rows: (
        pl.BlockSpec((None, None, rows, tm), lambda i: (proj(i) // nt, proj(i) % nt, 0, 0)),
        jax.ShapeDtypeStruct((b, nt, rows, tm), BF16))
    qt_spec, qt_shape = slab(qk_cols)
    vt_spec, vt_shape = slab(v_cols)
    consts = [g, wqt, wk, wvt, wc, ws, ccw, ccb, lng, lnb, scw]
    return pl.pallas_call(
        functools.partial(_mix_in_kernel, tiles_per_seq=nt),
        grid=(n_tiles + 1,),
        in_specs=[pl.BlockSpec((None, tm, d), lambda i: (proj(i) // nt, proj(i) % nt, 0))]
        + [_const_spec(c.shape) for c in consts],
        out_specs=[qt_spec, k_spec, vt_spec, oc_spec],
        out_shape=[qt_shape, k_shape, vt_shape, oc_shape],
        scratch_shapes=[
            pltpu.VMEM((tm, d), BF16),
            pltpu.VMEM((CC_HALO + tm, CC_CH), F32),
            pltpu.VMEM((SC_HALO + tm, SC_CH), F32),
            pltpu.VMEM((tm, SC_CH), F32),
            pltpu.VMEM((tm, CC_CH), F32),
            pltpu.VMEM((tm, SC_CH), F32),
            pltpu.VMEM((tm, SC_CH), F32),
            pltpu.VMEM((SUBLANES - 1, CC_HALO + tm - SUBLANES, CC_CH), F32),
            pltpu.VMEM((tm, CC_CH), F32),
            pltpu.VMEM((tm, SC_CH), F32),
        ],
        compiler_params=_params(1),
        name="mix_in",
    )(x, *consts)


def _diff_attn_kernel(lam_ref, subln_ref, qt_ref, k_ref, vt_ref, o_ref,
                      qs_ref, bias_ref, sa_ref, sb_ref, ca_ref, cb_ref, m_ref, acc_ref,
                      *, lam_init):
    tq = qt_ref.shape[1]
    tk = tq
    head = pl.program_id(1)
    qi = pl.program_id(2)
    slope = jnp.left_shift(1, 2 * (ATT_HEADS - 1 - head)).astype(F32) * (LOG2E / 256.0)

    @pl.when(qi == 0)
    def _():
        jj = lax.broadcasted_iota(jnp.int32, (tk, 2 * tq), 0)
        ii = lax.broadcasted_iota(jnp.int32, (tk, 2 * tq), 1)
        ii = jnp.where(ii >= tq, ii - tq, ii)
        bias = jj.astype(F32) * slope
        bias_ref[0] = bias
        bias_ref[1] = jnp.where(jj <= ii, bias, MASK_VALUE)

    qt = qt_ref[...]
    row = lax.broadcasted_iota(jnp.int32, qt.shape, 0)
    zero = jnp.zeros_like(qt)
    qs_ref[:, pl.ds(0, tq)] = jnp.where(row < ATT_DK, qt, zero)
    qs_ref[:, pl.ds(tq, tq)] = jnp.where(row >= ATT_DK, qt, zero)

    m_ref[...] = jnp.full(m_ref.shape, MASK_VALUE, F32)
    acc_ref[...] = jnp.zeros(acc_ref.shape, F32)

    def scores(c, s_ref, cmax_ref):
        j0 = pl.multiple_of(c * tk, tk)
        s = jnp.dot(k_ref[pl.ds(j0, tk), :], qs_ref[...],
                    preferred_element_type=F32)
        s = s + bias_ref[(c == qi).astype(jnp.int32)]
        s_ref[...] = s
        cmax_ref[...] = jnp.max(s, axis=0, keepdims=True)

    def accumulate(c, s_ref, cmax_ref):
        shift = slope * (c * tk).astype(F32)
        vtc = vt_ref[c]
        for nb in range(2 * tq // ATT_LANES):
            cols = pl.ds(nb * ATT_LANES, ATT_LANES)
            m_prev = m_ref[:, cols] - shift
            m_new = jnp.maximum(m_prev, cmax_ref[:, cols])
            alpha = jnp.exp2(m_prev - m_new)
            p = jnp.exp2(s_ref[:, cols] - m_new).astype(BF16)
            m_ref[:, cols] = m_new + shift
            pv = jnp.dot(vtc, p, preferred_element_type=F32)
            acc_ref[:, cols] = alpha * acc_ref[:, cols] + pv

    n_chunks = qi + 1
    scores(0, sa_ref, ca_ref)

    def pair(t, carry):
        c = 2 * t
        scores(c + 1, sb_ref, cb_ref)
        accumulate(c, sa_ref, ca_ref)
        scores(jnp.minimum(c + 2, qi), sa_ref, ca_ref)
        accumulate(c + 1, sb_ref, cb_ref)
        return carry

    lax.fori_loop(0, n_chunks // 2, pair, 0)

    @pl.when(n_chunks % 2 == 1)
    def _():
        accumulate(qi, sa_ref, ca_ref)

    lam_p = lam_ref[...]
    lam = (jnp.exp(jnp.sum(lam_p[0:1] * lam_p[1:2], axis=-1, keepdims=True))
           - jnp.exp(jnp.sum(lam_p[2:3] * lam_p[3:4], axis=-1, keepdims=True)) + lam_init)
    acc = acc_ref[pl.ds(0, ATT_DV), :]
    inv_l = 1.0 / acc_ref[pl.ds(ATT_DV, 1), :]
    o = acc[:, :tq] * inv_l[:, :tq] - lam * (acc[:, tq:] * inv_l[:, tq:])
    o = o.T
    o_ref[...] = (_rms(o, subln_ref[...]) * (1.0 - lam_init)).astype(BF16)


def _diff_attn(lam_p, subln, qt, k, vt, lam_init):
    b, t, _ = k.shape
    tq = ATT_TILE
    nt = t // tq
    return pl.pallas_call(
        functools.partial(_diff_attn_kernel, lam_init=lam_init),
        grid=(b, ATT_HEADS, nt),
        in_specs=[
            _const_spec(lam_p.shape),
            _const_spec(subln.shape),
            pl.BlockSpec((None, None, 2 * ATT_DK, tq), lambda i, h, j: (i, j, h, 0)),
            pl.BlockSpec((None, t, 2 * ATT_DK), lambda i, h, j: (i, 0, h)),
            pl.BlockSpec((None, nt, VT_ROWS, tq), lambda i, h, j: (i, 0, h, 0)),
        ],
        out_specs=pl.BlockSpec((None, tq, ATT_DV), lambda i, h, j: (i, j, h)),
        out_shape=jax.ShapeDtypeStruct((b, t, ATT_HEADS * ATT_DV), BF16),
        scratch_shapes=[
            pltpu.VMEM((2 * ATT_DK, 2 * tq), BF16),
            pltpu.VMEM((2, tq, 2 * tq), F32),
            pltpu.VMEM((tq, 2 * tq), F32),
            pltpu.VMEM((tq, 2 * tq), F32),
            pltpu.VMEM((1, 2 * tq), F32),
            pltpu.VMEM((1, 2 * tq), F32),
            pltpu.VMEM((1, 2 * tq), F32),
            pltpu.VMEM((VT_ROWS, 2 * tq), F32),
        ],
        compiler_params=_params(3),
        name="diff_attn",
    )(lam_p, subln, qt, k, vt)


def _mem_kv_kernel(mem_ref, g_ref, w_ref, kv_ref):
    h = _rms(mem_ref[...], g_ref[...]).astype(BF16)
    kv_ref[...] = jnp.dot(h, w_ref[...], preferred_element_type=F32).astype(BF16)


def _mem_kv(mem, g, w):
    b, m, d = mem.shape
    n = w.shape[1]
    return pl.pallas_call(
        _mem_kv_kernel,
        grid=(b,),
        in_specs=[pl.BlockSpec((None, m, d), lambda i: (i, 0, 0)), _const_spec(g.shape),
                  _const_spec(w.shape)],
        out_specs=pl.BlockSpec((None, m, n), lambda i: (i, 0, 0)),
        out_shape=jax.ShapeDtypeStruct((b, m, n), BF16),
        compiler_params=_params(1),
        name="mem_kv",
    )(mem, g, w)


def _mix_out_kernel(x_ref, oa_ref, oc_ref, wo_ref, g_ref, wq_ref, kv_ref, wxo_ref, o_ref):
    n_att = oa_ref.shape[1]
    x = x_ref[...]
    x = x + jnp.dot(oa_ref[...], wo_ref[pl.ds(0, n_att), :], preferred_element_type=F32)
    x = x + jnp.dot(oc_ref[...], wo_ref[pl.ds(n_att, oc_ref.shape[1]), :],
                    preferred_element_type=F32)

    hq = jnp.dot(_rms(x, g_ref[...]).astype(BF16), wq_ref[...], preferred_element_type=F32)
    hq = (hq * (XA_HD ** -0.5)).astype(BF16)
    kd = XA_HEADS * XA_HD
    heads = []
    for h in range(XA_HEADS):
        sl = slice(h * XA_HD, (h + 1) * XA_HD)
        s = lax.dot_general(hq[:, sl], kv_ref[:, sl], _NT, preferred_element_type=F32)
        p = jnp.exp(s - jnp.max(s, axis=-1, keepdims=True))
        inv = 1.0 / jnp.sum(p, axis=-1, keepdims=True)
        o = jnp.dot(p.astype(BF16), kv_ref[:, kd + h * XA_HD:kd + (h + 1) * XA_HD],
                    preferred_element_type=F32)
        heads.append((o * inv).astype(BF16))
    o = jnp.concatenate(heads, axis=-1)
    o_ref[...] = x + jnp.dot(o, wxo_ref[...], preferred_element_type=F32)


def _mix_out(x, oa, oc, wo, g, wq, kv, wxo):
    b, t, d = x.shape
    tm = TOKEN_TILE
    row = lambda cols: pl.BlockSpec((None, tm, cols), lambda i, j: (i, j, 0))
    return pl.pallas_call(
        _mix_out_kernel,
        grid=(b, t // tm),
        in_specs=[row(d), row(oa.shape[2]), row(oc.shape[2]), _const_spec(wo.shape),
                  _const_spec(g.shape), _const_spec(wq.shape),
                  pl.BlockSpec((None,) + kv.shape[1:], lambda i, j: (i, 0, 0)),
                  _const_spec(wxo.shape)],
        out_specs=row(d),
        out_shape=jax.ShapeDtypeStruct((b, t, d), F32),
        compiler_params=_params(2),
        name="mix_out",
    )(x, oa, oc, wo, g, wq, kv, wxo)


def kernel(x, mem, ffn1_norm, ffn1_w_gate, ffn1_w_up, ffn1_w_down, mix_norm, w_in, lam_q1, lam_k1, lam_q2, lam_k2, diff_subln, cc_dw, cc_dw_b, cc_ln_g, cc_ln_b, sc_dw, w_out, xa_norm, mem_norm, xa_wq, xa_wkv, xa_wo, ffn2_norm, ffn2_w_gate, ffn2_w_up, ffn2_w_down, final_norm):
    b, t, d = x.shape
    n_qk = ATT_HEADS * 2 * ATT_DK
    n_v = ATT_HEADS * ATT_DV
    bf = lambda w: w.astype(BF16)
    vec = lambda v: v.reshape(1, -1).astype(F32)

    for l in range(DEPTH):
        x = _ffn(x.reshape(b * t, d), vec(ffn1_norm[l]), bf(ffn1_w_gate[l]), bf(ffn1_w_up[l]),
                 bf(ffn1_w_down[l])).reshape(b, t, d)

        wi = w_in[l]
        qt, k, vt, oc = _mix_in(
            x, vec(mix_norm[l]),
            bf(wi[:, :n_qk].T), bf(wi[:, n_qk:2 * n_qk]), bf(wi[:, 2 * n_qk:2 * n_qk + n_v].T),
            bf(wi[:, 2 * n_qk + n_v:2 * n_qk + n_v + 2 * CC_CH]), bf(wi[:, 2 * n_qk + n_v + 2 * CC_CH:]),
            cc_dw[l].astype(F32), vec(cc_dw_b[l]), vec(cc_ln_g[l]), vec(cc_ln_b[l]),
            sc_dw[l].astype(F32))

        lam_init = 0.8 - 0.6 * math.exp(-0.3 * l)
        lam_p = jnp.stack([lam_q1[l], lam_k1[l], lam_q2[l], lam_k2[l]]).astype(F32)
        oa = _diff_attn(lam_p, vec(diff_subln[l]), qt, k, vt, lam_init)

        kv = _mem_kv(mem, vec(mem_norm[l]), bf(xa_wkv[l]))
        x = _mix_out(x, oa, oc, bf(w_out[l]), vec(xa_norm[l]), bf(xa_wq[l]), kv, bf(xa_wo[l]))

        last = l == DEPTH - 1
        x = _ffn(x.reshape(b * t, d), vec(ffn2_norm[l]), bf(ffn2_w_gate[l]), bf(ffn2_w_up[l]),
                 bf(ffn2_w_down[l]), vec(final_norm) if last else None).reshape(b, t, d)
    return x
```

```python
import functools
import math

import jax
import jax.numpy as jnp
from jax import lax
from jax.experimental import pallas as pl
from jax.experimental.pallas import tpu as pltpu

D_MODEL = 1024
DEPTH = 2
ATT_HEADS = 4
ATT_DV = 128
ATT_DK = 64
CC_CH = 256
CC_K = 31
SC_CH = 256
SC_K = 3
D_FF = 2816
MEM_LEN = 256
XA_HEADS = 4
XA_HD = 256
EPS = 1e-6

F32 = jnp.float32
BF16 = jnp.bfloat16

QK_COLS = ATT_HEADS * 2 * ATT_DK
V_COLS = ATT_HEADS * ATT_DV
K_COL = QK_COLS
V_COL = 2 * QK_COLS
CC_COL = V_COL + V_COLS
SC_COL = CC_COL + 2 * CC_CH

TOKEN_TILE = 512
ATT_TILE = 512
VT_ROWS = ATT_DV + 16
LOG2E = math.log2(math.e)
ATT_LANES = 256
CAST_ROWS = 128
SUBLANES = 8
CONV_ROWS = 64
CC_HALO = 32
SC_HALO = 8
VMEM_LIMIT = 56 * 1024 * 1024
MASK_VALUE = -1e30

_NT = (((1,), (1,)), ((), ()))


def _rms(x, g):
    return x * lax.rsqrt(jnp.mean(x * x, axis=-1, keepdims=True) + EPS) * g


def _const_spec(shape):
    nd = len(shape)
    return pl.BlockSpec(shape, lambda *_: (0,) * nd, pipeline_mode=pl.Buffered(1))


def _layer_spec(stacked, layer):
    nd = stacked.ndim - 1
    return pl.BlockSpec((None,) + stacked.shape[1:], lambda *_: (layer,) + (0,) * nd,
                        pipeline_mode=pl.Buffered(1))


def _params(n_axes):
    return pltpu.CompilerParams(
        dimension_semantics=("arbitrary",) * n_axes, vmem_limit_bytes=VMEM_LIMIT)


def _ffn_kernel(x_ref, g_ref, wg_ref, wu_ref, wd_ref, *rest, final):
    if final:
        fg_ref, o_ref = rest
    else:
        (o_ref,) = rest
    x = x_ref[...]
    h = _rms(x, g_ref[...]).astype(BF16)
    gate = jnp.dot(h, wg_ref[...], preferred_element_type=F32)
    up = jnp.dot(h, wu_ref[...], preferred_element_type=F32)
    act = (gate * jax.nn.sigmoid(gate) * up).astype(BF16)
    y = x + 0.5 * jnp.dot(act, wd_ref[...], preferred_element_type=F32)
    if final:
        y = _rms(y, fg_ref[...])
    o_ref[...] = y


def _ffn(x, g, wg, wu, wd, layer, final_g=None):
    n, d = x.shape
    tm = TOKEN_TILE
    final = final_g is not None
    row_spec = pl.BlockSpec((tm, d), lambda i: (i, 0))
    in_specs = [row_spec, _const_spec((1, d)), _layer_spec(wg, layer), _layer_spec(wu, layer),
                _layer_spec(wd, layer)]
    args = [x, g, wg, wu, wd]
    if final:
        in_specs.append(_const_spec((1, d)))
        args.append(final_g)
    return pl.pallas_call(
        functools.partial(_ffn_kernel, final=final),
        grid=(n // tm,),
        in_specs=in_specs,
        out_specs=row_spec,
        out_shape=jax.ShapeDtypeStruct((n, d), F32),
        compiler_params=_params(1),
        name="ffn_final" if final else "ffn",
    )(*args)


def _shifted_copies(ext_ref, shift_ref):
    rows = shift_ref.shape[1]
    for b in range(1, SUBLANES):
        shift_ref[b - 1] = ext_ref[pl.ds(b, rows), :]


def _dw_conv(ext_ref, w_ref, out_ref, *, taps, halo, blocks, shift_ref=None):
    base = halo - (taps - 1)
    folded = None
    for r in blocks:
        acc = None
        for j in range(taps):
            off = base + j
            if shift_ref is None or off % SUBLANES == 0:
                rows = ext_ref[pl.ds(off + r * CONV_ROWS, CONV_ROWS), :]
            else:
                rows = shift_ref[off % SUBLANES - 1,
                                 pl.ds(off - off % SUBLANES + r * CONV_ROWS, CONV_ROWS), :]
            term = rows * w_ref[pl.ds(j, 1), :]
            acc = term if acc is None else acc + term
        out_ref[pl.ds(r * CONV_ROWS, CONV_ROWS), :] = acc
        bits = pltpu.bitcast(acc, jnp.uint32)
        for i in range(0, CONV_ROWS, 8):
            for c in range(0, bits.shape[1], 128):
                tile = bits[i:i + 8, c:c + 128]
                folded = tile if folded is None else folded | tile
    return folded


CC_BLOCK_SPLIT = ((0,), (1,), (2, 3), (4, 5), (6, 7))


def _order_after(dst_ref, bits):
    half = jnp.uint32(16)
    zero = pltpu.bitcast(lax.shift_right_logical(lax.shift_right_logical(bits, half), half), F32)
    zero = jnp.concatenate([zero, zero], axis=0).astype(dst_ref.dtype)
    dst_ref[...] = dst_ref[...] + zero


def _mix_in_kernel(x_ref, g_ref, wqt_ref, wvt_ref, win_ref,
                   ccw_ref, ccb_ref, lng_ref, lnb_ref, scw_ref,
                   qt_ref, k_ref, vt_ref, oc_ref,
                   h_ref, ccx_ref, scx_ref, gate_ref, ccn_ref, scn_ref, gaten_ref, ccs_ref,
                   ccy_ref, scy_ref, *, tiles_per_seq):
    tm = x_ref.shape[0]
    g = pl.program_id(0)
    cc_conv = functools.partial(_dw_conv, ccx_ref, ccw_ref, ccy_ref, taps=CC_K, halo=CC_HALO,
                                shift_ref=ccs_ref)

    def cc_conv_then_next_projection(i):
        done = cc_conv(blocks=CC_BLOCK_SPLIT[i])
        _order_after(h_ref.at[pl.ds(0, 16), pl.ds(0, 128)], done)

    @pl.when(g == 0)
    def _():
        ccx_ref[...] = jnp.zeros(ccx_ref.shape, F32)
        scx_ref[...] = jnp.zeros(scx_ref.shape, F32)
        gate_ref[...] = jnp.zeros(gate_ref.shape, F32)

    h_ref[...] = _rms(x_ref[...], g_ref[...]).astype(BF16)
    _shifted_copies(ccx_ref, ccs_ref)

    qt = lax.dot_general(wqt_ref[...], h_ref[...], _NT, preferred_element_type=F32)
    qt_ref[...] = (qt * (LOG2E * ATT_DK ** -0.5)).astype(BF16)
    cc_conv_then_next_projection(0)

    k_ref[...] = jnp.dot(h_ref[...], win_ref[:, pl.ds(K_COL, QK_COLS)],
                         preferred_element_type=F32).astype(BF16)
    cc_conv_then_next_projection(1)

    vt = lax.dot_general(wvt_ref[...], h_ref[...], _NT, preferred_element_type=F32).astype(BF16)
    for hd in range(ATT_HEADS):
        vt_ref[pl.ds(hd * VT_ROWS, ATT_DV), :] = vt[hd * ATT_DV:(hd + 1) * ATT_DV]
        vt_ref[pl.ds(hd * VT_ROWS + ATT_DV, VT_ROWS - ATT_DV), :] = jnp.ones(
            (VT_ROWS - ATT_DV, tm), BF16)
    cc_conv_then_next_projection(2)

    zc = jnp.dot(h_ref[...], win_ref[:, pl.ds(CC_COL, 2 * CC_CH)],
                 preferred_element_type=F32)
    ccn_ref[...] = zc[:, :CC_CH] * jax.nn.sigmoid(zc[:, CC_CH:])
    cc_conv_then_next_projection(3)

    zs = jnp.dot(h_ref[...], win_ref[:, pl.ds(SC_COL, 3 * SC_CH)],
                 preferred_element_type=F32)
    scn_ref[...] = zs[:, SC_CH:2 * SC_CH] * zs[:, 2 * SC_CH:]
    gaten_ref[...] = zs[:, :SC_CH]
    cc_conv(blocks=CC_BLOCK_SPLIT[4])
    _dw_conv(scx_ref, scw_ref, scy_ref, taps=SC_K, halo=SC_HALO, blocks=range(tm // CONV_ROWS))
    oc_ref[:, CC_CH:] = (gate_ref[...] * scy_ref[...]).astype(BF16)

    u = ccy_ref[...] + ccb_ref[...]
    mu = jnp.mean(u, axis=-1, keepdims=True)
    var = jnp.mean(jnp.square(u - mu), axis=-1, keepdims=True)
    y = (u - mu) * lax.rsqrt(var + EPS) * lng_ref[...] + lnb_ref[...]
    oc_ref[:, :CC_CH] = (y * jax.nn.sigmoid(y)).astype(BF16)

    seq_start = lax.rem(g, tiles_per_seq) == 0
    ccx_ref[pl.ds(0, CC_HALO), :] = jnp.where(seq_start, 0.0, ccx_ref[pl.ds(tm, CC_HALO), :])
    ccx_ref[pl.ds(CC_HALO, tm), :] = ccn_ref[...]
    scx_ref[pl.ds(0, SC_HALO), :] = jnp.where(seq_start, 0.0, scx_ref[pl.ds(tm, SC_HALO), :])
    scx_ref[pl.ds(SC_HALO, tm), :] = scn_ref[...]
    gate_ref[...] = gaten_ref[...]


def _mix_in(x, g, wqt, wvt, w_in, layer, ccw, ccb, lng, lnb, scw):
    b, t, d = x.shape
    tm = ATT_TILE
    nt = t // tm
    n_tiles = b * nt
    qk_cols = QK_COLS
    v_cols = ATT_HEADS * VT_ROWS
    proj = lambda i: jnp.minimum(i, n_tiles - 1)
    conv = lambda i: jnp.maximum(i - 1, 0)
    row = lambda cols, dt, tile: (
        pl.BlockSpec((None, tm, cols), lambda i: (tile(i) // nt, tile(i) % nt, 0)),
        jax.ShapeDtypeStruct((b, t, cols), dt))
    k_spec, k_shape = row(qk_cols, BF16, proj)
    oc_spec, oc_shape = row(CC_CH + SC_CH, BF16, conv)
    slab = lambda rows: (
        pl.BlockSpec((None, None, rows, tm), lambda i: (proj(i) // nt, proj(i) % nt, 0, 0)),
        jax.ShapeDtypeStruct((b, nt, rows, tm), BF16))
    qt_spec, qt_shape = slab(qk_cols)
    vt_spec, vt_shape = slab(v_cols)
    consts = [ccw, ccb, lng, lnb, scw]
    return pl.pallas_call(
        functools.partial(_mix_in_kernel, tiles_per_seq=nt),
        grid=(n_tiles + 1,),
        in_specs=[pl.BlockSpec((None, tm, d), lambda i: (proj(i) // nt, proj(i) % nt, 0)),
                  _const_spec(g.shape), _const_spec(wqt.shape), _const_spec(wvt.shape),
                  _layer_spec(w_in, layer)]
        + [_const_spec(c.shape) for c in consts],
        out_specs=[qt_spec, k_spec, vt_spec, oc_spec],
        out_shape=[qt_shape, k_shape, vt_shape, oc_shape],
        scratch_shapes=[
            pltpu.VMEM((tm, d), BF16),
            pltpu.VMEM((CC_HALO + tm, CC_CH), F32),
            pltpu.VMEM((SC_HALO + tm, SC_CH), F32),
            pltpu.VMEM((tm, SC_CH), F32),
            pltpu.VMEM((tm, CC_CH), F32),
            pltpu.VMEM((tm, SC_CH), F32),
            pltpu.VMEM((tm, SC_CH), F32),
            pltpu.VMEM((SUBLANES - 1, CC_HALO + tm - SUBLANES, CC_CH), F32),
            pltpu.VMEM((tm, CC_CH), F32),
            pltpu.VMEM((tm, SC_CH), F32),
        ],
        compiler_params=_params(1),
        name="mix_in",
    )(x, g, wqt, wvt, w_in, *consts)


def _diff_attn_kernel(lam_ref, subln_ref, qt_ref, k_ref, vt_ref, o_ref,
                      qs_ref, bias_ref, sa_ref, sb_ref, ca_ref, cb_ref, m_ref, acc_ref,
                      *, lam_init):
    tq = qt_ref.shape[1]
    tk = tq
    head = pl.program_id(1)
    qi = pl.program_id(2)
    slope = jnp.left_shift(1, 2 * (ATT_HEADS - 1 - head)).astype(F32) * (LOG2E / 256.0)

    @pl.when(qi == 0)
    def _():
        jj = lax.broadcasted_iota(jnp.int32, (tk, 2 * tq), 0)
        ii = lax.broadcasted_iota(jnp.int32, (tk, 2 * tq), 1)
        ii = jnp.where(ii >= tq, ii - tq, ii)
        bias = jj.astype(F32) * slope
        bias_ref[0] = bias
        bias_ref[1] = jnp.where(jj <= ii, bias, MASK_VALUE)

    qt = qt_ref[...]
    row = lax.broadcasted_iota(jnp.int32, qt.shape, 0)
    zero = jnp.zeros_like(qt)
    qs_ref[:, pl.ds(0, tq)] = jnp.where(row < ATT_DK, qt, zero)
    qs_ref[:, pl.ds(tq, tq)] = jnp.where(row >= ATT_DK, qt, zero)

    m_ref[...] = jnp.full(m_ref.shape, MASK_VALUE, F32)
    acc_ref[...] = jnp.zeros(acc_ref.shape, F32)

    def scores(c, s_ref, cmax_ref):
        j0 = pl.multiple_of(c * tk, tk)
        s = jnp.dot(k_ref[pl.ds(j0, tk), :], qs_ref[...],
                    preferred_element_type=F32)
        s = s + bias_ref[(c == qi).astype(jnp.int32)]
        s_ref[...] = s
        cmax_ref[...] = jnp.max(s, axis=0, keepdims=True)

    def accumulate(c, s_ref, cmax_ref):
        shift = slope * (c * tk).astype(F32)
        vtc = vt_ref[c]
        for nb in range(2 * tq // ATT_LANES):
            cols = pl.ds(nb * ATT_LANES, ATT_LANES)
            m_prev = m_ref[:, cols] - shift
            m_new = jnp.maximum(m_prev, cmax_ref[:, cols])
            alpha = jnp.exp2(m_prev - m_new)
            p = jnp.exp2(s_ref[:, cols] - m_new).astype(BF16)
            m_ref[:, cols] = m_new + shift
            pv = jnp.dot(vtc, p, preferred_element_type=F32)
            acc_ref[:, cols] = alpha * acc_ref[:, cols] + pv

    n_chunks = qi + 1
    scores(0, sa_ref, ca_ref)

    def pair(t, carry):
        c = 2 * t
        scores(c + 1, sb_ref, cb_ref)
        accumulate(c, sa_ref, ca_ref)
        scores(jnp.minimum(c + 2, qi), sa_ref, ca_ref)
        accumulate(c + 1, sb_ref, cb_ref)
        return carry

    lax.fori_loop(0, n_chunks // 2, pair, 0)

    @pl.when(n_chunks % 2 == 1)
    def _():
        accumulate(qi, sa_ref, ca_ref)

    lam_p = lam_ref[...]
    lam = (jnp.exp(jnp.sum(lam_p[0:1] * lam_p[1:2], axis=-1, keepdims=True))
           - jnp.exp(jnp.sum(lam_p[2:3] * lam_p[3:4], axis=-1, keepdims=True)) + lam_init)
    acc = acc_ref[pl.ds(0, ATT_DV), :]
    inv_l = 1.0 / acc_ref[pl.ds(ATT_DV, 1), :]
    o = acc[:, :tq] * inv_l[:, :tq] - lam * (acc[:, tq:] * inv_l[:, tq:])
    inv_rms = lax.rsqrt(jnp.mean(o * o, axis=0, keepdims=True) + EPS)
    o_ref[...] = (o * inv_rms * subln_ref[...] * (1.0 - lam_init)).astype(BF16)


def _diff_attn(lam_p, subln, qt, k, vt, lam_init):
    b, t, _ = k.shape
    tq = ATT_TILE
    nt = t // tq
    return pl.pallas_call(
        functools.partial(_diff_attn_kernel, lam_init=lam_init),
        grid=(b, ATT_HEADS, nt),
        in_specs=[
            _const_spec(lam_p.shape),
            _const_spec(subln.shape),
            pl.BlockSpec((None, None, 2 * ATT_DK, tq), lambda i, h, j: (i, j, h, 0)),
            pl.BlockSpec((None, t, 2 * ATT_DK), lambda i, h, j: (i, 0, h)),
            pl.BlockSpec((None, nt, VT_ROWS, tq), lambda i, h, j: (i, 0, h, 0)),
        ],
        out_specs=pl.BlockSpec((None, None, ATT_DV, tq), lambda i, h, j: (i, j, h, 0)),
        out_shape=jax.ShapeDtypeStruct((b, nt, ATT_HEADS * ATT_DV, tq), BF16),
        scratch_shapes=[
            pltpu.VMEM((2 * ATT_DK, 2 * tq), BF16),
            pltpu.VMEM((2, tq, 2 * tq), F32),
            pltpu.VMEM((tq, 2 * tq), F32),
            pltpu.VMEM((tq, 2 * tq), F32),
            pltpu.VMEM((1, 2 * tq), F32),
            pltpu.VMEM((1, 2 * tq), F32),
            pltpu.VMEM((1, 2 * tq), F32),
            pltpu.VMEM((VT_ROWS, 2 * tq), F32),
        ],
        compiler_params=_params(3),
        name="diff_attn",
    )(lam_p, subln, qt, k, vt)


def _mem_kv_kernel(mem_ref, g_ref, w_ref, kv_ref):
    h = _rms(mem_ref[...], g_ref[...]).astype(BF16)
    kv_ref[...] = jnp.dot(h, w_ref[...], preferred_element_type=F32).astype(BF16)


def _mem_kv(mem, g, w, layer):
    b, m, d = mem.shape
    n = w.shape[2]
    return pl.pallas_call(
        _mem_kv_kernel,
        grid=(b,),
        in_specs=[pl.BlockSpec((None, m, d), lambda i: (i, 0, 0)), _const_spec(g.shape),
                  _layer_spec(w, layer)],
        out_specs=pl.BlockSpec((None, m, n), lambda i: (i, 0, 0)),
        out_shape=jax.ShapeDtypeStruct((b, m, n), BF16),
        compiler_params=_params(1),
        name="mem_kv",
    )(mem, g, w)


_TN = (((0,), (0,)), ((), ()))


def _mix_out_kernel(x_ref, oat_ref, oc_ref, wo_ref, g_ref, wq_ref, kv_ref, wxo_ref, o_ref):
    n_att = oat_ref.shape[0]
    x = x_ref[...]
    x = x + lax.dot_general(oat_ref[...], wo_ref[pl.ds(0, n_att), :], _TN,
                            preferred_element_type=F32)
    x = x + jnp.dot(oc_ref[...], wo_ref[pl.ds(n_att, oc_ref.shape[1]), :],
                    preferred_element_type=F32)

    hq = jnp.dot(_rms(x, g_ref[...]).astype(BF16), wq_ref[...], preferred_element_type=F32)
    hq = (hq * (XA_HD ** -0.5)).astype(BF16)
    kd = XA_HEADS * XA_HD
    heads = []
    for h in range(XA_HEADS):
        sl = slice(h * XA_HD, (h + 1) * XA_HD)
        s = lax.dot_general(hq[:, sl], kv_ref[:, sl], _NT, preferred_element_type=F32)
        p = jnp.exp(s - jnp.max(s, axis=-1, keepdims=True))
        inv = 1.0 / jnp.sum(p, axis=-1, keepdims=True)
        o = jnp.dot(p.astype(BF16), kv_ref[:, kd + h * XA_HD:kd + (h + 1) * XA_HD],
                    preferred_element_type=F32)
        heads.append((o * inv).astype(BF16))
    o = jnp.concatenate(heads, axis=-1)
    o_ref[...] = x + jnp.dot(o, wxo_ref[...], preferred_element_type=F32)


def _mix_out(x, oat, oc, wo, g, wq, kv, wxo, layer):
    b, t, d = x.shape
    tm = oat.shape[3]
    row = lambda cols: pl.BlockSpec((None, tm, cols), lambda i, j: (i, j, 0))
    return pl.pallas_call(
        _mix_out_kernel,
        grid=(b, t // tm),
        in_specs=[row(d), pl.BlockSpec((None, None, oat.shape[2], tm), lambda i, j: (i, j, 0, 0)),
                  row(oc.shape[2]), _layer_spec(wo, layer),
                  _const_spec(g.shape), _layer_spec(wq, layer),
                  pl.BlockSpec((None,) + kv.shape[1:], lambda i, j: (i, 0, 0)),
                  _layer_spec(wxo, layer)],
        out_specs=row(d),
        out_shape=jax.ShapeDtypeStruct((b, t, d), F32),
        compiler_params=_params(2),
        name="mix_out",
    )(x, oat, oc, wo, g, wq, kv, wxo)


def _cast_kernel(*refs):
    n = len(refs) // 2
    for src, dst in zip(refs[:n], refs[n:]):
        dst[...] = src[...].astype(dst.dtype)


def _to_bf16(*ws):
    layers, rows, cols = ws[0].shape
    assert all(w.shape == ws[0].shape for w in ws) and rows % CAST_ROWS == 0
    spec = pl.BlockSpec((None, CAST_ROWS, cols), lambda i, j: (i, j, 0))
    return pl.pallas_call(
        _cast_kernel,
        grid=(layers, rows // CAST_ROWS),
        in_specs=[spec] * len(ws),
        out_specs=[spec] * len(ws),
        out_shape=[jax.ShapeDtypeStruct(w.shape, BF16) for w in ws],
        compiler_params=_params(2),
        name="to_bf16",
    )(*ws)


def kernel(x, mem, ffn1_norm, ffn1_w_gate, ffn1_w_up, ffn1_w_down, mix_norm, w_in, lam_q1, lam_k1, lam_q2, lam_k2, diff_subln, cc_dw, cc_dw_b, cc_ln_g, cc_ln_b, sc_dw, w_out, xa_norm, mem_norm, xa_wq, xa_wkv, xa_wo, ffn2_norm, ffn2_w_gate, ffn2_w_up, ffn2_w_down, final_norm):
    b, t, d = x.shape
    vec = lambda v: v.reshape(1, -1).astype(F32)

    f1g, f1u, f2g, f2u, w_in = _to_bf16(ffn1_w_gate, ffn1_w_up, ffn2_w_gate, ffn2_w_up, w_in)
    f1d, f2d = _to_bf16(ffn1_w_down, ffn2_w_down)
    w_out, xa_wq, xa_wo = _to_bf16(w_out, xa_wq, xa_wo)
    (xa_wkv,) = _to_bf16(xa_wkv)

    for l in range(DEPTH):
        x = _ffn(x.reshape(b * t, d), vec(ffn1_norm[l]), f1g, f1u, f1d, l).reshape(b, t, d)

        qt, k, vt, oc = _mix_in(
            x, vec(mix_norm[l]),
            w_in[l, :, :QK_COLS].T, w_in[l, :, V_COL:V_COL + V_COLS].T, w_in, l,
            cc_dw[l].astype(F32), vec(cc_dw_b[l]), vec(cc_ln_g[l]), vec(cc_ln_b[l]),
            sc_dw[l].astype(F32))

        lam_init = 0.8 - 0.6 * math.exp(-0.3 * l)
        lam_p = jnp.stack([lam_q1[l], lam_k1[l], lam_q2[l], lam_k2[l]]).astype(F32)
        subln = jnp.broadcast_to(diff_subln[l].astype(F32)[:, None], (ATT_DV, ATT_TILE))
        oat = _diff_attn(lam_p, subln, qt, k, vt, lam_init)

        kv = _mem_kv(mem, vec(mem_norm[l]), xa_wkv, l)
        x = _mix_out(x, oat, oc, w_out, vec(xa_norm[l]), xa_wq, kv, xa_wo, l)

        last = l == DEPTH - 1
        x = _ffn(x.reshape(b * t, d), vec(ffn2_norm[l]), f2g, f2u, f2d, l,
                 vec(final_norm) if last else None).reshape(b, t, d)
    return x
```

```python
import functools
import math

import jax
import jax.numpy as jnp
from jax import lax
from jax.experimental import pallas as pl
from jax.experimental.pallas import tpu as pltpu

D_MODEL = 1024
DEPTH = 2
ATT_HEADS = 4
ATT_DV = 128
ATT_DK = 64
CC_CH = 256
CC_K = 31
SC_CH = 256
SC_K = 3
D_FF = 2816
MEM_LEN = 256
XA_HEADS = 4
XA_HD = 256
EPS = 1e-6

F32 = jnp.float32
BF16 = jnp.bfloat16

QK_COLS = ATT_HEADS * 2 * ATT_DK
V_COLS = ATT_HEADS * ATT_DV
K_COL = QK_COLS
V_COL = 2 * QK_COLS
CC_COL = V_COL + V_COLS
SC_COL = CC_COL + 2 * CC_CH

TOKEN_TILE = 512
ATT_TILE = 512
VT_ROWS = ATT_DV + 16
LOG2E = math.log2(math.e)
ATT_LANES = 256
CAST_ROWS = 128
SUBLANES = 8
CONV_ROWS = 64
CC_HALO = 32
SC_HALO = 8
VMEM_LIMIT = 56 * 1024 * 1024
MASK_VALUE = -1e30

_NT = (((1,), (1,)), ((), ()))


def _rms(x, g):
    return x * lax.rsqrt(jnp.mean(x * x, axis=-1, keepdims=True) + EPS) * g


def _const_spec(shape):
    nd = len(shape)
    return pl.BlockSpec(shape, lambda *_: (0,) * nd, pipeline_mode=pl.Buffered(1))


def _layer_spec(stacked, layer):
    nd = stacked.ndim - 1
    return pl.BlockSpec((None,) + stacked.shape[1:], lambda *_: (layer,) + (0,) * nd,
                        pipeline_mode=pl.Buffered(1))


def _params(n_axes):
    return pltpu.CompilerParams(
        dimension_semantics=("arbitrary",) * n_axes, vmem_limit_bytes=VMEM_LIMIT)


def _ffn_kernel(x_ref, g_ref, wg_ref, wu_ref, wd_ref, *rest, final):
    if final:
        fg_ref, o_ref = rest
    else:
        (o_ref,) = rest
    x = x_ref[...]
    h = _rms(x, g_ref[...]).astype(BF16)
    gate = jnp.dot(h, wg_ref[...], preferred_element_type=F32)
    up = jnp.dot(h, wu_ref[...], preferred_element_type=F32)
    act = (gate * jax.nn.sigmoid(gate) * up).astype(BF16)
    y = x + 0.5 * jnp.dot(act, wd_ref[...], preferred_element_type=F32)
    if final:
        y = _rms(y, fg_ref[...])
    o_ref[...] = y


def _ffn(x, g, wg, wu, wd, layer, final_g=None):
    n, d = x.shape
    tm = TOKEN_TILE
    final = final_g is not None
    row_spec = pl.BlockSpec((tm, d), lambda i: (i, 0))
    in_specs = [row_spec, _const_spec((1, d)), _layer_spec(wg, layer), _layer_spec(wu, layer),
                _layer_spec(wd, layer)]
    args = [x, g, wg, wu, wd]
    if final:
        in_specs.append(_const_spec((1, d)))
        args.append(final_g)
    return pl.pallas_call(
        functools.partial(_ffn_kernel, final=final),
        grid=(n // tm,),
        in_specs=in_specs,
        out_specs=row_spec,
        out_shape=jax.ShapeDtypeStruct((n, d), F32),
        compiler_params=_params(1),
        name="ffn_final" if final else "ffn",
    )(*args)


def _shifted_copies(ext_ref, shift_ref):
    rows = shift_ref.shape[1]
    for b in range(1, SUBLANES):
        shift_ref[b - 1] = ext_ref[pl.ds(b, rows), :]


def _dw_conv(ext_ref, w_ref, out_ref, *, taps, halo, blocks, shift_ref=None):
    base = halo - (taps - 1)
    folded = None
    for r in blocks:
        acc = None
        for j in range(taps):
            off = base + j
            if shift_ref is None or off % SUBLANES == 0:
                rows = ext_ref[pl.ds(off + r * CONV_ROWS, CONV_ROWS), :]
            else:
                rows = shift_ref[off % SUBLANES - 1,
                                 pl.ds(off - off % SUBLANES + r * CONV_ROWS, CONV_ROWS), :]
            term = rows * w_ref[pl.ds(j, 1), :]
            acc = term if acc is None else acc + term
        out_ref[pl.ds(r * CONV_ROWS, CONV_ROWS), :] = acc
        bits = pltpu.bitcast(acc, jnp.uint32)
        for i in range(0, CONV_ROWS, 8):
            for c in range(0, bits.shape[1], 128):
                tile = bits[i:i + 8, c:c + 128]
                folded = tile if folded is None else folded | tile
    return folded


CC_BLOCK_SPLIT = ((0,), (1,), (2, 3), (4, 5), (6, 7))


def _order_after(dst_ref, bits):
    half = jnp.uint32(16)
    zero = pltpu.bitcast(lax.shift_right_logical(lax.shift_right_logical(bits, half), half), F32)
    zero = jnp.concatenate([zero, zero], axis=0).astype(dst_ref.dtype)
    dst_ref[...] = dst_ref[...] + zero


def _mix_in_kernel(x_ref, g_ref, wqt_ref, wvt_ref, win_ref,
                   ccw_ref, ccb_ref, lng_ref, lnb_ref, scw_ref,
                   qt_ref, k_ref, vt_ref, oc_ref,
                   h_ref, ccx_ref, scx_ref, gate_ref, ccn_ref, scn_ref, gaten_ref, ccs_ref,
                   ccy_ref, scy_ref, *, tiles_per_seq):
    tm = x_ref.shape[0]
    g = pl.program_id(0)
    cc_conv = functools.partial(_dw_conv, ccx_ref, ccw_ref, ccy_ref, taps=CC_K, halo=CC_HALO,
                                shift_ref=ccs_ref)

    def cc_conv_then_next_projection(i):
        done = cc_conv(blocks=CC_BLOCK_SPLIT[i])
        _order_after(h_ref.at[pl.ds(0, 16), pl.ds(0, 128)], done)

    @pl.when(g == 0)
    def _():
        ccx_ref[...] = jnp.zeros(ccx_ref.shape, F32)
        scx_ref[...] = jnp.zeros(scx_ref.shape, F32)
        gate_ref[...] = jnp.zeros(gate_ref.shape, F32)

    h_ref[...] = _rms(x_ref[...], g_ref[...]).astype(BF16)
    _shifted_copies(ccx_ref, ccs_ref)

    qt = lax.dot_general(wqt_ref[...], h_ref[...], _NT, preferred_element_type=F32)
    qt_ref[...] = (qt * (LOG2E * ATT_DK ** -0.5)).astype(BF16)
    cc_conv_then_next_projection(0)

    k_ref[...] = jnp.dot(h_ref[...], win_ref[:, pl.ds(K_COL, QK_COLS)],
                         preferred_element_type=F32).astype(BF16)
    cc_conv_then_next_projection(1)

    vt = lax.dot_general(wvt_ref[...], h_ref[...], _NT, preferred_element_type=F32).astype(BF16)
    for hd in range(ATT_HEADS):
        vt_ref[pl.ds(hd * VT_ROWS, ATT_DV), :] = vt[hd * ATT_DV:(hd + 1) * ATT_DV]
        vt_ref[pl.ds(hd * VT_ROWS + ATT_DV, VT_ROWS - ATT_DV), :] = jnp.ones(
            (VT_ROWS - ATT_DV, tm), BF16)
    cc_conv_then_next_projection(2)

    zc = jnp.dot(h_ref[...], win_ref[:, pl.ds(CC_COL, 2 * CC_CH)],
                 preferred_element_type=F32)
    ccn_ref[...] = zc[:, :CC_CH] * jax.nn.sigmoid(zc[:, CC_CH:])
    cc_conv_then_next_projection(3)

    zs = jnp.dot(h_ref[...], win_ref[:, pl.ds(SC_COL, 3 * SC_CH)],
                 preferred_element_type=F32)
    scn_ref[...] = zs[:, SC_CH:2 * SC_CH] * zs[:, 2 * SC_CH:]
    gaten_ref[...] = zs[:, :SC_CH]
    cc_conv(blocks=CC_BLOCK_SPLIT[4])
    _dw_conv(scx_ref, scw_ref, scy_ref, taps=SC_K, halo=SC_HALO, blocks=range(tm // CONV_ROWS))
    oc_ref[:, CC_CH:] = (gate_ref[...] * scy_ref[...]).astype(BF16)

    u = ccy_ref[...] + ccb_ref[...]
    mu = jnp.mean(u, axis=-1, keepdims=True)
    var = jnp.mean(jnp.square(u - mu), axis=-1, keepdims=True)
    y = (u - mu) * lax.rsqrt(var + EPS) * lng_ref[...] + lnb_ref[...]
    oc_ref[:, :CC_CH] = (y * jax.nn.sigmoid(y)).astype(BF16)

    seq_start = lax.rem(g, tiles_per_seq) == 0
    ccx_ref[pl.ds(0, CC_HALO), :] = jnp.where(seq_start, 0.0, ccx_ref[pl.ds(tm, CC_HALO), :])
    ccx_ref[pl.ds(CC_HALO, tm), :] = ccn_ref[...]
    scx_ref[pl.ds(0, SC_HALO), :] = jnp.where(seq_start, 0.0, scx_ref[pl.ds(tm, SC_HALO), :])
    scx_ref[pl.ds(SC_HALO, tm), :] = scn_ref[...]
    gate_ref[...] = gaten_ref[...]


def _mix_in(x, g, wqt, wvt, w_in, layer, ccw, ccb, lng, lnb, scw):
    b, t, d = x.shape
    tm = ATT_TILE
    nt = t // tm
    n_tiles = b * nt
    qk_cols = QK_COLS
    v_cols = ATT_HEADS * VT_ROWS
    proj = lambda i: jnp.minimum(i, n_tiles - 1)
    conv = lambda i: jnp.maximum(i - 1, 0)
    row = lambda cols, dt, tile: (
        pl.BlockSpec((None, tm, cols), lambda i: (tile(i) // nt, tile(i) % nt, 0)),
        jax.ShapeDtypeStruct((b, t, cols), dt))
    k_spec, k_shape = row(qk_cols, BF16, proj)
    oc_spec, oc_shape = row(CC_CH + SC_CH, BF16, conv)
    slab = lambda rows: (
        pl.BlockSpec((None, None, rows, tm), lambda i: (proj(i) // nt, proj(i) % nt, 0, 0)),
        jax.ShapeDtypeStruct((b, nt, rows, tm), BF16))
    qt_spec, qt_shape = slab(qk_cols)
    vt_spec, vt_shape = slab(v_cols)
    consts = [ccw, ccb, lng, lnb, scw]
    return pl.pallas_call(
        functools.partial(_mix_in_kernel, tiles_per_seq=nt),
        grid=(n_tiles + 1,),
        in_specs=[pl.BlockSpec((None, tm, d), lambda i: (proj(i) // nt, proj(i) % nt, 0)),
                  _const_spec(g.shape), _const_spec(wqt.shape), _const_spec(wvt.shape),
                  _layer_spec(w_in, layer)]
        + [_const_spec(c.shape) for c in consts],
        out_specs=[qt_spec, k_spec, vt_spec, oc_spec],
        out_shape=[qt_shape, k_shape, vt_shape, oc_shape],
        scratch_shapes=[
            pltpu.VMEM((tm, d), BF16),
            pltpu.VMEM((CC_HALO + tm, CC_CH), F32),
            pltpu.VMEM((SC_HALO + tm, SC_CH), F32),
            pltpu.VMEM((tm, SC_CH), F32),
            pltpu.VMEM((tm, CC_CH), F32),
            pltpu.VMEM((tm, SC_CH), F32),
            pltpu.VMEM((tm, SC_CH), F32),
            pltpu.VMEM((SUBLANES - 1, CC_HALO + tm - SUBLANES, CC_CH), F32),
            pltpu.VMEM((tm, CC_CH), F32),
            pltpu.VMEM((tm, SC_CH), F32),
        ],
        compiler_params=_params(1),
        name="mix_in",
    )(x, g, wqt, wvt, w_in, *consts)


def _diff_attn_kernel(lam_ref, subln_ref, qt_ref, k_ref, vt_ref, o_ref,
                      qs_ref, bias_ref, sa_ref, sb_ref, ca_ref, cb_ref, m_ref, acc_ref,
                      *, lam_init):
    nt, _, tq = qt_ref.shape
    tk = tq
    head = pl.program_id(1)
    slope = jnp.left_shift(1, 2 * (ATT_HEADS - 1 - head)).astype(F32) * (LOG2E / 256.0)

    jj = lax.broadcasted_iota(jnp.int32, (tk, 2 * tq), 0)
    ii = lax.broadcasted_iota(jnp.int32, (tk, 2 * tq), 1)
    ii = jnp.where(ii >= tq, ii - tq, ii)
    bias = jj.astype(F32) * slope
    bias_ref[0] = bias
    bias_ref[1] = jnp.where(jj <= ii, bias, MASK_VALUE)

    for i in range(nt):
        qt = qt_ref[i]
        row = lax.broadcasted_iota(jnp.int32, qt.shape, 0)
        zero = jnp.zeros_like(qt)
        qs_ref[i, :, pl.ds(0, tq)] = jnp.where(row < ATT_DK, qt, zero)
        qs_ref[i, :, pl.ds(tq, tq)] = jnp.where(row >= ATT_DK, qt, zero)

    m_ref[...] = jnp.full(m_ref.shape, MASK_VALUE, F32)
    acc_ref[...] = jnp.zeros(acc_ref.shape, F32)

    def scores(item, s_ref, cmax_ref):
        qi, c = item
        j0 = pl.multiple_of(c * tk, tk)
        s = jnp.dot(k_ref[pl.ds(j0, tk), :], qs_ref[qi], preferred_element_type=F32)
        s = s + bias_ref[(c == qi).astype(jnp.int32)]
        s_ref[...] = s
        cmax_ref[...] = jnp.max(s, axis=0, keepdims=True)

    def accumulate(item, s_ref, cmax_ref):
        qi, c = item
        shift = slope * (c * tk).astype(F32)
        vtc = vt_ref[c]
        for nb in range(2 * tq // ATT_LANES):
            cols = pl.ds(nb * ATT_LANES, ATT_LANES)
            m_prev = m_ref[qi, :, cols] - shift
            m_new = jnp.maximum(m_prev, cmax_ref[:, cols])
            alpha = jnp.exp2(m_prev - m_new)
            p = jnp.exp2(s_ref[:, cols] - m_new).astype(BF16)
            m_ref[qi, :, cols] = m_new + shift
            pv = jnp.dot(vtc, p, preferred_element_type=F32)
            acc_ref[qi, :, cols] = alpha * acc_ref[qi, :, cols] + pv

    def following(item):
        qi, c = item
        last = c == qi
        return qi + last.astype(jnp.int32), jnp.where(last, 0, c + 1)

    n_items = nt * (nt + 1) // 2
    assert n_items % 2 == 0
    ping, pong = (sa_ref, ca_ref), (sb_ref, cb_ref)
    first = (jnp.int32(0), jnp.int32(0))
    scores(first, *ping)

    def pair(_, item0):
        item1 = following(item0)
        qi2, c2 = following(item1)
        past_end = qi2 == nt
        item2 = (jnp.where(past_end, 0, qi2), jnp.where(past_end, 0, c2))
        scores(item1, *pong)
        accumulate(item0, *ping)
        scores(item2, *ping)
        accumulate(item1, *pong)
        return item2

    lax.fori_loop(0, n_items // 2, pair, first)

    lam_p = lam_ref[...]
    lam = (jnp.exp(jnp.sum(lam_p[0:1] * lam_p[1:2], axis=-1, keepdims=True))
           - jnp.exp(jnp.sum(lam_p[2:3] * lam_p[3:4], axis=-1, keepdims=True)) + lam_init)
    for i in range(nt):
        acc = acc_ref[i, pl.ds(0, ATT_DV), :]
        inv_l = 1.0 / acc_ref[i, pl.ds(ATT_DV, 1), :]
        o = acc[:, :tq] * inv_l[:, :tq] - lam * (acc[:, tq:] * inv_l[:, tq:])
        inv_rms = lax.rsqrt(jnp.mean(o * o, axis=0, keepdims=True) + EPS)
        o_ref[i] = (o * inv_rms * subln_ref[...] * (1.0 - lam_init)).astype(BF16)


def _diff_attn(lam_p, subln, qt, k, vt, lam_init):
    b, t, _ = k.shape
    tq = ATT_TILE
    nt = t // tq
    per_head = lambda rows: pl.BlockSpec((None, nt, rows, tq), lambda i, h: (i, 0, h, 0))
    return pl.pallas_call(
        functools.partial(_diff_attn_kernel, lam_init=lam_init),
        grid=(b, ATT_HEADS),
        in_specs=[
            _const_spec(lam_p.shape),
            _const_spec(subln.shape),
            per_head(2 * ATT_DK),
            pl.BlockSpec((None, t, 2 * ATT_DK), lambda i, h: (i, 0, h)),
            per_head(VT_ROWS),
        ],
        out_specs=per_head(ATT_DV),
        out_shape=jax.ShapeDtypeStruct((b, nt, ATT_HEADS * ATT_DV, tq), BF16),
        scratch_shapes=[
            pltpu.VMEM((nt, 2 * ATT_DK, 2 * tq), BF16),
            pltpu.VMEM((2, tq, 2 * tq), F32),
            pltpu.VMEM((tq, 2 * tq), F32),
            pltpu.VMEM((tq, 2 * tq), F32),
            pltpu.VMEM((1, 2 * tq), F32),
            pltpu.VMEM((1, 2 * tq), F32),
            pltpu.VMEM((nt, 1, 2 * tq), F32),
            pltpu.VMEM((nt, VT_ROWS, 2 * tq), F32),
        ],
        compiler_params=_params(2),
        name="diff_attn",
    )(lam_p, subln, qt, k, vt)


def _mem_kv_kernel(mem_ref, g_ref, w_ref, kv_ref):
    h = _rms(mem_ref[...], g_ref[...]).astype(BF16)
    kv_ref[...] = jnp.dot(h, w_ref[...], preferred_element_type=F32).astype(BF16)


def _mem_kv(mem, g, w, layer):
    b, m, d = mem.shape
    n = w.shape[2]
    return pl.pallas_call(
        _mem_kv_kernel,
        grid=(b,),
        in_specs=[pl.BlockSpec((None, m, d), lambda i: (i, 0, 0)), _const_spec(g.shape),
                  _layer_spec(w, layer)],
        out_specs=pl.BlockSpec((None, m, n), lambda i: (i, 0, 0)),
        out_shape=jax.ShapeDtypeStruct((b, m, n), BF16),
        compiler_params=_params(1),
        name="mem_kv",
    )(mem, g, w)


_TN = (((0,), (0,)), ((), ()))


def _mix_out_kernel(x_ref, oat_ref, oc_ref, wo_ref, g_ref, wq_ref, kv_ref, wxo_ref, o_ref):
    n_att = oat_ref.shape[0]
    x = x_ref[...]
    x = x + lax.dot_general(oat_ref[...], wo_ref[pl.ds(0, n_att), :], _TN,
                            preferred_element_type=F32)
    x = x + jnp.dot(oc_ref[...], wo_ref[pl.ds(n_att, oc_ref.shape[1]), :],
                    preferred_element_type=F32)

    hq = jnp.dot(_rms(x, g_ref[...]).astype(BF16), wq_ref[...], preferred_element_type=F32)
    hq = (hq * (XA_HD ** -0.5)).astype(BF16)
    kd = XA_HEADS * XA_HD
    heads = []
    for h in range(XA_HEADS):
        sl = slice(h * XA_HD, (h + 1) * XA_HD)
        s = lax.dot_general(hq[:, sl], kv_ref[:, sl], _NT, preferred_element_type=F32)
        p = jnp.exp(s - jnp.max(s, axis=-1, keepdims=True))
        inv = 1.0 / jnp.sum(p, axis=-1, keepdims=True)
        o = jnp.dot(p.astype(BF16), kv_ref[:, kd + h * XA_HD:kd + (h + 1) * XA_HD],
                    preferred_element_type=F32)
        heads.append((o * inv).astype(BF16))
    o = jnp.concatenate(heads, axis=-1)
    o_ref[...] = x + jnp.dot(o, wxo_ref[...], preferred_element_type=F32)


def _mix_out(x, oat, oc, wo, g, wq, kv, wxo, layer):
    b, t, d = x.shape
    tm = oat.shape[3]
    row = lambda cols: pl.BlockSpec((None, tm, cols), lambda i, j: (i, j, 0))
    return pl.pallas_call(
        _mix_out_kernel,
        grid=(b, t // tm),
        in_specs=[row(d), pl.BlockSpec((None, None, oat.shape[2], tm), lambda i, j: (i, j, 0, 0)),
                  row(oc.shape[2]), _layer_spec(wo, layer),
                  _const_spec(g.shape), _layer_spec(wq, layer),
                  pl.BlockSpec((None,) + kv.shape[1:], lambda i, j: (i, 0, 0)),
                  _layer_spec(wxo, layer)],
        out_specs=row(d),
        out_shape=jax.ShapeDtypeStruct((b, t, d), F32),
        compiler_params=_params(2),
        name="mix_out",
    )(x, oat, oc, wo, g, wq, kv, wxo)


def _cast_kernel(*refs):
    n = len(refs) // 2
    for src, dst in zip(refs[:n], refs[n:]):
        dst[...] = src[...].astype(dst.dtype)


def _to_bf16(*ws):
    layers, rows, cols = ws[0].shape
    assert all(w.shape == ws[0].shape for w in ws) and rows % CAST_ROWS == 0
    spec = pl.BlockSpec((None, CAST_ROWS, cols), lambda i, j: (i, j, 0))
    return pl.pallas_call(
        _cast_kernel,
        grid=(layers, rows // CAST_ROWS),
        in_specs=[spec] * len(ws),
        out_specs=[spec] * len(ws),
        out_shape=[jax.ShapeDtypeStruct(w.shape, BF16) for w in ws],
        compiler_params=_params(2),
        name="to_bf16",
    )(*ws)


def kernel(x, mem, ffn1_norm, ffn1_w_gate, ffn1_w_up, ffn1_w_down, mix_norm, w_in, lam_q1, lam_k1, lam_q2, lam_k2, diff_subln, cc_dw, cc_dw_b, cc_ln_g, cc_ln_b, sc_dw, w_out, xa_norm, mem_norm, xa_wq, xa_wkv, xa_wo, ffn2_norm, ffn2_w_gate, ffn2_w_up, ffn2_w_down, final_norm):
    b, t, d = x.shape
    vec = lambda v: v.reshape(1, -1).astype(F32)

    f1g, f1u, f2g, f2u, w_in = _to_bf16(ffn1_w_gate, ffn1_w_up, ffn2_w_gate, ffn2_w_up, w_in)
    f1d, f2d = _to_bf16(ffn1_w_down, ffn2_w_down)
    w_out, xa_wq, xa_wo = _to_bf16(w_out, xa_wq, xa_wo)
    (xa_wkv,) = _to_bf16(xa_wkv)

    for l in range(DEPTH):
        x = _ffn(x.reshape(b * t, d), vec(ffn1_norm[l]), f1g, f1u, f1d, l).reshape(b, t, d)

        qt, k, vt, oc = _mix_in(
            x, vec(mix_norm[l]),
            w_in[l, :, :QK_COLS].T, w_in[l, :, V_COL:V_COL + V_COLS].T, w_in, l,
            cc_dw[l].astype(F32), vec(cc_dw_b[l]), vec(cc_ln_g[l]), vec(cc_ln_b[l]),
            sc_dw[l].astype(F32))

        lam_init = 0.8 - 0.6 * math.exp(-0.3 * l)
        lam_p = jnp.stack([lam_q1[l], lam_k1[l], lam_q2[l], lam_k2[l]]).astype(F32)
        subln = jnp.broadcast_to(diff_subln[l].astype(F32)[:, None], (ATT_DV, ATT_TILE))
        oat = _diff_attn(lam_p, subln, qt, k, vt, lam_init)

        kv = _mem_kv(mem, vec(mem_norm[l]), xa_wkv, l)
        x = _mix_out(x, oat, oc, w_out, vec(xa_norm[l]), xa_wq, kv, xa_wo, l)

        last = l == DEPTH - 1
        x = _ffn(x.reshape(b * t, d), vec(ffn2_norm[l]), f2g, f2u, f2d, l,
                 vec(final_norm) if last else None).reshape(b, t, d)
    return x
```

```python
import functools
import math

import jax
import jax.numpy as jnp
from jax import lax
from jax.experimental import pallas as pl
from jax.experimental.pallas import tpu as pltpu

D_MODEL = 1024
DEPTH = 2
ATT_HEADS = 4
ATT_DV = 128
ATT_DK = 64
CC_CH = 256
CC_K = 31
SC_CH = 256
SC_K = 3
D_FF = 2816
MEM_LEN = 256
XA_HEADS = 4
XA_HD = 256
EPS = 1e-6

F32 = jnp.float32
BF16 = jnp.bfloat16

QK_COLS = ATT_HEADS * 2 * ATT_DK
V_COLS = ATT_HEADS * ATT_DV
K_COL = QK_COLS
V_COL = 2 * QK_COLS
CC_COL = V_COL + V_COLS
SC_COL = CC_COL + 2 * CC_CH

TOKEN_TILE = 512
ATT_TILE = 512
VT_ROWS = ATT_DV + 16
LOG2E = math.log2(math.e)
ATT_SCORE_BUFS = 4
ATT_LANES = 256
CAST_ROWS = 128
LANES = 128
SUBLANES = 8
CONV_ROWS = 64
CC_HALO = 32
SC_HALO = 8
VMEM_LIMIT = 56 * 1024 * 1024
MASK_VALUE = -1e30

_NT = (((1,), (1,)), ((), ()))


def _rms(x, g):
    return x * lax.rsqrt(jnp.mean(x * x, axis=-1, keepdims=True) + EPS) * g


def _const_spec(shape):
    nd = len(shape)
    return pl.BlockSpec(shape, lambda *_: (0,) * nd, pipeline_mode=pl.Buffered(1))


def _layer_spec(stacked, layer):
    nd = stacked.ndim - 1
    return pl.BlockSpec((None,) + stacked.shape[1:], lambda *_: (layer,) + (0,) * nd,
                        pipeline_mode=pl.Buffered(1))


def _params(n_axes):
    return pltpu.CompilerParams(
        dimension_semantics=("arbitrary",) * n_axes, vmem_limit_bytes=VMEM_LIMIT)


def _fold_bits(v):
    bits = pltpu.bitcast(v, jnp.uint32)
    folded = None
    for i in range(0, bits.shape[0], SUBLANES):
        for c in range(0, bits.shape[1], LANES):
            tile = bits[i:i + SUBLANES, c:c + LANES]
            folded = tile if folded is None else folded | tile
    return folded


def _order_after(dst_ref, bits):
    half = jnp.uint32(16)
    zero = pltpu.bitcast(lax.shift_right_logical(lax.shift_right_logical(bits, half), half), F32)
    zero = jnp.concatenate([zero, zero], axis=0).astype(dst_ref.dtype)
    dst_ref[...] = dst_ref[...] + zero


def _ffn_kernel(x_ref, g_ref, wg_ref, wu_ref, wd_ref, *rest, final):
    if final:
        fg_ref, o_ref = rest
    else:
        (o_ref,) = rest
    x = x_ref[...]
    h = _rms(x, g_ref[...]).astype(BF16)
    gate = jnp.dot(h, wg_ref[...], preferred_element_type=F32)
    up = jnp.dot(h, wu_ref[...], preferred_element_type=F32)
    act = (gate * jax.nn.sigmoid(gate) * up).astype(BF16)
    y = x + 0.5 * jnp.dot(act, wd_ref[...], preferred_element_type=F32)
    if final:
        y = _rms(y, fg_ref[...])
    o_ref[...] = y


def _ffn(x, g, wg, wu, wd, layer, final_g=None):
    n, d = x.shape
    tm = TOKEN_TILE
    final = final_g is not None
    row_spec = pl.BlockSpec((tm, d), lambda i: (i, 0))
    in_specs = [row_spec, _const_spec((1, d)), _layer_spec(wg, layer), _layer_spec(wu, layer),
                _layer_spec(wd, layer)]
    args = [x, g, wg, wu, wd]
    if final:
        in_specs.append(_const_spec((1, d)))
        args.append(final_g)
    return pl.pallas_call(
        functools.partial(_ffn_kernel, final=final),
        grid=(n // tm,),
        in_specs=in_specs,
        out_specs=row_spec,
        out_shape=jax.ShapeDtypeStruct((n, d), F32),
        compiler_params=_params(1),
        name="ffn_final" if final else "ffn",
    )(*args)


def _shifted_copies(ext_ref, shift_ref):
    rows = shift_ref.shape[1]
    for b in range(1, SUBLANES):
        shift_ref[b - 1] = ext_ref[pl.ds(b, rows), :]


def _dw_conv(ext_ref, w_ref, out_ref, *, taps, halo, blocks, shift_ref=None):
    base = halo - (taps - 1)
    folded = None
    for r in blocks:
        acc = None
        for j in range(taps):
            off = base + j
            if shift_ref is None or off % SUBLANES == 0:
                rows = ext_ref[pl.ds(off + r * CONV_ROWS, CONV_ROWS), :]
            else:
                rows = shift_ref[off % SUBLANES - 1,
                                 pl.ds(off - off % SUBLANES + r * CONV_ROWS, CONV_ROWS), :]
            term = rows * w_ref[pl.ds(j, 1), :]
            acc = term if acc is None else acc + term
        out_ref[pl.ds(r * CONV_ROWS, CONV_ROWS), :] = acc
        folded = _fold_bits(acc) if folded is None else folded | _fold_bits(acc)
    return folded


CC_BLOCK_SPLIT = ((0,), (1,), (2, 3), (4, 5), (6, 7))


def _mix_in_kernel(x_ref, g_ref, wqt_ref, wvt_ref, win_ref,
                   ccw_ref, ccb_ref, lng_ref, lnb_ref, scw_ref,
                   qt_ref, k_ref, vt_ref, oc_ref,
                   h_ref, ccx_ref, scx_ref, gate_ref, ccn_ref, scn_ref, gaten_ref, ccs_ref,
                   ccy_ref, scy_ref, *, tiles_per_seq):
    tm = x_ref.shape[0]
    g = pl.program_id(0)
    cc_conv = functools.partial(_dw_conv, ccx_ref, ccw_ref, ccy_ref, taps=CC_K, halo=CC_HALO,
                                shift_ref=ccs_ref)

    def cc_conv_then_next_projection(i):
        done = cc_conv(blocks=CC_BLOCK_SPLIT[i])
        _order_after(h_ref.at[pl.ds(0, 16), pl.ds(0, 128)], done)

    @pl.when(g == 0)
    def _():
        ccx_ref[...] = jnp.zeros(ccx_ref.shape, F32)
        scx_ref[...] = jnp.zeros(scx_ref.shape, F32)
        gate_ref[...] = jnp.zeros(gate_ref.shape, F32)

    h_ref[...] = _rms(x_ref[...], g_ref[...]).astype(BF16)
    _shifted_copies(ccx_ref, ccs_ref)

    qt = lax.dot_general(wqt_ref[...], h_ref[...], _NT, preferred_element_type=F32)
    qt_ref[...] = (qt * (LOG2E * ATT_DK ** -0.5)).astype(BF16)
    cc_conv_then_next_projection(0)

    k_ref[...] = jnp.dot(h_ref[...], win_ref[:, pl.ds(K_COL, QK_COLS)],
                         preferred_element_type=F32).astype(BF16)
    cc_conv_then_next_projection(1)

    vt = lax.dot_general(wvt_ref[...], h_ref[...], _NT, preferred_element_type=F32).astype(BF16)
    for hd in range(ATT_HEADS):
        vt_ref[pl.ds(hd * VT_ROWS, ATT_DV), :] = vt[hd * ATT_DV:(hd + 1) * ATT_DV]
        vt_ref[pl.ds(hd * VT_ROWS + ATT_DV, VT_ROWS - ATT_DV), :] = jnp.ones(
            (VT_ROWS - ATT_DV, tm), BF16)
    cc_conv_then_next_projection(2)

    zc = jnp.dot(h_ref[...], win_ref[:, pl.ds(CC_COL, 2 * CC_CH)],
                 preferred_element_type=F32)
    ccn_ref[...] = zc[:, :CC_CH] * jax.nn.sigmoid(zc[:, CC_CH:])
    cc_conv_then_next_projection(3)

    zs = jnp.dot(h_ref[...], win_ref[:, pl.ds(SC_COL, 3 * SC_CH)],
                 preferred_element_type=F32)
    scn_ref[...] = zs[:, SC_CH:2 * SC_CH] * zs[:, 2 * SC_CH:]
    gaten_ref[...] = zs[:, :SC_CH]
    cc_conv(blocks=CC_BLOCK_SPLIT[4])
    _dw_conv(scx_ref, scw_ref, scy_ref, taps=SC_K, halo=SC_HALO, blocks=range(tm // CONV_ROWS))
    oc_ref[:, CC_CH:] = (gate_ref[...] * scy_ref[...]).astype(BF16)

    u = ccy_ref[...] + ccb_ref[...]
    mu = jnp.mean(u, axis=-1, keepdims=True)
    var = jnp.mean(jnp.square(u - mu), axis=-1, keepdims=True)
    y = (u - mu) * lax.rsqrt(var + EPS) * lng_ref[...] + lnb_ref[...]
    oc_ref[:, :CC_CH] = (y * jax.nn.sigmoid(y)).astype(BF16)

    seq_start = lax.rem(g, tiles_per_seq) == 0
    ccx_ref[pl.ds(0, CC_HALO), :] = jnp.where(seq_start, 0.0, ccx_ref[pl.ds(tm, CC_HALO), :])
    ccx_ref[pl.ds(CC_HALO, tm), :] = ccn_ref[...]
    scx_ref[pl.ds(0, SC_HALO), :] = jnp.where(seq_start, 0.0, scx_ref[pl.ds(tm, SC_HALO), :])
    scx_ref[pl.ds(SC_HALO, tm), :] = scn_ref[...]
    gate_ref[...] = gaten_ref[...]


def _mix_in(x, g, wqt, wvt, w_in, layer, ccw, ccb, lng, lnb, scw):
    b, t, d = x.shape
    tm = ATT_TILE
    nt = t // tm
    n_tiles = b * nt
    qk_cols = QK_COLS
    v_cols = ATT_HEADS * VT_ROWS
    proj = lambda i: jnp.minimum(i, n_tiles - 1)
    conv = lambda i: jnp.maximum(i - 1, 0)
    row = lambda cols, dt, tile: (
        pl.BlockSpec((None, tm, cols), lambda i: (tile(i) // nt, tile(i) % nt, 0)),
        jax.ShapeDtypeStruct((b, t, cols), dt))
    k_spec, k_shape = row(qk_cols, BF16, proj)
    oc_spec, oc_shape = row(CC_CH + SC_CH, BF16, conv)
    slab = lambda rows: (
        pl.BlockSpec((None, None, rows, tm), lambda i: (proj(i) // nt, proj(i) % nt, 0, 0)),
        jax.ShapeDtypeStruct((b, nt, rows, tm), BF16))
    qt_spec, qt_shape = slab(qk_cols)
    vt_spec, vt_shape = slab(v_cols)
    consts = [ccw, ccb, lng, lnb, scw]
    return pl.pallas_call(
        functools.partial(_mix_in_kernel, tiles_per_seq=nt),
        grid=(n_tiles + 1,),
        in_specs=[pl.BlockSpec((None, tm, d), lambda i: (proj(i) // nt, proj(i) % nt, 0)),
                  _const_spec(g.shape), _const_spec(wqt.shape), _const_spec(wvt.shape),
                  _layer_spec(w_in, layer)]
        + [_const_spec(c.shape) for c in consts],
        out_specs=[qt_spec, k_spec, vt_spec, oc_spec],
        out_shape=[qt_shape, k_shape, vt_shape, oc_shape],
        scratch_shapes=[
            pltpu.VMEM((tm, d), BF16),
            pltpu.VMEM((CC_HALO + tm, CC_CH), F32),
            pltpu.VMEM((SC_HALO + tm, SC_CH), F32),
            pltpu.VMEM((tm, SC_CH), F32),
            pltpu.VMEM((tm, CC_CH), F32),
            pltpu.VMEM((tm, SC_CH), F32),
            pltpu.VMEM((tm, SC_CH), F32),
            pltpu.VMEM((SUBLANES - 1, CC_HALO + tm - SUBLANES, CC_CH), F32),
            pltpu.VMEM((tm, CC_CH), F32),
            pltpu.VMEM((tm, SC_CH), F32),
        ],
        compiler_params=_params(1),
        name="mix_in",
    )(x, g, wqt, wvt, w_in, *consts)


def _diff_attn_kernel(lam_ref, subln_ref, qt_ref, k_ref, vt_ref, o_ref,
                      qs_ref, bias_ref, m_ref, acc_ref, *score_refs, lam_init):
    s_refs, cmax_refs = score_refs[:ATT_SCORE_BUFS], score_refs[ATT_SCORE_BUFS:]
    nt, _, tq = qt_ref.shape
    tk = tq
    head = pl.program_id(1)
    slope = jnp.left_shift(1, 2 * (ATT_HEADS - 1 - head)).astype(F32) * (LOG2E / 256.0)

    jj = lax.broadcasted_iota(jnp.int32, (tk, 2 * tq), 0)
    ii = lax.broadcasted_iota(jnp.int32, (tk, 2 * tq), 1)
    ii = jnp.where(ii >= tq, ii - tq, ii)
    bias = jj.astype(F32) * slope
    bias_ref[0] = bias
    bias_ref[1] = jnp.where(jj <= ii, bias, MASK_VALUE)

    for i in range(nt):
        qt = qt_ref[i]
        row = lax.broadcasted_iota(jnp.int32, qt.shape, 0)
        zero = jnp.zeros_like(qt)
        qs_ref[i, :, pl.ds(0, tq)] = jnp.where(row < ATT_DK, qt, zero)
        qs_ref[i, :, pl.ds(tq, tq)] = jnp.where(row >= ATT_DK, qt, zero)

    m_ref[...] = jnp.full(m_ref.shape, MASK_VALUE, F32)
    acc_ref[...] = jnp.zeros(acc_ref.shape, F32)

    def scores(item, s_ref, cmax_ref):
        qi, c = item
        j0 = pl.multiple_of(c * tk, tk)
        s = jnp.dot(k_ref[pl.ds(j0, tk), :], qs_ref[qi], preferred_element_type=F32)
        s = s + bias_ref[(c == qi).astype(jnp.int32)]
        s_ref[...] = s
        cmax_ref[...] = jnp.max(s, axis=0, keepdims=True)

    def accumulate(item, s_ref, cmax_ref):
        qi, c = item
        shift = slope * (c * tk).astype(F32)
        vtc = vt_ref[c]
        for nb in range(2 * tq // ATT_LANES):
            cols = pl.ds(nb * ATT_LANES, ATT_LANES)
            m_prev = m_ref[qi, :, cols] - shift
            m_new = jnp.maximum(m_prev, cmax_ref[:, cols])
            alpha = jnp.exp2(m_prev - m_new)
            p = jnp.exp2(s_ref[:, cols] - m_new).astype(BF16)
            m_ref[qi, :, cols] = m_new + shift
            pv = jnp.dot(vtc, p, preferred_element_type=F32)
            acc_ref[qi, :, cols] = alpha * acc_ref[qi, :, cols] + pv

    def following(item):
        qi, c = item
        last = c == qi
        return qi + last.astype(jnp.int32), jnp.where(last, 0, c + 1)

    n_items = nt * (nt + 1) // 2
    n_bufs = len(s_refs)
    assert n_items % n_bufs == 0
    bufs = list(zip(s_refs, cmax_refs))

    def following_clamped(item):
        qi, c = following(item)
        past_end = qi == nt
        return jnp.where(past_end, 0, qi), jnp.where(past_end, 0, c)

    ahead = [(jnp.int32(0), jnp.int32(0))]
    for _ in range(n_bufs - 2):
        ahead.append(following_clamped(ahead[-1]))
    for item, buf in zip(ahead, bufs):
        scores(item, *buf)

    def trip(_, carry):
        items = list(carry)
        for slot in range(n_bufs):
            nxt = following_clamped(items[-1])
            scores(nxt, *bufs[(slot + n_bufs - 1) % n_bufs])
            accumulate(items[0], *bufs[slot])
            items = items[1:] + [nxt]
        return tuple(items)

    lax.fori_loop(0, n_items // n_bufs, trip, tuple(ahead))

    lam_p = lam_ref[...]
    lam = (jnp.exp(jnp.sum(lam_p[0:1] * lam_p[1:2], axis=-1, keepdims=True))
           - jnp.exp(jnp.sum(lam_p[2:3] * lam_p[3:4], axis=-1, keepdims=True)) + lam_init)
    for i in range(nt):
        acc = acc_ref[i, pl.ds(0, ATT_DV), :]
        inv_l = 1.0 / acc_ref[i, pl.ds(ATT_DV, 1), :]
        o = acc[:, :tq] * inv_l[:, :tq] - lam * (acc[:, tq:] * inv_l[:, tq:])
        inv_rms = lax.rsqrt(jnp.mean(o * o, axis=0, keepdims=True) + EPS)
        o_ref[i] = (o * inv_rms * subln_ref[...] * (1.0 - lam_init)).astype(BF16)


def _diff_attn(lam_p, subln, qt, k, vt, lam_init):
    b, t, _ = k.shape
    tq = ATT_TILE
    nt = t // tq
    per_head = lambda rows: pl.BlockSpec((None, nt, rows, tq), lambda i, h: (i, 0, h, 0))
    return pl.pallas_call(
        functools.partial(_diff_attn_kernel, lam_init=lam_init),
        grid=(b, ATT_HEADS),
        in_specs=[
            _const_spec(lam_p.shape),
            _const_spec(subln.shape),
            per_head(2 * ATT_DK),
            pl.BlockSpec((None, t, 2 * ATT_DK), lambda i, h: (i, 0, h)),
            per_head(VT_ROWS),
        ],
        out_specs=per_head(ATT_DV),
        out_shape=jax.ShapeDtypeStruct((b, nt, ATT_HEADS * ATT_DV, tq), BF16),
        scratch_shapes=[
            pltpu.VMEM((nt, 2 * ATT_DK, 2 * tq), BF16),
            pltpu.VMEM((2, tq, 2 * tq), F32),
            pltpu.VMEM((nt, 1, 2 * tq), F32),
            pltpu.VMEM((nt, VT_ROWS, 2 * tq), F32),
        ]
        + [pltpu.VMEM((tq, 2 * tq), F32)] * ATT_SCORE_BUFS
        + [pltpu.VMEM((1, 2 * tq), F32)] * ATT_SCORE_BUFS,
        compiler_params=_params(2),
        name="diff_attn",
    )(lam_p, subln, qt, k, vt)


def _mem_kv_kernel(mem_ref, g_ref, w_ref, kv_ref):
    h = _rms(mem_ref[...], g_ref[...]).astype(BF16)
    kv_ref[...] = jnp.dot(h, w_ref[...], preferred_element_type=F32).astype(BF16)


def _mem_kv(mem, g, w, layer):
    b, m, d = mem.shape
    n = w.shape[2]
    return pl.pallas_call(
        _mem_kv_kernel,
        grid=(b,),
        in_specs=[pl.BlockSpec((None, m, d), lambda i: (i, 0, 0)), _const_spec(g.shape),
                  _layer_spec(w, layer)],
        out_specs=pl.BlockSpec((None, m, n), lambda i: (i, 0, 0)),
        out_shape=jax.ShapeDtypeStruct((b, m, n), BF16),
        compiler_params=_params(1),
        name="mem_kv",
    )(mem, g, w)


_TN = (((0,), (0,)), ((), ()))


def _mix_out_kernel(x_ref, oat_ref, oc_ref, wo_ref, g_ref, wq_ref, kv_ref, wxo_ref, o_ref):
    n_att = oat_ref.shape[0]
    x = x_ref[...]
    x = x + lax.dot_general(oat_ref[...], wo_ref[pl.ds(0, n_att), :], _TN,
                            preferred_element_type=F32)
    x = x + jnp.dot(oc_ref[...], wo_ref[pl.ds(n_att, oc_ref.shape[1]), :],
                    preferred_element_type=F32)

    hq = jnp.dot(_rms(x, g_ref[...]).astype(BF16), wq_ref[...], preferred_element_type=F32)
    hq = (hq * (XA_HD ** -0.5)).astype(BF16)
    kd = XA_HEADS * XA_HD
    heads = []
    for h in range(XA_HEADS):
        sl = slice(h * XA_HD, (h + 1) * XA_HD)
        s = lax.dot_general(hq[:, sl], kv_ref[:, sl], _NT, preferred_element_type=F32)
        p = jnp.exp(s - jnp.max(s, axis=-1, keepdims=True))
        inv = 1.0 / jnp.sum(p, axis=-1, keepdims=True)
        o = jnp.dot(p.astype(BF16), kv_ref[:, kd + h * XA_HD:kd + (h + 1) * XA_HD],
                    preferred_element_type=F32)
        heads.append((o * inv).astype(BF16))
    o = jnp.concatenate(heads, axis=-1)
    o_ref[...] = x + jnp.dot(o, wxo_ref[...], preferred_element_type=F32)


def _mix_out(x, oat, oc, wo, g, wq, kv, wxo, layer):
    b, t, d = x.shape
    tm = oat.shape[3]
    row = lambda cols: pl.BlockSpec((None, tm, cols), lambda i, j: (i, j, 0))
    return pl.pallas_call(
        _mix_out_kernel,
        grid=(b, t // tm),
        in_specs=[row(d), pl.BlockSpec((None, None, oat.shape[2], tm), lambda i, j: (i, j, 0, 0)),
                  row(oc.shape[2]), _layer_spec(wo, layer),
                  _const_spec(g.shape), _layer_spec(wq, layer),
                  pl.BlockSpec((None,) + kv.shape[1:], lambda i, j: (i, 0, 0)),
                  _layer_spec(wxo, layer)],
        out_specs=row(d),
        out_shape=jax.ShapeDtypeStruct((b, t, d), F32),
        compiler_params=_params(2),
        name="mix_out",
    )(x, oat, oc, wo, g, wq, kv, wxo)


def _cast_kernel(*refs):
    n = len(refs) // 2
    for src, dst in zip(refs[:n], refs[n:]):
        dst[...] = src[...].astype(dst.dtype)


def _to_bf16(*ws):
    layers, rows, cols = ws[0].shape
    assert all(w.shape == ws[0].shape for w in ws) and rows % CAST_ROWS == 0
    spec = pl.BlockSpec((None, CAST_ROWS, cols), lambda i, j: (i, j, 0))
    return pl.pallas_call(
        _cast_kernel,
        grid=(layers, rows // CAST_ROWS),
        in_specs=[spec] * len(ws),
        out_specs=[spec] * len(ws),
        out_shape=[jax.ShapeDtypeStruct(w.shape, BF16) for w in ws],
        compiler_params=_params(2),
        name="to_bf16",
    )(*ws)


def kernel(x, mem, ffn1_norm, ffn1_w_gate, ffn1_w_up, ffn1_w_down, mix_norm, w_in, lam_q1, lam_k1, lam_q2, lam_k2, diff_subln, cc_dw, cc_dw_b, cc_ln_g, cc_ln_b, sc_dw, w_out, xa_norm, mem_norm, xa_wq, xa_wkv, xa_wo, ffn2_norm, ffn2_w_gate, ffn2_w_up, ffn2_w_down, final_norm):
    b, t, d = x.shape
    vec = lambda v: v.reshape(1, -1).astype(F32)

    f1g, f1u, f2g, f2u, w_in = _to_bf16(ffn1_w_gate, ffn1_w_up, ffn2_w_gate, ffn2_w_up, w_in)
    f1d, f2d = _to_bf16(ffn1_w_down, ffn2_w_down)
    w_out, xa_wq, xa_wo = _to_bf16(w_out, xa_wq, xa_wo)
    (xa_wkv,) = _to_bf16(xa_wkv)

    for l in range(DEPTH):
        x = _ffn(x.reshape(b * t, d), vec(ffn1_norm[l]), f1g, f1u, f1d, l).reshape(b, t, d)

        qt, k, vt, oc = _mix_in(
            x, vec(mix_norm[l]),
            w_in[l, :, :QK_COLS].T, w_in[l, :, V_COL:V_COL + V_COLS].T, w_in, l,
            cc_dw[l].astype(F32), vec(cc_dw_b[l]), vec(cc_ln_g[l]), vec(cc_ln_b[l]),
            sc_dw[l].astype(F32))

        lam_init = 0.8 - 0.6 * math.exp(-0.3 * l)
        lam_p = jnp.stack([lam_q1[l], lam_k1[l], lam_q2[l], lam_k2[l]]).astype(F32)
        subln = jnp.broadcast_to(diff_subln[l].astype(F32)[:, None], (ATT_DV, ATT_TILE))
        oat = _diff_attn(lam_p, subln, qt, k, vt, lam_init)

        kv = _mem_kv(mem, vec(mem_norm[l]), xa_wkv, l)
        x = _mix_out(x, oat, oc, w_out, vec(xa_norm[l]), xa_wq, kv, xa_wo, l)

        last = l == DEPTH - 1
        x = _ffn(x.reshape(b * t, d), vec(ffn2_norm[l]), f2g, f2u, f2d, l,
                 vec(final_norm) if last else None).reshape(b, t, d)
    return x
```

```python
import functools
import math

import jax
import jax.numpy as jnp
from jax import lax
from jax.experimental import pallas as pl
from jax.experimental.pallas import tpu as pltpu

D_MODEL = 1024
DEPTH = 2
ATT_HEADS = 4
ATT_DV = 128
ATT_DK = 64
CC_CH = 256
CC_K = 31
SC_CH = 256
SC_K = 3
D_FF = 2816
MEM_LEN = 256
XA_HEADS = 4
XA_HD = 256
EPS = 1e-6

F32 = jnp.float32
BF16 = jnp.bfloat16

QK_COLS = ATT_HEADS * 2 * ATT_DK
V_COLS = ATT_HEADS * ATT_DV
K_COL = QK_COLS
V_COL = 2 * QK_COLS
CC_COL = V_COL + V_COLS
SC_COL = CC_COL + 2 * CC_CH

TOKEN_TILE = 512
ATT_TILE = 512
VT_ROWS = ATT_DV + 16
LOG2E = math.log2(math.e)
ATT_SCORE_BUFS = 4
ATT_LANES = 256
WEIGHT_CHUNKS = 16
LANES = 128
SUBLANES = 8
CONV_ROWS = 64
CC_HALO = 32
SC_HALO = 8
VMEM_LIMIT = 56 * 1024 * 1024
MASK_VALUE = -1e30

_NT = (((1,), (1,)), ((), ()))


def _rms(x, g):
    return x * lax.rsqrt(jnp.mean(x * x, axis=-1, keepdims=True) + EPS) * g


def _const_spec(shape):
    nd = len(shape)
    return pl.BlockSpec(shape, lambda *_: (0,) * nd, pipeline_mode=pl.Buffered(1))


_HBM = pl.BlockSpec(memory_space=pl.ANY)


class _ResidentWeight:
    def __init__(self, stacked, layer, chunk_rows):
        _, self.rows, self.cols = stacked.shape
        assert self.rows % chunk_rows == 0
        self.layer, self.chunk_rows = layer, chunk_rows

    def scratch(self):
        return [pltpu.VMEM((self.rows, self.cols), BF16),
                pltpu.VMEM((2, self.chunk_rows, self.cols), F32),
                pltpu.SemaphoreType.DMA((2,))]

    def load(self, hbm_ref, resident_ref, stage_ref, sem_ref):
        rc = self.chunk_rows
        n_chunks = self.rows // rc

        def copy(c):
            return pltpu.make_async_copy(hbm_ref.at[self.layer, pl.ds(c * rc, rc), :],
                                         stage_ref.at[c % 2], sem_ref.at[c % 2])

        copy(0).start()
        for c in range(n_chunks):
            if c + 1 < n_chunks:
                copy(c + 1).start()
            copy(c).wait()
            resident_ref[pl.ds(c * rc, rc), :] = stage_ref[c % 2].astype(BF16)


def _params(n_axes):
    return pltpu.CompilerParams(
        dimension_semantics=("arbitrary",) * n_axes, vmem_limit_bytes=VMEM_LIMIT)


def _fold_bits(v):
    bits = pltpu.bitcast(v, jnp.uint32)
    folded = None
    for i in range(0, bits.shape[0], SUBLANES):
        for c in range(0, bits.shape[1], LANES):
            tile = bits[i:i + SUBLANES, c:c + LANES]
            folded = tile if folded is None else folded | tile
    return folded


def _order_after(dst_ref, bits):
    half = jnp.uint32(16)
    zero = pltpu.bitcast(lax.shift_right_logical(lax.shift_right_logical(bits, half), half), F32)
    zero = jnp.concatenate([zero, zero], axis=0).astype(dst_ref.dtype)
    dst_ref[...] = dst_ref[...] + zero


def _ffn_kernel(x_ref, g_ref, wg_hbm, wu_hbm, wd_hbm, *rest, final, weights):
    if final:
        fg_ref, o_ref, *scratch = rest
    else:
        o_ref, *scratch = rest
    wg_ref, wu_ref, wd_ref = scratch[0], scratch[3], scratch[6]

    @pl.when(pl.program_id(0) == 0)
    def _():
        for w, hbm, i in zip(weights, (wg_hbm, wu_hbm, wd_hbm), (0, 3, 6)):
            w.load(hbm, *scratch[i:i + 3])

    x = x_ref[...]
    h = _rms(x, g_ref[...]).astype(BF16)
    gate = jnp.dot(h, wg_ref[...], preferred_element_type=F32)
    up = jnp.dot(h, wu_ref[...], preferred_element_type=F32)
    act = (gate * jax.nn.sigmoid(gate) * up).astype(BF16)
    y = x + 0.5 * jnp.dot(act, wd_ref[...], preferred_element_type=F32)
    if final:
        y = _rms(y, fg_ref[...])
    o_ref[...] = y


def _ffn(x, g, wg, wu, wd, layer, final_g=None):
    n, d = x.shape
    tm = TOKEN_TILE
    final = final_g is not None
    row_spec = pl.BlockSpec((tm, d), lambda i: (i, 0))
    weights = [_ResidentWeight(w, layer, w.shape[1] // WEIGHT_CHUNKS) for w in (wg, wu, wd)]
    in_specs = [row_spec, _const_spec((1, d)), _HBM, _HBM, _HBM]
    args = [x, g, wg, wu, wd]
    if final:
        in_specs.append(_const_spec((1, d)))
        args.append(final_g)
    return pl.pallas_call(
        functools.partial(_ffn_kernel, final=final, weights=weights),
        grid=(n // tm,),
        in_specs=in_specs,
        out_specs=row_spec,
        out_shape=jax.ShapeDtypeStruct((n, d), F32),
        scratch_shapes=[s for w in weights for s in w.scratch()],
        compiler_params=_params(1),
        name="ffn_final" if final else "ffn",
    )(*args)


def _shifted_copies(ext_ref, shift_ref):
    rows = shift_ref.shape[1]
    for b in range(1, SUBLANES):
        shift_ref[b - 1] = ext_ref[pl.ds(b, rows), :]


def _dw_conv(ext_ref, w_ref, out_ref, *, taps, halo, blocks, shift_ref=None):
    base = halo - (taps - 1)
    folded = None
    for r in blocks:
        acc = None
        for j in range(taps):
            off = base + j
            if shift_ref is None or off % SUBLANES == 0:
                rows = ext_ref[pl.ds(off + r * CONV_ROWS, CONV_ROWS), :]
            else:
                rows = shift_ref[off % SUBLANES - 1,
                                 pl.ds(off - off % SUBLANES + r * CONV_ROWS, CONV_ROWS), :]
            term = rows * w_ref[pl.ds(j, 1), :]
            acc = term if acc is None else acc + term
        out_ref[pl.ds(r * CONV_ROWS, CONV_ROWS), :] = acc
        folded = _fold_bits(acc) if folded is None else folded | _fold_bits(acc)
    return folded


CC_BLOCK_SPLIT = ((0,), (1,), (2, 3), (4, 5), (6, 7))


def _mix_in_kernel(x_ref, g_ref, wqt_ref, wvt_ref, win_hbm,
                   ccw_ref, ccb_ref, lng_ref, lnb_ref, scw_ref,
                   qt_ref, k_ref, vt_ref, oc_ref,
                   h_ref, ccx_ref, scx_ref, gate_ref, ccn_ref, scn_ref, gaten_ref, ccs_ref,
                   ccy_ref, scy_ref, win_ref, win_stage_ref, win_sem, *, tiles_per_seq, w_in):
    tm = x_ref.shape[0]
    g = pl.program_id(0)
    cc_conv = functools.partial(_dw_conv, ccx_ref, ccw_ref, ccy_ref, taps=CC_K, halo=CC_HALO,
                                shift_ref=ccs_ref)

    def cc_conv_then_next_projection(i):
        done = cc_conv(blocks=CC_BLOCK_SPLIT[i])
        _order_after(h_ref.at[pl.ds(0, 16), pl.ds(0, 128)], done)

    @pl.when(g == 0)
    def _():
        ccx_ref[...] = jnp.zeros(ccx_ref.shape, F32)
        scx_ref[...] = jnp.zeros(scx_ref.shape, F32)
        gate_ref[...] = jnp.zeros(gate_ref.shape, F32)
        w_in.load(win_hbm, win_ref, win_stage_ref, win_sem)

    h_ref[...] = _rms(x_ref[...], g_ref[...]).astype(BF16)
    _shifted_copies(ccx_ref, ccs_ref)

    qt = lax.dot_general(wqt_ref[...], h_ref[...], _NT, preferred_element_type=F32)
    qt_ref[...] = (qt * (LOG2E * ATT_DK ** -0.5)).astype(BF16)
    cc_conv_then_next_projection(0)

    k_ref[...] = jnp.dot(h_ref[...], win_ref[:, pl.ds(K_COL, QK_COLS)],
                         preferred_element_type=F32).astype(BF16)
    cc_conv_then_next_projection(1)

    vt = lax.dot_general(wvt_ref[...], h_ref[...], _NT, preferred_element_type=F32).astype(BF16)
    for hd in range(ATT_HEADS):
        vt_ref[pl.ds(hd * VT_ROWS, ATT_DV), :] = vt[hd * ATT_DV:(hd + 1) * ATT_DV]
        vt_ref[pl.ds(hd * VT_ROWS + ATT_DV, VT_ROWS - ATT_DV), :] = jnp.ones(
            (VT_ROWS - ATT_DV, tm), BF16)
    cc_conv_then_next_projection(2)

    zc = jnp.dot(h_ref[...], win_ref[:, pl.ds(CC_COL, 2 * CC_CH)],
                 preferred_element_type=F32)
    ccn_ref[...] = zc[:, :CC_CH] * jax.nn.sigmoid(zc[:, CC_CH:])
    cc_conv_then_next_projection(3)

    zs = jnp.dot(h_ref[...], win_ref[:, pl.ds(SC_COL, 3 * SC_CH)],
                 preferred_element_type=F32)
    scn_ref[...] = zs[:, SC_CH:2 * SC_CH] * zs[:, 2 * SC_CH:]
    gaten_ref[...] = zs[:, :SC_CH]
    cc_conv(blocks=CC_BLOCK_SPLIT[4])
    _dw_conv(scx_ref, scw_ref, scy_ref, taps=SC_K, halo=SC_HALO, blocks=range(tm // CONV_ROWS))
    oc_ref[:, CC_CH:] = (gate_ref[...] * scy_ref[...]).astype(BF16)

    u = ccy_ref[...] + ccb_ref[...]
    mu = jnp.mean(u, axis=-1, keepdims=True)
    var = jnp.mean(jnp.square(u - mu), axis=-1, keepdims=True)
    y = (u - mu) * lax.rsqrt(var + EPS) * lng_ref[...] + lnb_ref[...]
    oc_ref[:, :CC_CH] = (y * jax.nn.sigmoid(y)).astype(BF16)

    seq_start = lax.rem(g, tiles_per_seq) == 0
    ccx_ref[pl.ds(0, CC_HALO), :] = jnp.where(seq_start, 0.0, ccx_ref[pl.ds(tm, CC_HALO), :])
    ccx_ref[pl.ds(CC_HALO, tm), :] = ccn_ref[...]
    scx_ref[pl.ds(0, SC_HALO), :] = jnp.where(seq_start, 0.0, scx_ref[pl.ds(tm, SC_HALO), :])
    scx_ref[pl.ds(SC_HALO, tm), :] = scn_ref[...]
    gate_ref[...] = gaten_ref[...]


def _mix_in(x, g, wqt, wvt, w_in, layer, ccw, ccb, lng, lnb, scw):
    b, t, d = x.shape
    tm = ATT_TILE
    nt = t // tm
    n_tiles = b * nt
    qk_cols = QK_COLS
    v_cols = ATT_HEADS * VT_ROWS
    proj = lambda i: jnp.minimum(i, n_tiles - 1)
    conv = lambda i: jnp.maximum(i - 1, 0)
    row = lambda cols, dt, tile: (
        pl.BlockSpec((None, tm, cols), lambda i: (tile(i) // nt, tile(i) % nt, 0)),
        jax.ShapeDtypeStruct((b, t, cols), dt))
    k_spec, k_shape = row(qk_cols, BF16, proj)
    oc_spec, oc_shape = row(CC_CH + SC_CH, BF16, conv)
    slab = lambda rows: (
        pl.BlockSpec((None, None, rows, tm), lambda i: (proj(i) // nt, proj(i) % nt, 0, 0)),
        jax.ShapeDtypeStruct((b, nt, rows, tm), BF16))
    qt_spec, qt_shape = slab(qk_cols)
    vt_spec, vt_shape = slab(v_cols)
    consts = [ccw, ccb, lng, lnb, scw]
    win = _ResidentWeight(w_in, layer, w_in.shape[1] // WEIGHT_CHUNKS)
    return pl.pallas_call(
        functools.partial(_mix_in_kernel, tiles_per_seq=nt, w_in=win),
        grid=(n_tiles + 1,),
        in_specs=[pl.BlockSpec((None, tm, d), lambda i: (proj(i) // nt, proj(i) % nt, 0)),
                  _const_spec(g.shape), _const_spec(wqt.shape), _const_spec(wvt.shape), _HBM]
        + [_const_spec(c.shape) for c in consts],
        out_specs=[qt_spec, k_spec, vt_spec, oc_spec],
        out_shape=[qt_shape, k_shape, vt_shape, oc_shape],
        scratch_shapes=[
            pltpu.VMEM((tm, d), BF16),
            pltpu.VMEM((CC_HALO + tm, CC_CH), F32),
            pltpu.VMEM((SC_HALO + tm, SC_CH), F32),
            pltpu.VMEM((tm, SC_CH), F32),
            pltpu.VMEM((tm, CC_CH), F32),
            pltpu.VMEM((tm, SC_CH), F32),
            pltpu.VMEM((tm, SC_CH), F32),
            pltpu.VMEM((SUBLANES - 1, CC_HALO + tm - SUBLANES, CC_CH), F32),
            pltpu.VMEM((tm, CC_CH), F32),
            pltpu.VMEM((tm, SC_CH), F32),
        ] + win.scratch(),
        compiler_params=_params(1),
        name="mix_in",
    )(x, g, wqt, wvt, w_in, *consts)


def _diff_attn_kernel(lam_ref, subln_ref, qt_ref, k_ref, vt_ref, o_ref,
                      qs_ref, bias_ref, m_ref, acc_ref, *score_refs, lam_init):
    s_refs, cmax_refs = score_refs[:ATT_SCORE_BUFS], score_refs[ATT_SCORE_BUFS:]
    nt, _, tq = qt_ref.shape
    tk = tq
    head = pl.program_id(1)
    slope = jnp.left_shift(1, 2 * (ATT_HEADS - 1 - head)).astype(F32) * (LOG2E / 256.0)

    jj = lax.broadcasted_iota(jnp.int32, (tk, 2 * tq), 0)
    ii = lax.broadcasted_iota(jnp.int32, (tk, 2 * tq), 1)
    ii = jnp.where(ii >= tq, ii - tq, ii)
    bias = jj.astype(F32) * slope
    bias_ref[0] = bias
    bias_ref[1] = jnp.where(jj <= ii, bias, MASK_VALUE)

    for i in range(nt):
        qt = qt_ref[i]
        row = lax.broadcasted_iota(jnp.int32, qt.shape, 0)
        zero = jnp.zeros_like(qt)
        qs_ref[i, :, pl.ds(0, tq)] = jnp.where(row < ATT_DK, qt, zero)
        qs_ref[i, :, pl.ds(tq, tq)] = jnp.where(row >= ATT_DK, qt, zero)

    m_ref[...] = jnp.full(m_ref.shape, MASK_VALUE, F32)
    acc_ref[...] = jnp.zeros(acc_ref.shape, F32)

    def scores(item, s_ref, cmax_ref):
        qi, c = item
        j0 = pl.multiple_of(c * tk, tk)
        s = jnp.dot(k_ref[pl.ds(j0, tk), :], qs_ref[qi], preferred_element_type=F32)
        s = s + bias_ref[(c == qi).astype(jnp.int32)]
        s_ref[...] = s
        cmax_ref[...] = jnp.max(s, axis=0, keepdims=True)

    def accumulate(item, s_ref, cmax_ref):
        qi, c = item
        shift = slope * (c * tk).astype(F32)
        vtc = vt_ref[c]
        for nb in range(2 * tq // ATT_LANES):
            cols = pl.ds(nb * ATT_LANES, ATT_LANES)
            m_prev = m_ref[qi, :, cols] - shift
            m_new = jnp.maximum(m_prev, cmax_ref[:, cols])
            alpha = jnp.exp2(m_prev - m_new)
            p = jnp.exp2(s_ref[:, cols] - m_new).astype(BF16)
            m_ref[qi, :, cols] = m_new + shift
            pv = jnp.dot(vtc, p, preferred_element_type=F32)
            acc_ref[qi, :, cols] = alpha * acc_ref[qi, :, cols] + pv

    def following(item):
        qi, c = item
        last = c == qi
        return qi + last.astype(jnp.int32), jnp.where(last, 0, c + 1)

    n_items = nt * (nt + 1) // 2
    n_bufs = len(s_refs)
    assert n_items % n_bufs == 0
    bufs = list(zip(s_refs, cmax_refs))

    def following_clamped(item):
        qi, c = following(item)
        past_end = qi == nt
        return jnp.where(past_end, 0, qi), jnp.where(past_end, 0, c)

    ahead = [(jnp.int32(0), jnp.int32(0))]
    for _ in range(n_bufs - 2):
        ahead.append(following_clamped(ahead[-1]))
    for item, buf in zip(ahead, bufs):
        scores(item, *buf)

    def trip(_, carry):
        items = list(carry)
        for slot in range(n_bufs):
            nxt = following_clamped(items[-1])
            scores(nxt, *bufs[(slot + n_bufs - 1) % n_bufs])
            accumulate(items[0], *bufs[slot])
            items = items[1:] + [nxt]
        return tuple(items)

    lax.fori_loop(0, n_items // n_bufs, trip, tuple(ahead))

    lam_p = lam_ref[...]
    lam = (jnp.exp(jnp.sum(lam_p[0:1] * lam_p[1:2], axis=-1, keepdims=True))
           - jnp.exp(jnp.sum(lam_p[2:3] * lam_p[3:4], axis=-1, keepdims=True)) + lam_init)
    for i in range(nt):
        acc = acc_ref[i, pl.ds(0, ATT_DV), :]
        inv_l = 1.0 / acc_ref[i, pl.ds(ATT_DV, 1), :]
        o = acc[:, :tq] * inv_l[:, :tq] - lam * (acc[:, tq:] * inv_l[:, tq:])
        inv_rms = lax.rsqrt(jnp.mean(o * o, axis=0, keepdims=True) + EPS)
        o_ref[i] = (o * inv_rms * subln_ref[...] * (1.0 - lam_init)).astype(BF16)


def _diff_attn(lam_p, subln, qt, k, vt, lam_init):
    b, t, _ = k.shape
    tq = ATT_TILE
    nt = t // tq
    per_head = lambda rows: pl.BlockSpec((None, nt, rows, tq), lambda i, h: (i, 0, h, 0))
    return pl.pallas_call(
        functools.partial(_diff_attn_kernel, lam_init=lam_init),
        grid=(b, ATT_HEADS),
        in_specs=[
            _const_spec(lam_p.shape),
            _const_spec(subln.shape),
            per_head(2 * ATT_DK),
            pl.BlockSpec((None, t, 2 * ATT_DK), lambda i, h: (i, 0, h)),
            per_head(VT_ROWS),
        ],
        out_specs=per_head(ATT_DV),
        out_shape=jax.ShapeDtypeStruct((b, nt, ATT_HEADS * ATT_DV, tq), BF16),
        scratch_shapes=[
            pltpu.VMEM((nt, 2 * ATT_DK, 2 * tq), BF16),
            pltpu.VMEM((2, tq, 2 * tq), F32),
            pltpu.VMEM((nt, 1, 2 * tq), F32),
            pltpu.VMEM((nt, VT_ROWS, 2 * tq), F32),
        ]
        + [pltpu.VMEM((tq, 2 * tq), F32)] * ATT_SCORE_BUFS
        + [pltpu.VMEM((1, 2 * tq), F32)] * ATT_SCORE_BUFS,
        compiler_params=_params(2),
        name="diff_attn",
    )(lam_p, subln, qt, k, vt)


def _mem_kv_kernel(mem_ref, g_ref, w_hbm, kv_ref, w_ref, w_stage_ref, w_sem, *, weight):
    @pl.when(pl.program_id(0) == 0)
    def _():
        weight.load(w_hbm, w_ref, w_stage_ref, w_sem)

    h = _rms(mem_ref[...], g_ref[...]).astype(BF16)
    kv_ref[...] = jnp.dot(h, w_ref[...], preferred_element_type=F32).astype(BF16)


def _mem_kv(mem, g, w, layer):
    b, m, d = mem.shape
    n = w.shape[2]
    weight = _ResidentWeight(w, layer, w.shape[1] // WEIGHT_CHUNKS)
    return pl.pallas_call(
        functools.partial(_mem_kv_kernel, weight=weight),
        grid=(b,),
        in_specs=[pl.BlockSpec((None, m, d), lambda i: (i, 0, 0)), _const_spec(g.shape), _HBM],
        out_specs=pl.BlockSpec((None, m, n), lambda i: (i, 0, 0)),
        out_shape=jax.ShapeDtypeStruct((b, m, n), BF16),
        scratch_shapes=weight.scratch(),
        compiler_params=_params(1),
        name="mem_kv",
    )(mem, g, w)


_TN = (((0,), (0,)), ((), ()))


def _mix_out_kernel(x_ref, oat_ref, oc_ref, wo_hbm, g_ref, wq_hbm, kv_ref, wxo_hbm, o_ref,
                    *scratch, weights):
    wo_ref, wq_ref, wxo_ref = scratch[0], scratch[3], scratch[6]

    @pl.when((pl.program_id(0) == 0) & (pl.program_id(1) == 0))
    def _():
        for w, hbm, i in zip(weights, (wo_hbm, wq_hbm, wxo_hbm), (0, 3, 6)):
            w.load(hbm, *scratch[i:i + 3])

    n_att = oat_ref.shape[0]
    x = x_ref[...]
    x = x + lax.dot_general(oat_ref[...], wo_ref[pl.ds(0, n_att), :], _TN,
                            preferred_element_type=F32)
    x = x + jnp.dot(oc_ref[...], wo_ref[pl.ds(n_att, oc_ref.shape[1]), :],
                    preferred_element_type=F32)

    hq = jnp.dot(_rms(x, g_ref[...]).astype(BF16), wq_ref[...], preferred_element_type=F32)
    hq = (hq * (XA_HD ** -0.5)).astype(BF16)
    kd = XA_HEADS * XA_HD
    heads = []
    for h in range(XA_HEADS):
        sl = slice(h * XA_HD, (h + 1) * XA_HD)
        s = lax.dot_general(hq[:, sl], kv_ref[:, sl], _NT, preferred_element_type=F32)
        p = jnp.exp(s - jnp.max(s, axis=-1, keepdims=True))
        inv = 1.0 / jnp.sum(p, axis=-1, keepdims=True)
        o = jnp.dot(p.astype(BF16), kv_ref[:, kd + h * XA_HD:kd + (h + 1) * XA_HD],
                    preferred_element_type=F32)
        heads.append((o * inv).astype(BF16))
    o = jnp.concatenate(heads, axis=-1)
    o_ref[...] = x + jnp.dot(o, wxo_ref[...], preferred_element_type=F32)


def _mix_out(x, oat, oc, wo, g, wq, kv, wxo, layer):
    b, t, d = x.shape
    tm = oat.shape[3]
    row = lambda cols: pl.BlockSpec((None, tm, cols), lambda i, j: (i, j, 0))
    weights = [_ResidentWeight(w, layer, w.shape[1] // WEIGHT_CHUNKS) for w in (wo, wq, wxo)]
    return pl.pallas_call(
        functools.partial(_mix_out_kernel, weights=weights),
        grid=(b, t // tm),
        in_specs=[row(d), pl.BlockSpec((None, None, oat.shape[2], tm), lambda i, j: (i, j, 0, 0)),
                  row(oc.shape[2]), _HBM, _const_spec(g.shape), _HBM,
                  pl.BlockSpec((None,) + kv.shape[1:], lambda i, j: (i, 0, 0)), _HBM],
        out_specs=row(d),
        out_shape=jax.ShapeDtypeStruct((b, t, d), F32),
        scratch_shapes=[s for w in weights for s in w.scratch()],
        compiler_params=_params(2),
        name="mix_out",
    )(x, oat, oc, wo, g, wq, kv, wxo)


def kernel(x, mem, ffn1_norm, ffn1_w_gate, ffn1_w_up, ffn1_w_down, mix_norm, w_in, lam_q1, lam_k1, lam_q2, lam_k2, diff_subln, cc_dw, cc_dw_b, cc_ln_g, cc_ln_b, sc_dw, w_out, xa_norm, mem_norm, xa_wq, xa_wkv, xa_wo, ffn2_norm, ffn2_w_gate, ffn2_w_up, ffn2_w_down, final_norm):
    b, t, d = x.shape
    vec = lambda v: v.reshape(1, -1).astype(F32)

    for l in range(DEPTH):
        x = _ffn(x.reshape(b * t, d), vec(ffn1_norm[l]), ffn1_w_gate, ffn1_w_up, ffn1_w_down,
                 l).reshape(b, t, d)

        qt, k, vt, oc = _mix_in(
            x, vec(mix_norm[l]),
            w_in[l, :, :QK_COLS].T.astype(BF16), w_in[l, :, V_COL:V_COL + V_COLS].T.astype(BF16),
            w_in, l,
            cc_dw[l].astype(F32), vec(cc_dw_b[l]), vec(cc_ln_g[l]), vec(cc_ln_b[l]),
            sc_dw[l].astype(F32))

        lam_init = 0.8 - 0.6 * math.exp(-0.3 * l)
        lam_p = jnp.stack([lam_q1[l], lam_k1[l], lam_q2[l], lam_k2[l]]).astype(F32)
        subln = jnp.broadcast_to(diff_subln[l].astype(F32)[:, None], (ATT_DV, ATT_TILE))
        oat = _diff_attn(lam_p, subln, qt, k, vt, lam_init)

        kv = _mem_kv(mem, vec(mem_norm[l]), xa_wkv, l)
        x = _mix_out(x, oat, oc, w_out, vec(xa_norm[l]), xa_wq, kv, xa_wo, l)

        last = l == DEPTH - 1
        x = _ffn(x.reshape(b * t, d), vec(ffn2_norm[l]), ffn2_w_gate, ffn2_w_up, ffn2_w_down, l,
                 vec(final_norm) if last else None).reshape(b, t, d)
    return x
```

```python
import functools
import math

import jax
import jax.numpy as jnp
from jax import lax
from jax.experimental import pallas as pl
from jax.experimental.pallas import tpu as pltpu

D_MODEL = 1024
DEPTH = 2
ATT_HEADS = 4
ATT_DV = 128
ATT_DK = 64
CC_CH = 256
CC_K = 31
SC_CH = 256
SC_K = 3
D_FF = 2816
MEM_LEN = 256
XA_HEADS = 4
XA_HD = 256
EPS = 1e-6

F32 = jnp.float32
BF16 = jnp.bfloat16

QK_COLS = ATT_HEADS * 2 * ATT_DK
V_COLS = ATT_HEADS * ATT_DV
K_COL = QK_COLS
V_COL = 2 * QK_COLS
CC_COL = V_COL + V_COLS
SC_COL = CC_COL + 2 * CC_CH

TOKEN_TILE = 512
ATT_TILE = 512
VT_ROWS = ATT_DV + 16
LOG2E = math.log2(math.e)
ATT_SCORE_BUFS = 4
ATT_LANES = 256
WEIGHT_CHUNK_BYTES = 3 * 512 * 1024
WEIGHT_SLOTS = 4
LANES = 128
SUBLANES = 8
CONV_ROWS = 64
CC_HALO = 32
SC_HALO = 8
VMEM_LIMIT = 56 * 1024 * 1024
MASK_VALUE = -1e30

_NT = (((1,), (1,)), ((), ()))


def _rms(x, g):
    return x * lax.rsqrt(jnp.mean(x * x, axis=-1, keepdims=True) + EPS) * g


def _const_spec(shape):
    nd = len(shape)
    return pl.BlockSpec(shape, lambda *_: (0,) * nd, pipeline_mode=pl.Buffered(1))


_HBM = pl.BlockSpec(memory_space=pl.ANY)


class _ResidentWeight:
    def __init__(self, stacked, layer):
        _, self.rows, self.cols = stacked.shape
        self.layer = layer
        self.chunk_rows = max(r for r in range(2 * SUBLANES, self.rows + 1, 2 * SUBLANES)
                              if self.rows % r == 0 and r * self.cols * 4 <= WEIGHT_CHUNK_BYTES)

    def scratch(self):
        return [pltpu.VMEM((self.rows, self.cols), BF16),
                pltpu.VMEM((WEIGHT_SLOTS, self.chunk_rows, self.cols), F32),
                pltpu.SemaphoreType.DMA((WEIGHT_SLOTS,))]

    def load(self, hbm_ref, resident_ref, stage_ref, sem_ref, on_chunk=None):
        rc = self.chunk_rows
        n_chunks = self.rows // rc

        def copy(c):
            slot = c % WEIGHT_SLOTS
            return pltpu.make_async_copy(hbm_ref.at[self.layer, pl.ds(c * rc, rc), :],
                                         stage_ref.at[slot], sem_ref.at[slot])

        for c in range(min(WEIGHT_SLOTS - 1, n_chunks)):
            copy(c).start()
        for c in range(n_chunks):
            if c + WEIGHT_SLOTS - 1 < n_chunks:
                copy(c + WEIGHT_SLOTS - 1).start()
            copy(c).wait()
            chunk = stage_ref[c % WEIGHT_SLOTS]
            resident_ref[pl.ds(c * rc, rc), :] = chunk.astype(BF16)
            if on_chunk is not None:
                on_chunk(c * rc, chunk)


def _params(n_axes):
    return pltpu.CompilerParams(
        dimension_semantics=("arbitrary",) * n_axes, vmem_limit_bytes=VMEM_LIMIT)


def _fold_bits(v):
    bits = pltpu.bitcast(v, jnp.uint32)
    folded = None
    for i in range(0, bits.shape[0], SUBLANES):
        for c in range(0, bits.shape[1], LANES):
            tile = bits[i:i + SUBLANES, c:c + LANES]
            folded = tile if folded is None else folded | tile
    return folded


def _order_after(dst_ref, bits):
    half = jnp.uint32(16)
    zero = pltpu.bitcast(lax.shift_right_logical(lax.shift_right_logical(bits, half), half), F32)
    zero = jnp.concatenate([zero, zero], axis=0).astype(dst_ref.dtype)
    dst_ref[...] = dst_ref[...] + zero


def _ffn_kernel(x_ref, g_ref, wg_hbm, wu_hbm, wd_hbm, *rest, final, weights):
    if final:
        fg_ref, o_ref, *scratch = rest
    else:
        o_ref, *scratch = rest
    wg_ref, wu_ref, wd_ref = scratch[0], scratch[3], scratch[6]

    @pl.when(pl.program_id(0) == 0)
    def _():
        for w, hbm, i in zip(weights, (wg_hbm, wu_hbm, wd_hbm), (0, 3, 6)):
            w.load(hbm, *scratch[i:i + 3])

    x = x_ref[...]
    h = _rms(x, g_ref[...]).astype(BF16)
    gate = jnp.dot(h, wg_ref[...], preferred_element_type=F32)
    up = jnp.dot(h, wu_ref[...], preferred_element_type=F32)
    act = (gate * jax.nn.sigmoid(gate) * up).astype(BF16)
    y = x + 0.5 * jnp.dot(act, wd_ref[...], preferred_element_type=F32)
    if final:
        y = _rms(y, fg_ref[...])
    o_ref[...] = y


def _ffn(x, g, wg, wu, wd, layer, final_g=None):
    n, d = x.shape
    tm = TOKEN_TILE
    final = final_g is not None
    row_spec = pl.BlockSpec((tm, d), lambda i: (i, 0))
    weights = [_ResidentWeight(w, layer) for w in (wg, wu, wd)]
    in_specs = [row_spec, _const_spec((1, d)), _HBM, _HBM, _HBM]
    args = [x, g, wg, wu, wd]
    if final:
        in_specs.append(_const_spec((1, d)))
        args.append(final_g)
    return pl.pallas_call(
        functools.partial(_ffn_kernel, final=final, weights=weights),
        grid=(n // tm,),
        in_specs=in_specs,
        out_specs=row_spec,
        out_shape=jax.ShapeDtypeStruct((n, d), F32),
        scratch_shapes=[s for w in weights for s in w.scratch()],
        compiler_params=_params(1),
        name="ffn_final" if final else "ffn",
    )(*args)


def _shifted_copies(ext_ref, shift_ref):
    rows = shift_ref.shape[1]
    for b in range(1, SUBLANES):
        shift_ref[b - 1] = ext_ref[pl.ds(b, rows), :]


def _dw_conv(ext_ref, w_ref, out_ref, *, taps, halo, blocks, shift_ref=None):
    base = halo - (taps - 1)
    folded = None
    for r in blocks:
        acc = None
        for j in range(taps):
            off = base + j
            if shift_ref is None or off % SUBLANES == 0:
                rows = ext_ref[pl.ds(off + r * CONV_ROWS, CONV_ROWS), :]
            else:
                rows = shift_ref[off % SUBLANES - 1,
                                 pl.ds(off - off % SUBLANES + r * CONV_ROWS, CONV_ROWS), :]
            term = rows * w_ref[pl.ds(j, 1), :]
            acc = term if acc is None else acc + term
        out_ref[pl.ds(r * CONV_ROWS, CONV_ROWS), :] = acc
        folded = _fold_bits(acc) if folded is None else folded | _fold_bits(acc)
    return folded


CC_BLOCK_SPLIT = ((0,), (1,), (2, 3), (4, 5), (6, 7))


def _mix_in_kernel(x_ref, g_ref, win_hbm,
                   ccw_ref, ccb_ref, lng_ref, lnb_ref, scw_ref,
                   qt_ref, k_ref, vt_ref, oc_ref,
                   h_ref, ccx_ref, scx_ref, gate_ref, ccn_ref, scn_ref, gaten_ref, ccs_ref,
                   ccy_ref, scy_ref, wqt_ref, wvt_ref, win_ref, win_stage_ref, win_sem,
                   *, tiles_per_seq, w_in):
    tm = x_ref.shape[0]
    g = pl.program_id(0)
    cc_conv = functools.partial(_dw_conv, ccx_ref, ccw_ref, ccy_ref, taps=CC_K, halo=CC_HALO,
                                shift_ref=ccs_ref)

    def cc_conv_then_next_projection(i):
        done = cc_conv(blocks=CC_BLOCK_SPLIT[i])
        _order_after(h_ref.at[pl.ds(0, 16), pl.ds(0, 128)], done)

    @pl.when(g == 0)
    def _():
        ccx_ref[...] = jnp.zeros(ccx_ref.shape, F32)
        scx_ref[...] = jnp.zeros(scx_ref.shape, F32)
        gate_ref[...] = jnp.zeros(gate_ref.shape, F32)

        def transposed_parts(row0, chunk):
            rows = pl.ds(row0, chunk.shape[0])
            wqt_ref[:, rows] = chunk[:, :QK_COLS].T.astype(BF16)
            wvt_ref[:, rows] = chunk[:, V_COL:V_COL + V_COLS].T.astype(BF16)

        w_in.load(win_hbm, win_ref, win_stage_ref, win_sem, on_chunk=transposed_parts)

    h_ref[...] = _rms(x_ref[...], g_ref[...]).astype(BF16)
    _shifted_copies(ccx_ref, ccs_ref)

    qt = lax.dot_general(wqt_ref[...], h_ref[...], _NT, preferred_element_type=F32)
    qt_ref[...] = (qt * (LOG2E * ATT_DK ** -0.5)).astype(BF16)
    cc_conv_then_next_projection(0)

    k_ref[...] = jnp.dot(h_ref[...], win_ref[:, pl.ds(K_COL, QK_COLS)],
                         preferred_element_type=F32).astype(BF16)
    cc_conv_then_next_projection(1)

    vt = lax.dot_general(wvt_ref[...], h_ref[...], _NT, preferred_element_type=F32).astype(BF16)
    for hd in range(ATT_HEADS):
        vt_ref[pl.ds(hd * VT_ROWS, ATT_DV), :] = vt[hd * ATT_DV:(hd + 1) * ATT_DV]
        vt_ref[pl.ds(hd * VT_ROWS + ATT_DV, VT_ROWS - ATT_DV), :] = jnp.ones(
            (VT_ROWS - ATT_DV, tm), BF16)
    cc_conv_then_next_projection(2)

    zc = jnp.dot(h_ref[...], win_ref[:, pl.ds(CC_COL, 2 * CC_CH)],
                 preferred_element_type=F32)
    ccn_ref[...] = zc[:, :CC_CH] * jax.nn.sigmoid(zc[:, CC_CH:])
    cc_conv_then_next_projection(3)

    zs = jnp.dot(h_ref[...], win_ref[:, pl.ds(SC_COL, 3 * SC_CH)],
                 preferred_element_type=F32)
    scn_ref[...] = zs[:, SC_CH:2 * SC_CH] * zs[:, 2 * SC_CH:]
    gaten_ref[...] = zs[:, :SC_CH]
    cc_conv(blocks=CC_BLOCK_SPLIT[4])
    _dw_conv(scx_ref, scw_ref, scy_ref, taps=SC_K, halo=SC_HALO, blocks=range(tm // CONV_ROWS))
    oc_ref[:, CC_CH:] = (gate_ref[...] * scy_ref[...]).astype(BF16)

    u = ccy_ref[...] + ccb_ref[...]
    mu = jnp.mean(u, axis=-1, keepdims=True)
    var = jnp.mean(jnp.square(u - mu), axis=-1, keepdims=True)
    y = (u - mu) * lax.rsqrt(var + EPS) * lng_ref[...] + lnb_ref[...]
    oc_ref[:, :CC_CH] = (y * jax.nn.sigmoid(y)).astype(BF16)

    seq_start = lax.rem(g, tiles_per_seq) == 0
    ccx_ref[pl.ds(0, CC_HALO), :] = jnp.where(seq_start, 0.0, ccx_ref[pl.ds(tm, CC_HALO), :])
    ccx_ref[pl.ds(CC_HALO, tm), :] = ccn_ref[...]
    scx_ref[pl.ds(0, SC_HALO), :] = jnp.where(seq_start, 0.0, scx_ref[pl.ds(tm, SC_HALO), :])
    scx_ref[pl.ds(SC_HALO, tm), :] = scn_ref[...]
    gate_ref[...] = gaten_ref[...]


def _mix_in(x, g, w_in, layer, ccw, ccb, lng, lnb, scw):
    b, t, d = x.shape
    tm = ATT_TILE
    nt = t // tm
    n_tiles = b * nt
    qk_cols = QK_COLS
    v_cols = ATT_HEADS * VT_ROWS
    proj = lambda i: jnp.minimum(i, n_tiles - 1)
    conv = lambda i: jnp.maximum(i - 1, 0)
    row = lambda cols, dt, tile: (
        pl.BlockSpec((None, tm, cols), lambda i: (tile(i) // nt, tile(i) % nt, 0)),
        jax.ShapeDtypeStruct((b, t, cols), dt))
    k_spec, k_shape = row(qk_cols, BF16, proj)
    oc_spec, oc_shape = row(CC_CH + SC_CH, BF16, conv)
    slab = lambda rows: (
        pl.BlockSpec((None, None, rows, tm), lambda i: (proj(i) // nt, proj(i) % nt, 0, 0)),
        jax.ShapeDtypeStruct((b, nt, rows, tm), BF16))
    qt_spec, qt_shape = slab(qk_cols)
    vt_spec, vt_shape = slab(v_cols)
    consts = [ccw, ccb, lng, lnb, scw]
    win = _ResidentWeight(w_in, layer)
    assert win.chunk_rows % LANES == 0
    return pl.pallas_call(
        functools.partial(_mix_in_kernel, tiles_per_seq=nt, w_in=win),
        grid=(n_tiles + 1,),
        in_specs=[pl.BlockSpec((None, tm, d), lambda i: (proj(i) // nt, proj(i) % nt, 0)),
                  _const_spec(g.shape), _HBM]
        + [_const_spec(c.shape) for c in consts],
        out_specs=[qt_spec, k_spec, vt_spec, oc_spec],
        out_shape=[qt_shape, k_shape, vt_shape, oc_shape],
        scratch_shapes=[
            pltpu.VMEM((tm, d), BF16),
            pltpu.VMEM((CC_HALO + tm, CC_CH), F32),
            pltpu.VMEM((SC_HALO + tm, SC_CH), F32),
            pltpu.VMEM((tm, SC_CH), F32),
            pltpu.VMEM((tm, CC_CH), F32),
            pltpu.VMEM((tm, SC_CH), F32),
            pltpu.VMEM((tm, SC_CH), F32),
            pltpu.VMEM((SUBLANES - 1, CC_HALO + tm - SUBLANES, CC_CH), F32),
            pltpu.VMEM((tm, CC_CH), F32),
            pltpu.VMEM((tm, SC_CH), F32),
            pltpu.VMEM((QK_COLS, d), BF16),
            pltpu.VMEM((V_COLS, d), BF16),
        ] + win.scratch(),
        compiler_params=_params(1),
        name="mix_in",
    )(x, g, w_in, *consts)


def _diff_attn_kernel(lam_ref, subln_ref, qt_ref, k_ref, vt_ref, o_ref,
                      qs_ref, bias_ref, m_ref, acc_ref, *score_refs, lam_init):
    s_refs, cmax_refs = score_refs[:ATT_SCORE_BUFS], score_refs[ATT_SCORE_BUFS:]
    nt, _, tq = qt_ref.shape
    tk = tq
    head = pl.program_id(1)
    slope = jnp.left_shift(1, 2 * (ATT_HEADS - 1 - head)).astype(F32) * (LOG2E / 256.0)

    jj = lax.broadcasted_iota(jnp.int32, (tk, 2 * tq), 0)
    ii = lax.broadcasted_iota(jnp.int32, (tk, 2 * tq), 1)
    ii = jnp.where(ii >= tq, ii - tq, ii)
    bias = jj.astype(F32) * slope
    bias_ref[0] = bias
    bias_ref[1] = jnp.where(jj <= ii, bias, MASK_VALUE)

    for i in range(nt):
        qt = qt_ref[i]
        row = lax.broadcasted_iota(jnp.int32, qt.shape, 0)
        zero = jnp.zeros_like(qt)
        qs_ref[i, :, pl.ds(0, tq)] = jnp.where(row < ATT_DK, qt, zero)
        qs_ref[i, :, pl.ds(tq, tq)] = jnp.where(row >= ATT_DK, qt, zero)

    m_ref[...] = jnp.full(m_ref.shape, MASK_VALUE, F32)
    acc_ref[...] = jnp.zeros(acc_ref.shape, F32)

    def scores(item, s_ref, cmax_ref):
        qi, c = item
        j0 = pl.multiple_of(c * tk, tk)
        s = jnp.dot(k_ref[pl.ds(j0, tk), :], qs_ref[qi], preferred_element_type=F32)
        s = s + bias_ref[(c == qi).astype(jnp.int32)]
        s_ref[...] = s
        cmax_ref[...] = jnp.max(s, axis=0, keepdims=True)

    def accumulate(item, s_ref, cmax_ref):
        qi, c = item
        shift = slope * (c * tk).astype(F32)
        vtc = vt_ref[c]
        for nb in range(2 * tq // ATT_LANES):
            cols = pl.ds(nb * ATT_LANES, ATT_LANES)
            m_prev = m_ref[qi, :, cols] - shift
            m_new = jnp.maximum(m_prev, cmax_ref[:, cols])
            alpha = jnp.exp2(m_prev - m_new)
            p = jnp.exp2(s_ref[:, cols] - m_new).astype(BF16)
            m_ref[qi, :, cols] = m_new + shift
            pv = jnp.dot(vtc, p, preferred_element_type=F32)
            acc_ref[qi, :, cols] = alpha * acc_ref[qi, :, cols] + pv

    def following(item):
        qi, c = item
        last = c == qi
        return qi + last.astype(jnp.int32), jnp.where(last, 0, c + 1)

    n_items = nt * (nt + 1) // 2
    n_bufs = len(s_refs)
    assert n_items % n_bufs == 0
    bufs = list(zip(s_refs, cmax_refs))

    def following_clamped(item):
        qi, c = following(item)
        past_end = qi == nt
        return jnp.where(past_end, 0, qi), jnp.where(past_end, 0, c)

    ahead = [(jnp.int32(0), jnp.int32(0))]
    for _ in range(n_bufs - 2):
        ahead.append(following_clamped(ahead[-1]))
    for item, buf in zip(ahead, bufs):
        scores(item, *buf)

    def trip(_, carry):
        items = list(carry)
        for slot in range(n_bufs):
            nxt = following_clamped(items[-1])
            scores(nxt, *bufs[(slot + n_bufs - 1) % n_bufs])
            accumulate(items[0], *bufs[slot])
            items = items[1:] + [nxt]
        return tuple(items)

    lax.fori_loop(0, n_items // n_bufs, trip, tuple(ahead))

    lam_p = lam_ref[...]
    lam = (jnp.exp(jnp.sum(lam_p[0:1] * lam_p[1:2], axis=-1, keepdims=True))
           - jnp.exp(jnp.sum(lam_p[2:3] * lam_p[3:4], axis=-1, keepdims=True)) + lam_init)
    for i in range(nt):
        acc = acc_ref[i, pl.ds(0, ATT_DV), :]
        inv_l = 1.0 / acc_ref[i, pl.ds(ATT_DV, 1), :]
        o = acc[:, :tq] * inv_l[:, :tq] - lam * (acc[:, tq:] * inv_l[:, tq:])
        inv_rms = lax.rsqrt(jnp.mean(o * o, axis=0, keepdims=True) + EPS)
        o_ref[i] = (o * inv_rms * subln_ref[...] * (1.0 - lam_init)).astype(BF16)


def _diff_attn(lam_p, subln, qt, k, vt, lam_init):
    b, t, _ = k.shape
    tq = ATT_TILE
    nt = t // tq
    per_head = lambda rows: pl.BlockSpec((None, nt, rows, tq), lambda i, h: (i, 0, h, 0))
    return pl.pallas_call(
        functools.partial(_diff_attn_kernel, lam_init=lam_init),
        grid=(b, ATT_HEADS),
        in_specs=[
            _const_spec(lam_p.shape),
            _const_spec(subln.shape),
            per_head(2 * ATT_DK),
            pl.BlockSpec((None, t, 2 * ATT_DK), lambda i, h: (i, 0, h)),
            per_head(VT_ROWS),
        ],
        out_specs=per_head(ATT_DV),
        out_shape=jax.ShapeDtypeStruct((b, nt, ATT_HEADS * ATT_DV, tq), BF16),
        scratch_shapes=[
            pltpu.VMEM((nt, 2 * ATT_DK, 2 * tq), BF16),
            pltpu.VMEM((2, tq, 2 * tq), F32),
            pltpu.VMEM((nt, 1, 2 * tq), F32),
            pltpu.VMEM((nt, VT_ROWS, 2 * tq), F32),
        ]
        + [pltpu.VMEM((tq, 2 * tq), F32)] * ATT_SCORE_BUFS
        + [pltpu.VMEM((1, 2 * tq), F32)] * ATT_SCORE_BUFS,
        compiler_params=_params(2),
        name="diff_attn",
    )(lam_p, subln, qt, k, vt)


def _mem_kv_kernel(mem_ref, g_ref, w_hbm, kv_ref, w_ref, w_stage_ref, w_sem, *, weight):
    @pl.when(pl.program_id(0) == 0)
    def _():
        weight.load(w_hbm, w_ref, w_stage_ref, w_sem)

    h = _rms(mem_ref[...], g_ref[...]).astype(BF16)
    kv_ref[...] = jnp.dot(h, w_ref[...], preferred_element_type=F32).astype(BF16)


def _mem_kv(mem, g, w, layer):
    b, m, d = mem.shape
    n = w.shape[2]
    weight = _ResidentWeight(w, layer)
    return pl.pallas_call(
        functools.partial(_mem_kv_kernel, weight=weight),
        grid=(b,),
        in_specs=[pl.BlockSpec((None, m, d), lambda i: (i, 0, 0)), _const_spec(g.shape), _HBM],
        out_specs=pl.BlockSpec((None, m, n), lambda i: (i, 0, 0)),
        out_shape=jax.ShapeDtypeStruct((b, m, n), BF16),
        scratch_shapes=weight.scratch(),
        compiler_params=_params(1),
        name="mem_kv",
    )(mem, g, w)


_TN = (((0,), (0,)), ((), ()))


def _mix_out_kernel(x_ref, oat_ref, oc_ref, wo_hbm, g_ref, wq_hbm, kv_ref, wxo_hbm, o_ref,
                    *scratch, weights):
    wo_ref, wq_ref, wxo_ref = scratch[0], scratch[3], scratch[6]

    @pl.when((pl.program_id(0) == 0) & (pl.program_id(1) == 0))
    def _():
        for w, hbm, i in zip(weights, (wo_hbm, wq_hbm, wxo_hbm), (0, 3, 6)):
            w.load(hbm, *scratch[i:i + 3])

    n_att = oat_ref.shape[0]
    x = x_ref[...]
    x = x + lax.dot_general(oat_ref[...], wo_ref[pl.ds(0, n_att), :], _TN,
                            preferred_element_type=F32)
    x = x + jnp.dot(oc_ref[...], wo_ref[pl.ds(n_att, oc_ref.shape[1]), :],
                    preferred_element_type=F32)

    hq = jnp.dot(_rms(x, g_ref[...]).astype(BF16), wq_ref[...], preferred_element_type=F32)
    hq = (hq * (XA_HD ** -0.5)).astype(BF16)
    kd = XA_HEADS * XA_HD
    heads = []
    for h in range(XA_HEADS):
        sl = slice(h * XA_HD, (h + 1) * XA_HD)
        s = lax.dot_general(hq[:, sl], kv_ref[:, sl], _NT, preferred_element_type=F32)
        p = jnp.exp(s - jnp.max(s, axis=-1, keepdims=True))
        inv = 1.0 / jnp.sum(p, axis=-1, keepdims=True)
        o = jnp.dot(p.astype(BF16), kv_ref[:, kd + h * XA_HD:kd + (h + 1) * XA_HD],
                    preferred_element_type=F32)
        heads.append((o * inv).astype(BF16))
    o = jnp.concatenate(heads, axis=-1)
    o_ref[...] = x + jnp.dot(o, wxo_ref[...], preferred_element_type=F32)


def _mix_out(x, oat, oc, wo, g, wq, kv, wxo, layer):
    b, t, d = x.shape
    tm = oat.shape[3]
    row = lambda cols: pl.BlockSpec((None, tm, cols), lambda i, j: (i, j, 0))
    weights = [_ResidentWeight(w, layer) for w in (wo, wq, wxo)]
    return pl.pallas_call(
        functools.partial(_mix_out_kernel, weights=weights),
        grid=(b, t // tm),
        in_specs=[row(d), pl.BlockSpec((None, None, oat.shape[2], tm), lambda i, j: (i, j, 0, 0)),
                  row(oc.shape[2]), _HBM, _const_spec(g.shape), _HBM,
                  pl.BlockSpec((None,) + kv.shape[1:], lambda i, j: (i, 0, 0)), _HBM],
        out_specs=row(d),
        out_shape=jax.ShapeDtypeStruct((b, t, d), F32),
        scratch_shapes=[s for w in weights for s in w.scratch()],
        compiler_params=_params(2),
        name="mix_out",
    )(x, oat, oc, wo, g, wq, kv, wxo)


def kernel(x, mem, ffn1_norm, ffn1_w_gate, ffn1_w_up, ffn1_w_down, mix_norm, w_in, lam_q1, lam_k1, lam_q2, lam_k2, diff_subln, cc_dw, cc_dw_b, cc_ln_g, cc_ln_b, sc_dw, w_out, xa_norm, mem_norm, xa_wq, xa_wkv, xa_wo, ffn2_norm, ffn2_w_gate, ffn2_w_up, ffn2_w_down, final_norm):
    b, t, d = x.shape
    vec = lambda v: v.reshape(1, -1).astype(F32)

    for l in range(DEPTH):
        x = _ffn(x.reshape(b * t, d), vec(ffn1_norm[l]), ffn1_w_gate, ffn1_w_up, ffn1_w_down,
                 l).reshape(b, t, d)

        qt, k, vt, oc = _mix_in(
            x, vec(mix_norm[l]), w_in, l,
            cc_dw[l].astype(F32), vec(cc_dw_b[l]), vec(cc_ln_g[l]), vec(cc_ln_b[l]),
            sc_dw[l].astype(F32))

        lam_init = 0.8 - 0.6 * math.exp(-0.3 * l)
        lam_p = jnp.stack([lam_q1[l], lam_k1[l], lam_q2[l], lam_k2[l]]).astype(F32)
        subln = jnp.broadcast_to(diff_subln[l].astype(F32)[:, None], (ATT_DV, ATT_TILE))
        oat = _diff_attn(lam_p, subln, qt, k, vt, lam_init)

        kv = _mem_kv(mem, vec(mem_norm[l]), xa_wkv, l)
        x = _mix_out(x, oat, oc, w_out, vec(xa_norm[l]), xa_wq, kv, xa_wo, l)

        last = l == DEPTH - 1
        x = _ffn(x.reshape(b * t, d), vec(ffn2_norm[l]), ffn2_w_gate, ffn2_w_up, ffn2_w_down, l,
                 vec(final_norm) if last else None).reshape(b, t, d)
    return x
```

```python
import functools
import math

import jax
import jax.numpy as jnp
from jax import lax
from jax.experimental import pallas as pl
from jax.experimental.pallas import tpu as pltpu

D_MODEL = 1024
DEPTH = 2
ATT_HEADS = 4
ATT_DV = 128
ATT_DK = 64
CC_CH = 256
CC_K = 31
SC_CH = 256
SC_K = 3
D_FF = 2816
MEM_LEN = 256
XA_HEADS = 4
XA_HD = 256
EPS = 1e-6

F32 = jnp.float32
BF16 = jnp.bfloat16

QK_COLS = ATT_HEADS * 2 * ATT_DK
V_COLS = ATT_HEADS * ATT_DV
K_COL = QK_COLS
V_COL = 2 * QK_COLS
CC_COL = V_COL + V_COLS
SC_COL = CC_COL + 2 * CC_CH

TOKEN_TILE = 512
MIX_OUT_TILE = 512
ATT_TILE = 512
VT_ROWS = ATT_DV + 16
LOG2E = math.log2(math.e)
ATT_SCORE_BUFS = 4
ATT_LANES = 256
WEIGHT_CHUNK_BYTES = 3 * 512 * 1024
WEIGHT_SLOTS = 4
LANES = 128
SUBLANES = 8
CONV_ROWS = 64
CC_HALO = 32
SC_HALO = 8
VMEM_LIMIT = 56 * 1024 * 1024
MASK_VALUE = -1e30

_NT = (((1,), (1,)), ((), ()))


def _rms(x, g):
    return x * lax.rsqrt(jnp.mean(x * x, axis=-1, keepdims=True) + EPS) * g


def _const_spec(shape):
    nd = len(shape)
    return pl.BlockSpec(shape, lambda *_: (0,) * nd, pipeline_mode=pl.Buffered(1))


_HBM = pl.BlockSpec(memory_space=pl.ANY)


class _ResidentWeight:
    def __init__(self, stacked, layer):
        _, self.rows, self.cols = stacked.shape
        self.layer = layer
        self.chunk_rows = max(r for r in range(2 * SUBLANES, self.rows + 1, 2 * SUBLANES)
                              if self.rows % r == 0 and r * self.cols * 4 <= WEIGHT_CHUNK_BYTES)

    def scratch(self):
        return [pltpu.VMEM((self.rows, self.cols), BF16),
                pltpu.VMEM((WEIGHT_SLOTS, self.chunk_rows, self.cols), F32),
                pltpu.SemaphoreType.DMA((WEIGHT_SLOTS,))]

    def load(self, hbm_ref, resident_ref, stage_ref, sem_ref, on_chunk=None):
        rc = self.chunk_rows
        n_chunks = self.rows // rc

        def copy(c):
            slot = c % WEIGHT_SLOTS
            return pltpu.make_async_copy(hbm_ref.at[self.layer, pl.ds(c * rc, rc), :],
                                         stage_ref.at[slot], sem_ref.at[slot])

        for c in range(min(WEIGHT_SLOTS - 1, n_chunks)):
            copy(c).start()
        for c in range(n_chunks):
            if c + WEIGHT_SLOTS - 1 < n_chunks:
                copy(c + WEIGHT_SLOTS - 1).start()
            copy(c).wait()
            chunk = stage_ref[c % WEIGHT_SLOTS]
            resident_ref[pl.ds(c * rc, rc), :] = chunk.astype(BF16)
            if on_chunk is not None:
                on_chunk(c * rc, chunk)


def _params(n_axes):
    return pltpu.CompilerParams(
        dimension_semantics=("arbitrary",) * n_axes, vmem_limit_bytes=VMEM_LIMIT)


def _fold_bits(v):
    bits = pltpu.bitcast(v, jnp.uint32)
    folded = None
    for i in range(0, bits.shape[0], SUBLANES):
        for c in range(0, bits.shape[1], LANES):
            tile = bits[i:i + SUBLANES, c:c + LANES]
            folded = tile if folded is None else folded | tile
    return folded


def _order_after(dst_ref, bits):
    half = jnp.uint32(16)
    zero = pltpu.bitcast(lax.shift_right_logical(lax.shift_right_logical(bits, half), half), F32)
    zero = jnp.concatenate([zero, zero], axis=0).astype(dst_ref.dtype)
    dst_ref[...] = dst_ref[...] + zero


def _ffn_kernel(x_ref, g_ref, wg_hbm, wu_hbm, wd_hbm, *rest, final, weights):
    if final:
        fg_ref, o_ref, *scratch = rest
    else:
        o_ref, *scratch = rest
    wg_ref, wu_ref, wd_ref = scratch[0], scratch[3], scratch[6]

    @pl.when(pl.program_id(0) == 0)
    def _():
        for w, hbm, i in zip(weights, (wg_hbm, wu_hbm, wd_hbm), (0, 3, 6)):
            w.load(hbm, *scratch[i:i + 3])

    x = x_ref[...]
    h = _rms(x, g_ref[...]).astype(BF16)
    gate = jnp.dot(h, wg_ref[...], preferred_element_type=F32)
    up = jnp.dot(h, wu_ref[...], preferred_element_type=F32)
    act = (gate * jax.nn.sigmoid(gate) * up).astype(BF16)
    y = x + 0.5 * jnp.dot(act, wd_ref[...], preferred_element_type=F32)
    if final:
        y = _rms(y, fg_ref[...])
    o_ref[...] = y


def _ffn(x, g, wg, wu, wd, layer, final_g=None):
    n, d = x.shape
    tm = TOKEN_TILE
    final = final_g is not None
    row_spec = pl.BlockSpec((tm, d), lambda i: (i, 0))
    weights = [_ResidentWeight(w, layer) for w in (wg, wu, wd)]
    in_specs = [row_spec, _const_spec((1, d)), _HBM, _HBM, _HBM]
    args = [x, g, wg, wu, wd]
    if final:
        in_specs.append(_const_spec((1, d)))
        args.append(final_g)
    return pl.pallas_call(
        functools.partial(_ffn_kernel, final=final, weights=weights),
        grid=(n // tm,),
        in_specs=in_specs,
        out_specs=row_spec,
        out_shape=jax.ShapeDtypeStruct((n, d), F32),
        scratch_shapes=[s for w in weights for s in w.scratch()],
        compiler_params=_params(1),
        name="ffn_final" if final else "ffn",
    )(*args)


def _shifted_copies(ext_ref, shift_ref):
    rows = shift_ref.shape[1]
    for b in range(1, SUBLANES):
        shift_ref[b - 1] = ext_ref[pl.ds(b, rows), :]


def _dw_conv(ext_ref, w_ref, out_ref, *, taps, halo, blocks, shift_ref=None):
    base = halo - (taps - 1)
    folded = None
    for r in blocks:
        acc = None
        for j in range(taps):
            off = base + j
            if shift_ref is None or off % SUBLANES == 0:
                rows = ext_ref[pl.ds(off + r * CONV_ROWS, CONV_ROWS), :]
            else:
                rows = shift_ref[off % SUBLANES - 1,
                                 pl.ds(off - off % SUBLANES + r * CONV_ROWS, CONV_ROWS), :]
            term = rows * w_ref[pl.ds(j, 1), :]
            acc = term if acc is None else acc + term
        out_ref[pl.ds(r * CONV_ROWS, CONV_ROWS), :] = acc
        folded = _fold_bits(acc) if folded is None else folded | _fold_bits(acc)
    return folded


CC_BLOCK_SPLIT = ((0,), (1,), (2, 3), (4, 5), (6, 7))


def _mix_in_kernel(x_ref, g_ref, win_hbm,
                   ccw_ref, ccb_ref, lng_ref, lnb_ref, scw_ref,
                   qt_ref, k_ref, vt_ref, oc_ref,
                   h_ref, ccx_ref, scx_ref, gate_ref, ccn_ref, scn_ref, gaten_ref, ccs_ref,
                   ccy_ref, scy_ref, wqt_ref, wvt_ref, win_ref, win_stage_ref, win_sem,
                   *, tiles_per_seq, w_in):
    tm = x_ref.shape[0]
    g = pl.program_id(0)
    cc_conv = functools.partial(_dw_conv, ccx_ref, ccw_ref, ccy_ref, taps=CC_K, halo=CC_HALO,
                                shift_ref=ccs_ref)

    def cc_conv_then_next_projection(i):
        done = cc_conv(blocks=CC_BLOCK_SPLIT[i])
        _order_after(h_ref.at[pl.ds(0, 16), pl.ds(0, 128)], done)

    @pl.when(g == 0)
    def _():
        ccx_ref[...] = jnp.zeros(ccx_ref.shape, F32)
        scx_ref[...] = jnp.zeros(scx_ref.shape, F32)
        gate_ref[...] = jnp.zeros(gate_ref.shape, F32)

        def transposed_parts(row0, chunk):
            rows = pl.ds(row0, chunk.shape[0])
            wqt_ref[:, rows] = chunk[:, :QK_COLS].T.astype(BF16)
            wvt_ref[:, rows] = chunk[:, V_COL:V_COL + V_COLS].T.astype(BF16)

        w_in.load(win_hbm, win_ref, win_stage_ref, win_sem, on_chunk=transposed_parts)

    h_ref[...] = _rms(x_ref[...], g_ref[...]).astype(BF16)
    _shifted_copies(ccx_ref, ccs_ref)

    qt = lax.dot_general(wqt_ref[...], h_ref[...], _NT, preferred_element_type=F32)
    qt_ref[...] = (qt * (LOG2E * ATT_DK ** -0.5)).astype(BF16)
    cc_conv_then_next_projection(0)

    k_ref[...] = jnp.dot(h_ref[...], win_ref[:, pl.ds(K_COL, QK_COLS)],
                         preferred_element_type=F32).astype(BF16)
    cc_conv_then_next_projection(1)

    vt = lax.dot_general(wvt_ref[...], h_ref[...], _NT, preferred_element_type=F32).astype(BF16)
    for hd in range(ATT_HEADS):
        vt_ref[pl.ds(hd * VT_ROWS, ATT_DV), :] = vt[hd * ATT_DV:(hd + 1) * ATT_DV]
        vt_ref[pl.ds(hd * VT_ROWS + ATT_DV, VT_ROWS - ATT_DV), :] = jnp.ones(
            (VT_ROWS - ATT_DV, tm), BF16)
    cc_conv_then_next_projection(2)

    zc = jnp.dot(h_ref[...], win_ref[:, pl.ds(CC_COL, 2 * CC_CH)],
                 preferred_element_type=F32)
    ccn_ref[...] = zc[:, :CC_CH] * jax.nn.sigmoid(zc[:, CC_CH:])
    cc_conv_then_next_projection(3)

    zs = jnp.dot(h_ref[...], win_ref[:, pl.ds(SC_COL, 3 * SC_CH)],
                 preferred_element_type=F32)
    scn_ref[...] = zs[:, SC_CH:2 * SC_CH] * zs[:, 2 * SC_CH:]
    gaten_ref[...] = zs[:, :SC_CH]
    cc_conv(blocks=CC_BLOCK_SPLIT[4])
    _dw_conv(scx_ref, scw_ref, scy_ref, taps=SC_K, halo=SC_HALO, blocks=range(tm // CONV_ROWS))
    oc_ref[:, CC_CH:] = (gate_ref[...] * scy_ref[...]).astype(BF16)

    u = ccy_ref[...] + ccb_ref[...]
    mu = jnp.mean(u, axis=-1, keepdims=True)
    var = jnp.mean(jnp.square(u - mu), axis=-1, keepdims=True)
    y = (u - mu) * lax.rsqrt(var + EPS) * lng_ref[...] + lnb_ref[...]
    oc_ref[:, :CC_CH] = (y * jax.nn.sigmoid(y)).astype(BF16)

    seq_start = lax.rem(g, tiles_per_seq) == 0
    ccx_ref[pl.ds(0, CC_HALO), :] = jnp.where(seq_start, 0.0, ccx_ref[pl.ds(tm, CC_HALO), :])
    ccx_ref[pl.ds(CC_HALO, tm), :] = ccn_ref[...]
    scx_ref[pl.ds(0, SC_HALO), :] = jnp.where(seq_start, 0.0, scx_ref[pl.ds(tm, SC_HALO), :])
    scx_ref[pl.ds(SC_HALO, tm), :] = scn_ref[...]
    gate_ref[...] = gaten_ref[...]


def _mix_in(x, g, w_in, layer, ccw, ccb, lng, lnb, scw):
    b, t, d = x.shape
    tm = ATT_TILE
    nt = t // tm
    n_tiles = b * nt
    qk_cols = QK_COLS
    v_cols = ATT_HEADS * VT_ROWS
    proj = lambda i: jnp.minimum(i, n_tiles - 1)
    conv = lambda i: jnp.maximum(i - 1, 0)
    row = lambda cols, dt, tile: (
        pl.BlockSpec((None, tm, cols), lambda i: (tile(i) // nt, tile(i) % nt, 0)),
        jax.ShapeDtypeStruct((b, t, cols), dt))
    k_spec, k_shape = row(qk_cols, BF16, proj)
    oc_spec, oc_shape = row(CC_CH + SC_CH, BF16, conv)
    slab = lambda rows: (
        pl.BlockSpec((None, None, rows, tm), lambda i: (proj(i) // nt, proj(i) % nt, 0, 0)),
        jax.ShapeDtypeStruct((b, nt, rows, tm), BF16))
    qt_spec, qt_shape = slab(qk_cols)
    vt_spec, vt_shape = slab(v_cols)
    consts = [ccw, ccb, lng, lnb, scw]
    win = _ResidentWeight(w_in, layer)
    assert win.chunk_rows % LANES == 0
    return pl.pallas_call(
        functools.partial(_mix_in_kernel, tiles_per_seq=nt, w_in=win),
        grid=(n_tiles + 1,),
        in_specs=[pl.BlockSpec((None, tm, d), lambda i: (proj(i) // nt, proj(i) % nt, 0)),
                  _const_spec(g.shape), _HBM]
        + [_const_spec(c.shape) for c in consts],
        out_specs=[qt_spec, k_spec, vt_spec, oc_spec],
        out_shape=[qt_shape, k_shape, vt_shape, oc_shape],
        scratch_shapes=[
            pltpu.VMEM((tm, d), BF16),
            pltpu.VMEM((CC_HALO + tm, CC_CH), F32),
            pltpu.VMEM((SC_HALO + tm, SC_CH), F32),
            pltpu.VMEM((tm, SC_CH), F32),
            pltpu.VMEM((tm, CC_CH), F32),
            pltpu.VMEM((tm, SC_CH), F32),
            pltpu.VMEM((tm, SC_CH), F32),
            pltpu.VMEM((SUBLANES - 1, CC_HALO + tm - SUBLANES, CC_CH), F32),
            pltpu.VMEM((tm, CC_CH), F32),
            pltpu.VMEM((tm, SC_CH), F32),
            pltpu.VMEM((QK_COLS, d), BF16),
            pltpu.VMEM((V_COLS, d), BF16),
        ] + win.scratch(),
        compiler_params=_params(1),
        name="mix_in",
    )(x, g, w_in, *consts)


def _diff_attn_kernel(lam_ref, subln_ref, qt_ref, k_ref, vt_ref, o_ref,
                      qs_ref, bias_ref, m_ref, acc_ref, *score_refs, lam_init):
    s_refs, cmax_refs = score_refs[:ATT_SCORE_BUFS], score_refs[ATT_SCORE_BUFS:]
    nt, _, tq = qt_ref.shape
    tk = tq
    head = pl.program_id(1)
    slope = jnp.left_shift(1, 2 * (ATT_HEADS - 1 - head)).astype(F32) * (LOG2E / 256.0)

    jj = lax.broadcasted_iota(jnp.int32, (tk, 2 * tq), 0)
    ii = lax.broadcasted_iota(jnp.int32, (tk, 2 * tq), 1)
    ii = jnp.where(ii >= tq, ii - tq, ii)
    bias = jj.astype(F32) * slope
    bias_ref[0] = bias
    bias_ref[1] = jnp.where(jj <= ii, bias, MASK_VALUE)

    for i in range(nt):
        qt = qt_ref[i]
        row = lax.broadcasted_iota(jnp.int32, qt.shape, 0)
        zero = jnp.zeros_like(qt)
        qs_ref[i, :, pl.ds(0, tq)] = jnp.where(row < ATT_DK, qt, zero)
        qs_ref[i, :, pl.ds(tq, tq)] = jnp.where(row >= ATT_DK, qt, zero)

    m_ref[...] = jnp.full(m_ref.shape, MASK_VALUE, F32)
    acc_ref[...] = jnp.zeros(acc_ref.shape, F32)

    def scores(item, s_ref, cmax_ref):
        qi, c = item
        j0 = pl.multiple_of(c * tk, tk)
        s = jnp.dot(k_ref[pl.ds(j0, tk), :], qs_ref[qi], preferred_element_type=F32)
        s = s + bias_ref[(c == qi).astype(jnp.int32)]
        s_ref[...] = s
        cmax_ref[...] = jnp.max(s, axis=0, keepdims=True)

    def accumulate(item, s_ref, cmax_ref):
        qi, c = item
        shift = slope * (c * tk).astype(F32)
        vtc = vt_ref[c]
        for nb in range(2 * tq // ATT_LANES):
            cols = pl.ds(nb * ATT_LANES, ATT_LANES)
            m_prev = m_ref[qi, :, cols] - shift
            m_new = jnp.maximum(m_prev, cmax_ref[:, cols])
            alpha = jnp.exp2(m_prev - m_new)
            p = jnp.exp2(s_ref[:, cols] - m_new).astype(BF16)
            m_ref[qi, :, cols] = m_new + shift
            pv = jnp.dot(vtc, p, preferred_element_type=F32)
            acc_ref[qi, :, cols] = alpha * acc_ref[qi, :, cols] + pv

    def following(item):
        qi, c = item
        last = c == qi
        return qi + last.astype(jnp.int32), jnp.where(last, 0, c + 1)

    n_items = nt * (nt + 1) // 2
    n_bufs = len(s_refs)
    assert n_items % n_bufs == 0
    bufs = list(zip(s_refs, cmax_refs))

    def following_clamped(item):
        qi, c = following(item)
        past_end = qi == nt
        return jnp.where(past_end, 0, qi), jnp.where(past_end, 0, c)

    ahead = [(jnp.int32(0), jnp.int32(0))]
    for _ in range(n_bufs - 2):
        ahead.append(following_clamped(ahead[-1]))
    for item, buf in zip(ahead, bufs):
        scores(item, *buf)

    def trip(_, carry):
        items = list(carry)
        for slot in range(n_bufs):
            nxt = following_clamped(items[-1])
            scores(nxt, *bufs[(slot + n_bufs - 1) % n_bufs])
            accumulate(items[0], *bufs[slot])
            items = items[1:] + [nxt]
        return tuple(items)

    lax.fori_loop(0, n_items // n_bufs, trip, tuple(ahead))

    lam_p = lam_ref[...]
    lam = (jnp.exp(jnp.sum(lam_p[0:1] * lam_p[1:2], axis=-1, keepdims=True))
           - jnp.exp(jnp.sum(lam_p[2:3] * lam_p[3:4], axis=-1, keepdims=True)) + lam_init)
    for i in range(nt):
        acc = acc_ref[i, pl.ds(0, ATT_DV), :]
        inv_l = 1.0 / acc_ref[i, pl.ds(ATT_DV, 1), :]
        o = acc[:, :tq] * inv_l[:, :tq] - lam * (acc[:, tq:] * inv_l[:, tq:])
        inv_rms = lax.rsqrt(jnp.mean(o * o, axis=0, keepdims=True) + EPS)
        o_ref[i] = (o * inv_rms * subln_ref[...] * (1.0 - lam_init)).astype(BF16)


def _diff_attn(lam_p, subln, qt, k, vt, lam_init):
    b, t, _ = k.shape
    tq = ATT_TILE
    nt = t // tq
    per_head = lambda rows: pl.BlockSpec((None, nt, rows, tq), lambda i, h: (i, 0, h, 0))
    return pl.pallas_call(
        functools.partial(_diff_attn_kernel, lam_init=lam_init),
        grid=(b, ATT_HEADS),
        in_specs=[
            _const_spec(lam_p.shape),
            _const_spec(subln.shape),
            per_head(2 * ATT_DK),
            pl.BlockSpec((None, t, 2 * ATT_DK), lambda i, h: (i, 0, h)),
            per_head(VT_ROWS),
        ],
        out_specs=per_head(ATT_DV),
        out_shape=jax.ShapeDtypeStruct((b, nt, ATT_HEADS * ATT_DV, tq), BF16),
        scratch_shapes=[
            pltpu.VMEM((nt, 2 * ATT_DK, 2 * tq), BF16),
            pltpu.VMEM((2, tq, 2 * tq), F32),
            pltpu.VMEM((nt, 1, 2 * tq), F32),
            pltpu.VMEM((nt, VT_ROWS, 2 * tq), F32),
        ]
        + [pltpu.VMEM((tq, 2 * tq), F32)] * ATT_SCORE_BUFS
        + [pltpu.VMEM((1, 2 * tq), F32)] * ATT_SCORE_BUFS,
        compiler_params=_params(2),
        name="diff_attn",
    )(lam_p, subln, qt, k, vt)


_TN = (((0,), (0,)), ((), ()))


def _mix_out_kernel(x_ref, oat_ref, oc_ref, mem_ref, mg_ref, g_ref, wo_hbm, wq_hbm, wxo_hbm, wkv_hbm,
                    o_ref, kv_ref, *scratch, weights):
    wo_ref, wq_ref, wxo_ref, wkv_ref = scratch[0::3]

    @pl.when((pl.program_id(0) == 0) & (pl.program_id(1) == 0))
    def _():
        for w, hbm, i in zip(weights, (wo_hbm, wq_hbm, wxo_hbm, wkv_hbm), range(0, 12, 3)):
            w.load(hbm, *scratch[i:i + 3])

    @pl.when(pl.program_id(1) == 0)
    def _():
        hm = _rms(mem_ref[...], mg_ref[...]).astype(BF16)
        kv_ref[...] = jnp.dot(hm, wkv_ref[...], preferred_element_type=F32).astype(BF16)

    n_slabs, n_att, _ = oat_ref.shape
    wo_att = wo_ref[pl.ds(0, n_att), :]
    att = [lax.dot_general(oat_ref[s], wo_att, _TN, preferred_element_type=F32)
           for s in range(n_slabs)]
    x = x_ref[...] + jnp.concatenate(att, axis=0)
    x = x + jnp.dot(oc_ref[...], wo_ref[pl.ds(n_att, oc_ref.shape[1]), :],
                    preferred_element_type=F32)

    hq = jnp.dot(_rms(x, g_ref[...]).astype(BF16), wq_ref[...], preferred_element_type=F32)
    hq = (hq * (XA_HD ** -0.5)).astype(BF16)
    kd = XA_HEADS * XA_HD
    heads = []
    for h in range(XA_HEADS):
        sl = slice(h * XA_HD, (h + 1) * XA_HD)
        s = lax.dot_general(hq[:, sl], kv_ref[:, sl], _NT, preferred_element_type=F32)
        p = jnp.exp(s - jnp.max(s, axis=-1, keepdims=True))
        inv = 1.0 / jnp.sum(p, axis=-1, keepdims=True)
        o = jnp.dot(p.astype(BF16), kv_ref[:, kd + h * XA_HD:kd + (h + 1) * XA_HD],
                    preferred_element_type=F32)
        heads.append((o * inv).astype(BF16))
    o = jnp.concatenate(heads, axis=-1)
    o_ref[...] = x + jnp.dot(o, wxo_ref[...], preferred_element_type=F32)


def _mix_out(x, oat, oc, mem, mem_g, g, wo, wq, wxo, wkv, layer):
    b, t, d = x.shape
    tq = oat.shape[3]
    n_slabs = MIX_OUT_TILE // tq
    tm = MIX_OUT_TILE
    m = mem.shape[1]
    row = lambda cols: pl.BlockSpec((None, tm, cols), lambda i, j: (i, j, 0))
    weights = [_ResidentWeight(w, layer) for w in (wo, wq, wxo, wkv)]
    return pl.pallas_call(
        functools.partial(_mix_out_kernel, weights=weights),
        grid=(b, t // tm),
        in_specs=[row(d),
                  pl.BlockSpec((None, n_slabs, oat.shape[2], tq), lambda i, j: (i, j, 0, 0)),
                  row(oc.shape[2]),
                  pl.BlockSpec((None, m, d), lambda i, j: (i, 0, 0)),
                  _const_spec(mem_g.shape), _const_spec(g.shape), _HBM, _HBM, _HBM, _HBM],
        out_specs=row(d),
        out_shape=jax.ShapeDtypeStruct((b, t, d), F32),
        scratch_shapes=[pltpu.VMEM((m, wkv.shape[2]), BF16)]
        + [s for w in weights for s in w.scratch()],
        compiler_params=_params(2),
        name="mix_out",
    )(x, oat, oc, mem, mem_g, g, wo, wq, wxo, wkv)


def kernel(x, mem, ffn1_norm, ffn1_w_gate, ffn1_w_up, ffn1_w_down, mix_norm, w_in, lam_q1, lam_k1, lam_q2, lam_k2, diff_subln, cc_dw, cc_dw_b, cc_ln_g, cc_ln_b, sc_dw, w_out, xa_norm, mem_norm, xa_wq, xa_wkv, xa_wo, ffn2_norm, ffn2_w_gate, ffn2_w_up, ffn2_w_down, final_norm):
    b, t, d = x.shape
    vec = lambda v: v.reshape(1, -1).astype(F32)

    for l in range(DEPTH):
        x = _ffn(x.reshape(b * t, d), vec(ffn1_norm[l]), ffn1_w_gate, ffn1_w_up, ffn1_w_down,
                 l).reshape(b, t, d)

        qt, k, vt, oc = _mix_in(
            x, vec(mix_norm[l]), w_in, l,
            cc_dw[l].astype(F32), vec(cc_dw_b[l]), vec(cc_ln_g[l]), vec(cc_ln_b[l]),
            sc_dw[l].astype(F32))

        lam_init = 0.8 - 0.6 * math.exp(-0.3 * l)
        lam_p = jnp.stack([lam_q1[l], lam_k1[l], lam_q2[l], lam_k2[l]]).astype(F32)
        subln = jnp.broadcast_to(diff_subln[l].astype(F32)[:, None], (ATT_DV, ATT_TILE))
        oat = _diff_attn(lam_p, subln, qt, k, vt, lam_init)

        x = _mix_out(x, oat, oc, mem, vec(mem_norm[l]), vec(xa_norm[l]),
                     w_out, xa_wq, xa_wo, xa_wkv, l)

        last = l == DEPTH - 1
        x = _ffn(x.reshape(b * t, d), vec(ffn2_norm[l]), ffn2_w_gate, ffn2_w_up, ffn2_w_down, l,
                 vec(final_norm) if last else None).reshape(b, t, d)
    return x
```

```python
import functools
import math

import jax
import jax.numpy as jnp
from jax import lax
from jax.experimental import pallas as pl
from jax.experimental.pallas import tpu as pltpu

DEPTH = 2
ATT_HEADS = 4
ATT_DV = 128
ATT_DK = 64
CC_CH = 256
CC_K = 31
SC_CH = 256
SC_K = 3
XA_HEADS = 4
XA_HD = 256
EPS = 1e-6

F32 = jnp.float32
BF16 = jnp.bfloat16

QK_COLS = ATT_HEADS * 2 * ATT_DK
V_COLS = ATT_HEADS * ATT_DV
K_COL = QK_COLS
V_COL = 2 * QK_COLS
CC_COL = V_COL + V_COLS
SC_COL = CC_COL + 2 * CC_CH

TOKEN_TILE = 512
MIX_OUT_TILE = 512
ATT_TILE = 512
VT_ROWS = ATT_DV + 16
LOG2E = math.log2(math.e)
ATT_SCORE_BUFS = 4
ATT_LANES = 256
WEIGHT_CHUNK_BYTES = 3 * 512 * 1024
WEIGHT_SLOTS = 4
LANES = 128
SUBLANES = 8
CONV_ROWS = 64
CC_HALO = 32
SC_HALO = 8
VMEM_LIMIT = 56 * 1024 * 1024
MASK_VALUE = -1e30

_NT = (((1,), (1,)), ((), ()))


def _rms(x, g):
    return x * lax.rsqrt(jnp.mean(x * x, axis=-1, keepdims=True) + EPS) * g


def _const_spec(shape):
    nd = len(shape)
    return pl.BlockSpec(shape, lambda *_: (0,) * nd, pipeline_mode=pl.Buffered(1))


_HBM = pl.BlockSpec(memory_space=pl.ANY)


class _ResidentWeight:
    def __init__(self, stacked, layer):
        _, self.rows, self.cols = stacked.shape
        self.layer = layer
        self.chunk_rows = max(r for r in range(2 * SUBLANES, self.rows + 1, 2 * SUBLANES)
                              if self.rows % r == 0 and r * self.cols * 4 <= WEIGHT_CHUNK_BYTES)

    def scratch(self):
        return [pltpu.VMEM((self.rows, self.cols), BF16),
                pltpu.VMEM((WEIGHT_SLOTS, self.chunk_rows, self.cols), F32),
                pltpu.SemaphoreType.DMA((WEIGHT_SLOTS,))]

    def load(self, hbm_ref, resident_ref, stage_ref, sem_ref, on_chunk=None):
        rc = self.chunk_rows
        n_chunks = self.rows // rc

        def copy(c):
            slot = c % WEIGHT_SLOTS
            return pltpu.make_async_copy(hbm_ref.at[self.layer, pl.ds(c * rc, rc), :],
                                         stage_ref.at[slot], sem_ref.at[slot])

        for c in range(min(WEIGHT_SLOTS - 1, n_chunks)):
            copy(c).start()
        for c in range(n_chunks):
            if c + WEIGHT_SLOTS - 1 < n_chunks:
                copy(c + WEIGHT_SLOTS - 1).start()
            copy(c).wait()
            chunk = stage_ref[c % WEIGHT_SLOTS]
            resident_ref[pl.ds(c * rc, rc), :] = chunk.astype(BF16)
            if on_chunk is not None:
                on_chunk(c * rc, chunk)


def _params(n_axes):
    return pltpu.CompilerParams(
        dimension_semantics=("arbitrary",) * n_axes, vmem_limit_bytes=VMEM_LIMIT)


def _fold_bits(v):
    bits = pltpu.bitcast(v, jnp.uint32)
    folded = None
    for i in range(0, bits.shape[0], SUBLANES):
        for c in range(0, bits.shape[1], LANES):
            tile = bits[i:i + SUBLANES, c:c + LANES]
            folded = tile if folded is None else folded | tile
    return folded


def _order_after(dst_ref, bits):
    half = jnp.uint32(16)
    zero = pltpu.bitcast(lax.shift_right_logical(lax.shift_right_logical(bits, half), half), F32)
    zero = jnp.concatenate([zero, zero], axis=0).astype(dst_ref.dtype)
    dst_ref[...] = dst_ref[...] + zero


def _ffn_kernel(x_ref, g_ref, wg_hbm, wu_hbm, wd_hbm, *rest, final, weights):
    if final:
        fg_ref, o_ref, *scratch = rest
    else:
        o_ref, *scratch = rest
    wg_ref, wu_ref, wd_ref = scratch[0], scratch[3], scratch[6]

    @pl.when(pl.program_id(0) == 0)
    def _():
        for w, hbm, i in zip(weights, (wg_hbm, wu_hbm, wd_hbm), (0, 3, 6)):
            w.load(hbm, *scratch[i:i + 3])

    x = x_ref[...]
    h = _rms(x, g_ref[...]).astype(BF16)
    gate = jnp.dot(h, wg_ref[...], preferred_element_type=F32)
    up = jnp.dot(h, wu_ref[...], preferred_element_type=F32)
    act = (gate * jax.nn.sigmoid(gate) * up).astype(BF16)
    y = x + 0.5 * jnp.dot(act, wd_ref[...], preferred_element_type=F32)
    if final:
        y = _rms(y, fg_ref[...])
    o_ref[...] = y


def _ffn(x, g, wg, wu, wd, layer, final_g=None):
    n, d = x.shape
    tm = TOKEN_TILE
    final = final_g is not None
    row_spec = pl.BlockSpec((tm, d), lambda i: (i, 0))
    weights = [_ResidentWeight(w, layer) for w in (wg, wu, wd)]
    in_specs = [row_spec, _const_spec((1, d)), _HBM, _HBM, _HBM]
    args = [x, g, wg, wu, wd]
    if final:
        in_specs.append(_const_spec((1, d)))
        args.append(final_g)
    return pl.pallas_call(
        functools.partial(_ffn_kernel, final=final, weights=weights),
        grid=(n // tm,),
        in_specs=in_specs,
        out_specs=row_spec,
        out_shape=jax.ShapeDtypeStruct((n, d), F32),
        scratch_shapes=[s for w in weights for s in w.scratch()],
        compiler_params=_params(1),
        name="ffn_final" if final else "ffn",
    )(*args)


def _shifted_copies(ext_ref, shift_ref):
    rows = shift_ref.shape[1]
    for b in range(1, SUBLANES):
        shift_ref[b - 1] = ext_ref[pl.ds(b, rows), :]


def _dw_conv(ext_ref, w_ref, out_ref, *, taps, halo, blocks, shift_ref=None):
    base = halo - (taps - 1)
    folded = None
    for r in blocks:
        acc = None
        for j in range(taps):
            off = base + j
            if shift_ref is None or off % SUBLANES == 0:
                rows = ext_ref[pl.ds(off + r * CONV_ROWS, CONV_ROWS), :]
            else:
                rows = shift_ref[off % SUBLANES - 1,
                                 pl.ds(off - off % SUBLANES + r * CONV_ROWS, CONV_ROWS), :]
            term = rows * w_ref[pl.ds(j, 1), :]
            acc = term if acc is None else acc + term
        out_ref[pl.ds(r * CONV_ROWS, CONV_ROWS), :] = acc
        folded = _fold_bits(acc) if folded is None else folded | _fold_bits(acc)
    return folded


CC_BLOCK_SPLIT = ((0,), (1,), (2, 3), (4, 5), (6, 7))


def _mix_in_kernel(x_ref, g_ref, win_hbm,
                   ccw_ref, ccb_ref, lng_ref, lnb_ref, scw_ref,
                   qt_ref, k_ref, vt_ref, oc_ref,
                   h_ref, ccx_ref, scx_ref, gate_ref, ccn_ref, scn_ref, gaten_ref, ccs_ref,
                   ccy_ref, scy_ref, wqt_ref, wvt_ref, win_ref, win_stage_ref, win_sem,
                   *, tiles_per_seq, w_in):
    tm = x_ref.shape[0]
    g = pl.program_id(0)
    cc_conv = functools.partial(_dw_conv, ccx_ref, ccw_ref, ccy_ref, taps=CC_K, halo=CC_HALO,
                                shift_ref=ccs_ref)

    def cc_conv_then_next_projection(i):
        done = cc_conv(blocks=CC_BLOCK_SPLIT[i])
        _order_after(h_ref.at[pl.ds(0, 16), pl.ds(0, 128)], done)

    @pl.when(g == 0)
    def _():
        ccx_ref[...] = jnp.zeros(ccx_ref.shape, F32)
        scx_ref[...] = jnp.zeros(scx_ref.shape, F32)
        gate_ref[...] = jnp.zeros(gate_ref.shape, F32)

        def transposed_parts(row0, chunk):
            rows = pl.ds(row0, chunk.shape[0])
            wqt_ref[:, rows] = chunk[:, :QK_COLS].T.astype(BF16)
            wvt_ref[:, rows] = chunk[:, V_COL:V_COL + V_COLS].T.astype(BF16)

        w_in.load(win_hbm, win_ref, win_stage_ref, win_sem, on_chunk=transposed_parts)

    h_ref[...] = _rms(x_ref[...], g_ref[...]).astype(BF16)
    _shifted_copies(ccx_ref, ccs_ref)

    qt = lax.dot_general(wqt_ref[...], h_ref[...], _NT, preferred_element_type=F32)
    qt_ref[...] = (qt * (LOG2E * ATT_DK ** -0.5)).astype(BF16)
    cc_conv_then_next_projection(0)

    k_ref[...] = jnp.dot(h_ref[...], win_ref[:, pl.ds(K_COL, QK_COLS)],
                         preferred_element_type=F32).astype(BF16)
    cc_conv_then_next_projection(1)

    vt = lax.dot_general(wvt_ref[...], h_ref[...], _NT, preferred_element_type=F32).astype(BF16)
    for hd in range(ATT_HEADS):
        vt_ref[pl.ds(hd * VT_ROWS, ATT_DV), :] = vt[hd * ATT_DV:(hd + 1) * ATT_DV]
        vt_ref[pl.ds(hd * VT_ROWS + ATT_DV, VT_ROWS - ATT_DV), :] = jnp.ones(
            (VT_ROWS - ATT_DV, tm), BF16)
    cc_conv_then_next_projection(2)

    zc = jnp.dot(h_ref[...], win_ref[:, pl.ds(CC_COL, 2 * CC_CH)],
                 preferred_element_type=F32)
    ccn_ref[...] = zc[:, :CC_CH] * jax.nn.sigmoid(zc[:, CC_CH:])
    cc_conv_then_next_projection(3)

    zs = jnp.dot(h_ref[...], win_ref[:, pl.ds(SC_COL, 3 * SC_CH)],
                 preferred_element_type=F32)
    scn_ref[...] = zs[:, SC_CH:2 * SC_CH] * zs[:, 2 * SC_CH:]
    gaten_ref[...] = zs[:, :SC_CH]
    cc_conv(blocks=CC_BLOCK_SPLIT[4])
    _dw_conv(scx_ref, scw_ref, scy_ref, taps=SC_K, halo=SC_HALO, blocks=range(tm // CONV_ROWS))
    oc_ref[:, CC_CH:] = (gate_ref[...] * scy_ref[...]).astype(BF16)

    u = ccy_ref[...] + ccb_ref[...]
    mu = jnp.mean(u, axis=-1, keepdims=True)
    var = jnp.mean(jnp.square(u - mu), axis=-1, keepdims=True)
    y = (u - mu) * lax.rsqrt(var + EPS) * lng_ref[...] + lnb_ref[...]
    oc_ref[:, :CC_CH] = (y * jax.nn.sigmoid(y)).astype(BF16)

    seq_start = lax.rem(g, tiles_per_seq) == 0
    ccx_ref[pl.ds(0, CC_HALO), :] = jnp.where(seq_start, 0.0, ccx_ref[pl.ds(tm, CC_HALO), :])
    ccx_ref[pl.ds(CC_HALO, tm), :] = ccn_ref[...]
    scx_ref[pl.ds(0, SC_HALO), :] = jnp.where(seq_start, 0.0, scx_ref[pl.ds(tm, SC_HALO), :])
    scx_ref[pl.ds(SC_HALO, tm), :] = scn_ref[...]
    gate_ref[...] = gaten_ref[...]


def _mix_in(x, g, w_in, layer, ccw, ccb, lng, lnb, scw):
    b, t, d = x.shape
    tm = ATT_TILE
    nt = t // tm
    n_tiles = b * nt
    qk_cols = QK_COLS
    v_cols = ATT_HEADS * VT_ROWS
    proj = lambda i: jnp.minimum(i, n_tiles - 1)
    conv = lambda i: jnp.maximum(i - 1, 0)
    row = lambda cols, dt, tile: (
        pl.BlockSpec((None, tm, cols), lambda i: (tile(i) // nt, tile(i) % nt, 0)),
        jax.ShapeDtypeStruct((b, t, cols), dt))
    k_spec, k_shape = row(qk_cols, BF16, proj)
    oc_spec, oc_shape = row(CC_CH + SC_CH, BF16, conv)
    slab = lambda rows: (
        pl.BlockSpec((None, None, rows, tm), lambda i: (proj(i) // nt, proj(i) % nt, 0, 0)),
        jax.ShapeDtypeStruct((b, nt, rows, tm), BF16))
    qt_spec, qt_shape = slab(qk_cols)
    vt_spec, vt_shape = slab(v_cols)
    consts = [ccw, ccb, lng, lnb, scw]
    win = _ResidentWeight(w_in, layer)
    assert win.chunk_rows % LANES == 0
    return pl.pallas_call(
        functools.partial(_mix_in_kernel, tiles_per_seq=nt, w_in=win),
        grid=(n_tiles + 1,),
        in_specs=[pl.BlockSpec((None, tm, d), lambda i: (proj(i) // nt, proj(i) % nt, 0)),
                  _const_spec(g.shape), _HBM]
        + [_const_spec(c.shape) for c in consts],
        out_specs=[qt_spec, k_spec, vt_spec, oc_spec],
        out_shape=[qt_shape, k_shape, vt_shape, oc_shape],
        scratch_shapes=[
            pltpu.VMEM((tm, d), BF16),
            pltpu.VMEM((CC_HALO + tm, CC_CH), F32),
            pltpu.VMEM((SC_HALO + tm, SC_CH), F32),
            pltpu.VMEM((tm, SC_CH), F32),
            pltpu.VMEM((tm, CC_CH), F32),
            pltpu.VMEM((tm, SC_CH), F32),
            pltpu.VMEM((tm, SC_CH), F32),
            pltpu.VMEM((SUBLANES - 1, CC_HALO + tm - SUBLANES, CC_CH), F32),
            pltpu.VMEM((tm, CC_CH), F32),
            pltpu.VMEM((tm, SC_CH), F32),
            pltpu.VMEM((QK_COLS, d), BF16),
            pltpu.VMEM((V_COLS, d), BF16),
        ] + win.scratch(),
        compiler_params=_params(1),
        name="mix_in",
    )(x, g, w_in, *consts)


def _diff_attn_kernel(lam_ref, subln_ref, qt_ref, k_ref, vt_ref, o_ref,
                      qs_ref, bias_ref, m_ref, acc_ref, *score_refs, lam_init):
    s_refs, cmax_refs = score_refs[:ATT_SCORE_BUFS], score_refs[ATT_SCORE_BUFS:]
    nt, _, tq = qt_ref.shape
    tk = tq
    head = pl.program_id(1)
    slope = jnp.left_shift(1, 2 * (ATT_HEADS - 1 - head)).astype(F32) * (LOG2E / 256.0)

    jj = lax.broadcasted_iota(jnp.int32, (tk, 2 * tq), 0)
    ii = lax.broadcasted_iota(jnp.int32, (tk, 2 * tq), 1)
    ii = jnp.where(ii >= tq, ii - tq, ii)
    bias = jj.astype(F32) * slope
    bias_ref[0] = bias
    bias_ref[1] = jnp.where(jj <= ii, bias, MASK_VALUE)

    for i in range(nt):
        qt = qt_ref[i]
        row = lax.broadcasted_iota(jnp.int32, qt.shape, 0)
        zero = jnp.zeros_like(qt)
        qs_ref[i, :, pl.ds(0, tq)] = jnp.where(row < ATT_DK, qt, zero)
        qs_ref[i, :, pl.ds(tq, tq)] = jnp.where(row >= ATT_DK, qt, zero)

    m_ref[...] = jnp.full(m_ref.shape, MASK_VALUE, F32)
    acc_ref[...] = jnp.zeros(acc_ref.shape, F32)

    def scores(item, s_ref, cmax_ref):
        qi, c = item
        j0 = pl.multiple_of(c * tk, tk)
        s = jnp.dot(k_ref[pl.ds(j0, tk), :], qs_ref[qi], preferred_element_type=F32)
        s = s + bias_ref[(c == qi).astype(jnp.int32)]
        s_ref[...] = s
        cmax_ref[...] = jnp.max(s, axis=0, keepdims=True)

    def accumulate(item, s_ref, cmax_ref):
        qi, c = item
        shift = slope * (c * tk).astype(F32)
        vtc = vt_ref[c]
        for nb in range(2 * tq // ATT_LANES):
            cols = pl.ds(nb * ATT_LANES, ATT_LANES)
            m_prev = m_ref[qi, :, cols] - shift
            m_new = jnp.maximum(m_prev, cmax_ref[:, cols])
            alpha = jnp.exp2(m_prev - m_new)
            p = jnp.exp2(s_ref[:, cols] - m_new).astype(BF16)
            m_ref[qi, :, cols] = m_new + shift
            pv = jnp.dot(vtc, p, preferred_element_type=F32)
            acc_ref[qi, :, cols] = alpha * acc_ref[qi, :, cols] + pv

    def following(item):
        qi, c = item
        last = c == qi
        return qi + last.astype(jnp.int32), jnp.where(last, 0, c + 1)

    n_items = nt * (nt + 1) // 2
    n_bufs = len(s_refs)
    assert n_items % n_bufs == 0
    bufs = list(zip(s_refs, cmax_refs))

    def following_clamped(item):
        qi, c = following(item)
        past_end = qi == nt
        return jnp.where(past_end, 0, qi), jnp.where(past_end, 0, c)

    ahead = [(jnp.int32(0), jnp.int32(0))]
    for _ in range(n_bufs - 2):
        ahead.append(following_clamped(ahead[-1]))
    for item, buf in zip(ahead, bufs):
        scores(item, *buf)

    def trip(_, carry):
        items = list(carry)
        for slot in range(n_bufs):
            nxt = following_clamped(items[-1])
            scores(nxt, *bufs[(slot + n_bufs - 1) % n_bufs])
            accumulate(items[0], *bufs[slot])
            items = items[1:] + [nxt]
        return tuple(items)

    lax.fori_loop(0, n_items // n_bufs, trip, tuple(ahead))

    lam_p = lam_ref[...]
    lam = (jnp.exp(jnp.sum(lam_p[0:1] * lam_p[1:2], axis=-1, keepdims=True))
           - jnp.exp(jnp.sum(lam_p[2:3] * lam_p[3:4], axis=-1, keepdims=True)) + lam_init)
    for i in range(nt):
        acc = acc_ref[i, pl.ds(0, ATT_DV), :]
        inv_l = 1.0 / acc_ref[i, pl.ds(ATT_DV, 1), :]
        o = acc[:, :tq] * inv_l[:, :tq] - lam * (acc[:, tq:] * inv_l[:, tq:])
        inv_rms = lax.rsqrt(jnp.mean(o * o, axis=0, keepdims=True) + EPS)
        o_ref[i] = (o * inv_rms * subln_ref[...] * (1.0 - lam_init)).astype(BF16)


def _diff_attn(lam_p, subln, qt, k, vt, lam_init):
    b, t, _ = k.shape
    tq = ATT_TILE
    nt = t // tq
    per_head = lambda rows: pl.BlockSpec((None, nt, rows, tq), lambda i, h: (i, 0, h, 0))
    return pl.pallas_call(
        functools.partial(_diff_attn_kernel, lam_init=lam_init),
        grid=(b, ATT_HEADS),
        in_specs=[
            _const_spec(lam_p.shape),
            _const_spec(subln.shape),
            per_head(2 * ATT_DK),
            pl.BlockSpec((None, t, 2 * ATT_DK), lambda i, h: (i, 0, h)),
            per_head(VT_ROWS),
        ],
        out_specs=per_head(ATT_DV),
        out_shape=jax.ShapeDtypeStruct((b, nt, ATT_HEADS * ATT_DV, tq), BF16),
        scratch_shapes=[
            pltpu.VMEM((nt, 2 * ATT_DK, 2 * tq), BF16),
            pltpu.VMEM((2, tq, 2 * tq), F32),
            pltpu.VMEM((nt, 1, 2 * tq), F32),
            pltpu.VMEM((nt, VT_ROWS, 2 * tq), F32),
        ]
        + [pltpu.VMEM((tq, 2 * tq), F32)] * ATT_SCORE_BUFS
        + [pltpu.VMEM((1, 2 * tq), F32)] * ATT_SCORE_BUFS,
        compiler_params=_params(2),
        name="diff_attn",
    )(lam_p, subln, qt, k, vt)


_TN = (((0,), (0,)), ((), ()))


def _mix_out_kernel(x_ref, oat_ref, oc_ref, mem_ref, mg_ref, g_ref, wo_hbm, wq_hbm, wxo_hbm, wkv_hbm,
                    o_ref, kv_ref, *scratch, weights):
    wo_ref, wq_ref, wxo_ref, wkv_ref = scratch[0::3]

    @pl.when((pl.program_id(0) == 0) & (pl.program_id(1) == 0))
    def _():
        for w, hbm, i in zip(weights, (wo_hbm, wq_hbm, wxo_hbm, wkv_hbm), range(0, 12, 3)):
            w.load(hbm, *scratch[i:i + 3])

    @pl.when(pl.program_id(1) == 0)
    def _():
        hm = _rms(mem_ref[...], mg_ref[...]).astype(BF16)
        kv_ref[...] = jnp.dot(hm, wkv_ref[...], preferred_element_type=F32).astype(BF16)

    n_slabs, n_att, _ = oat_ref.shape
    wo_att = wo_ref[pl.ds(0, n_att), :]
    att = [lax.dot_general(oat_ref[s], wo_att, _TN, preferred_element_type=F32)
           for s in range(n_slabs)]
    x = x_ref[...] + jnp.concatenate(att, axis=0)
    x = x + jnp.dot(oc_ref[...], wo_ref[pl.ds(n_att, oc_ref.shape[1]), :],
                    preferred_element_type=F32)

    hq = jnp.dot(_rms(x, g_ref[...]).astype(BF16), wq_ref[...], preferred_element_type=F32)
    hq = (hq * (XA_HD ** -0.5)).astype(BF16)
    kd = XA_HEADS * XA_HD
    heads = []
    for h in range(XA_HEADS):
        sl = slice(h * XA_HD, (h + 1) * XA_HD)
        s = lax.dot_general(hq[:, sl], kv_ref[:, sl], _NT, preferred_element_type=F32)
        p = jnp.exp(s - jnp.max(s, axis=-1, keepdims=True))
        inv = 1.0 / jnp.sum(p, axis=-1, keepdims=True)
        o = jnp.dot(p.astype(BF16), kv_ref[:, kd + h * XA_HD:kd + (h + 1) * XA_HD],
                    preferred_element_type=F32)
        heads.append((o * inv).astype(BF16))
    o = jnp.concatenate(heads, axis=-1)
    o_ref[...] = x + jnp.dot(o, wxo_ref[...], preferred_element_type=F32)


def _mix_out(x, oat, oc, mem, mem_g, g, wo, wq, wxo, wkv, layer):
    b, t, d = x.shape
    tq = oat.shape[3]
    n_slabs = MIX_OUT_TILE // tq
    tm = MIX_OUT_TILE
    m = mem.shape[1]
    row = lambda cols: pl.BlockSpec((None, tm, cols), lambda i, j: (i, j, 0))
    weights = [_ResidentWeight(w, layer) for w in (wo, wq, wxo, wkv)]
    return pl.pallas_call(
        functools.partial(_mix_out_kernel, weights=weights),
        grid=(b, t // tm),
        in_specs=[row(d),
                  pl.BlockSpec((None, n_slabs, oat.shape[2], tq), lambda i, j: (i, j, 0, 0)),
                  row(oc.shape[2]),
                  pl.BlockSpec((None, m, d), lambda i, j: (i, 0, 0)),
                  _const_spec(mem_g.shape), _const_spec(g.shape), _HBM, _HBM, _HBM, _HBM],
        out_specs=row(d),
        out_shape=jax.ShapeDtypeStruct((b, t, d), F32),
        scratch_shapes=[pltpu.VMEM((m, wkv.shape[2]), BF16)]
        + [s for w in weights for s in w.scratch()],
        compiler_params=_params(2),
        name="mix_out",
    )(x, oat, oc, mem, mem_g, g, wo, wq, wxo, wkv)


def kernel(x, mem, ffn1_norm, ffn1_w_gate, ffn1_w_up, ffn1_w_down, mix_norm, w_in, lam_q1, lam_k1, lam_q2, lam_k2, diff_subln, cc_dw, cc_dw_b, cc_ln_g, cc_ln_b, sc_dw, w_out, xa_norm, mem_norm, xa_wq, xa_wkv, xa_wo, ffn2_norm, ffn2_w_gate, ffn2_w_up, ffn2_w_down, final_norm):
    b, t, d = x.shape
    vec = lambda v: v.reshape(1, -1).astype(F32)

    for l in range(DEPTH):
        x = _ffn(x.reshape(b * t, d), vec(ffn1_norm[l]), ffn1_w_gate, ffn1_w_up, ffn1_w_down,
                 l).reshape(b, t, d)

        qt, k, vt, oc = _mix_in(
            x, vec(mix_norm[l]), w_in, l,
            cc_dw[l].astype(F32), vec(cc_dw_b[l]), vec(cc_ln_g[l]), vec(cc_ln_b[l]),
            sc_dw[l].astype(F32))

        lam_init = 0.8 - 0.6 * math.exp(-0.3 * l)
        lam_p = jnp.stack([lam_q1[l], lam_k1[l], lam_q2[l], lam_k2[l]]).astype(F32)
        subln = jnp.broadcast_to(diff_subln[l].astype(F32)[:, None], (ATT_DV, ATT_TILE))
        oat = _diff_attn(lam_p, subln, qt, k, vt, lam_init)

        x = _mix_out(x, oat, oc, mem, vec(mem_norm[l]), vec(xa_norm[l]),
                     w_out, xa_wq, xa_wo, xa_wkv, l)

        last = l == DEPTH - 1
        x = _ffn(x.reshape(b * t, d), vec(ffn2_norm[l]), ffn2_w_gate, ffn2_w_up, ffn2_w_down, l,
                 vec(final_norm) if last else None).reshape(b, t, d)
    return x
```

```python
import functools
import math

import jax
import jax.numpy as jnp
from jax import lax
from jax.experimental import pallas as pl
from jax.experimental.pallas import tpu as pltpu

DEPTH = 2
ATT_HEADS = 4
ATT_DV = 128
ATT_DK = 64
CC_CH = 256
CC_K = 31
SC_CH = 256
SC_K = 3
XA_HEADS = 4
XA_HD = 256
EPS = 1e-6

F32 = jnp.float32
BF16 = jnp.bfloat16

QK_COLS = ATT_HEADS * 2 * ATT_DK
V_COLS = ATT_HEADS * ATT_DV
K_COL = QK_COLS
V_COL = 2 * QK_COLS
CC_COL = V_COL + V_COLS
SC_COL = CC_COL + 2 * CC_CH

TOKEN_TILE = 1024
FFN_ROWS = 512
MIX_OUT_TILE = 1024
ATT_TILE = 512
VT_ROWS = ATT_DV + 16
LOG2E = math.log2(math.e)
ATT_SCORE_BUFS = 4
ATT_LANES = 256
WEIGHT_CHUNK_BYTES = 3 * 512 * 1024
WEIGHT_SLOTS = 4
LANES = 128
SUBLANES = 8
CONV_ROWS = 64
CC_HALO = 32
SC_HALO = 8
VMEM_LIMIT = 56 * 1024 * 1024
MASK_VALUE = -1e30

_NT = (((1,), (1,)), ((), ()))


def _rms(x, g):
    return x * lax.rsqrt(jnp.mean(x * x, axis=-1, keepdims=True) + EPS) * g


def _const_spec(shape):
    nd = len(shape)
    return pl.BlockSpec(shape, lambda *_: (0,) * nd, pipeline_mode=pl.Buffered(1))


_HBM = pl.BlockSpec(memory_space=pl.ANY)


class _ResidentWeight:
    def __init__(self, stacked, layer):
        _, self.rows, self.cols = stacked.shape
        self.layer = layer
        self.chunk_rows = max(r for r in range(2 * SUBLANES, self.rows + 1, 2 * SUBLANES)
                              if self.rows % r == 0 and r * self.cols * 4 <= WEIGHT_CHUNK_BYTES)

    def scratch(self):
        return pltpu.VMEM((self.rows, self.cols), BF16)

    def load(self, hbm_ref, resident_ref, on_chunk=None):
        rc = self.chunk_rows
        n_chunks = self.rows // rc

        def stream(stage_ref, sem_ref):
            def copy(c):
                slot = c % WEIGHT_SLOTS
                return pltpu.make_async_copy(hbm_ref.at[self.layer, pl.ds(c * rc, rc), :],
                                             stage_ref.at[slot], sem_ref.at[slot])

            for c in range(min(WEIGHT_SLOTS - 1, n_chunks)):
                copy(c).start()
            for c in range(n_chunks):
                if c + WEIGHT_SLOTS - 1 < n_chunks:
                    copy(c + WEIGHT_SLOTS - 1).start()
                copy(c).wait()
                chunk = stage_ref[c % WEIGHT_SLOTS]
                resident_ref[pl.ds(c * rc, rc), :] = chunk.astype(BF16)
                if on_chunk is not None:
                    on_chunk(c * rc, chunk)

        pl.run_scoped(stream, pltpu.VMEM((WEIGHT_SLOTS, rc, self.cols), F32),
                      pltpu.SemaphoreType.DMA((WEIGHT_SLOTS,)))


def _params(n_axes):
    return pltpu.CompilerParams(
        dimension_semantics=("arbitrary",) * n_axes, vmem_limit_bytes=VMEM_LIMIT)


def _fold_bits(v):
    bits = pltpu.bitcast(v, jnp.uint32)
    folded = None
    for i in range(0, bits.shape[0], SUBLANES):
        for c in range(0, bits.shape[1], LANES):
            tile = bits[i:i + SUBLANES, c:c + LANES]
            folded = tile if folded is None else folded | tile
    return folded


def _order_after(dst_ref, bits):
    half = jnp.uint32(16)
    zero = pltpu.bitcast(lax.shift_right_logical(lax.shift_right_logical(bits, half), half), F32)
    zero = jnp.concatenate([zero, zero], axis=0).astype(dst_ref.dtype)
    dst_ref[...] = dst_ref[...] + zero


def _ffn_kernel(x_ref, g_ref, wg_hbm, wu_hbm, wd_hbm, *rest, final, weights):
    if final:
        fg_ref, o_ref, wg_ref, wu_ref, wd_ref = rest
    else:
        o_ref, wg_ref, wu_ref, wd_ref = rest

    @pl.when(pl.program_id(0) == 0)
    def _():
        for w, hbm, resident in zip(weights, (wg_hbm, wu_hbm, wd_hbm), (wg_ref, wu_ref, wd_ref)):
            w.load(hbm, resident)

    for r0 in range(0, x_ref.shape[0], FFN_ROWS):
        rows = pl.ds(r0, FFN_ROWS)
        x = x_ref[rows, :]
        h = _rms(x, g_ref[...]).astype(BF16)
        gate = jnp.dot(h, wg_ref[...], preferred_element_type=F32)
        up = jnp.dot(h, wu_ref[...], preferred_element_type=F32)
        act = (gate * jax.nn.sigmoid(gate) * up).astype(BF16)
        y = x + 0.5 * jnp.dot(act, wd_ref[...], preferred_element_type=F32)
        if final:
            y = _rms(y, fg_ref[...])
        o_ref[rows, :] = y


def _ffn(x, g, wg, wu, wd, layer, final_g=None):
    n, d = x.shape
    tm = TOKEN_TILE
    final = final_g is not None
    row_spec = pl.BlockSpec((tm, d), lambda i: (i, 0))
    weights = [_ResidentWeight(w, layer) for w in (wg, wu, wd)]
    in_specs = [row_spec, _const_spec((1, d)), _HBM, _HBM, _HBM]
    args = [x, g, wg, wu, wd]
    if final:
        in_specs.append(_const_spec((1, d)))
        args.append(final_g)
    return pl.pallas_call(
        functools.partial(_ffn_kernel, final=final, weights=weights),
        grid=(n // tm,),
        in_specs=in_specs,
        out_specs=row_spec,
        out_shape=jax.ShapeDtypeStruct((n, d), F32),
        scratch_shapes=[w.scratch() for w in weights],
        compiler_params=_params(1),
        name="ffn_final" if final else "ffn",
    )(*args)


def _shifted_copies(ext_ref, shift_ref):
    rows = shift_ref.shape[1]
    for b in range(1, SUBLANES):
        shift_ref[b - 1] = ext_ref[pl.ds(b, rows), :]


def _dw_conv(ext_ref, w_ref, out_ref, *, taps, halo, blocks, shift_ref=None):
    base = halo - (taps - 1)
    folded = None
    for r in blocks:
        acc = None
        for j in range(taps):
            off = base + j
            if shift_ref is None or off % SUBLANES == 0:
                rows = ext_ref[pl.ds(off + r * CONV_ROWS, CONV_ROWS), :]
            else:
                rows = shift_ref[off % SUBLANES - 1,
                                 pl.ds(off - off % SUBLANES + r * CONV_ROWS, CONV_ROWS), :]
            term = rows * w_ref[pl.ds(j, 1), :]
            acc = term if acc is None else acc + term
        out_ref[pl.ds(r * CONV_ROWS, CONV_ROWS), :] = acc
        folded = _fold_bits(acc) if folded is None else folded | _fold_bits(acc)
    return folded


CC_BLOCK_SPLIT = ((0,), (1,), (2, 3), (4, 5), (6, 7))


def _mix_in_kernel(x_ref, g_ref, win_hbm,
                   ccw_ref, ccb_ref, lng_ref, lnb_ref, scw_ref,
                   qt_ref, k_ref, vt_ref, oc_ref,
                   h_ref, ccx_ref, scx_ref, gate_ref, ccn_ref, scn_ref, gaten_ref, ccs_ref,
                   ccy_ref, scy_ref, wqt_ref, wvt_ref, win_ref,
                   *, tiles_per_seq, w_in):
    tm = x_ref.shape[0]
    g = pl.program_id(0)
    cc_conv = functools.partial(_dw_conv, ccx_ref, ccw_ref, ccy_ref, taps=CC_K, halo=CC_HALO,
                                shift_ref=ccs_ref)

    def cc_conv_then_next_projection(i):
        done = cc_conv(blocks=CC_BLOCK_SPLIT[i])
        _order_after(h_ref.at[pl.ds(0, 16), pl.ds(0, 128)], done)

    @pl.when(g == 0)
    def _():
        ccx_ref[...] = jnp.zeros(ccx_ref.shape, F32)
        scx_ref[...] = jnp.zeros(scx_ref.shape, F32)
        gate_ref[...] = jnp.zeros(gate_ref.shape, F32)

        def transposed_parts(row0, chunk):
            rows = pl.ds(row0, chunk.shape[0])
            wqt_ref[:, rows] = chunk[:, :QK_COLS].T.astype(BF16)
            wvt_ref[:, rows] = chunk[:, V_COL:V_COL + V_COLS].T.astype(BF16)

        w_in.load(win_hbm, win_ref, on_chunk=transposed_parts)

    h_ref[...] = _rms(x_ref[...], g_ref[...]).astype(BF16)
    _shifted_copies(ccx_ref, ccs_ref)

    qt = lax.dot_general(wqt_ref[...], h_ref[...], _NT, preferred_element_type=F32)
    qt_ref[...] = (qt * (LOG2E * ATT_DK ** -0.5)).astype(BF16)
    cc_conv_then_next_projection(0)

    k_ref[...] = jnp.dot(h_ref[...], win_ref[:, pl.ds(K_COL, QK_COLS)],
                         preferred_element_type=F32).astype(BF16)
    cc_conv_then_next_projection(1)

    vt = lax.dot_general(wvt_ref[...], h_ref[...], _NT, preferred_element_type=F32).astype(BF16)
    for hd in range(ATT_HEADS):
        vt_ref[pl.ds(hd * VT_ROWS, ATT_DV), :] = vt[hd * ATT_DV:(hd + 1) * ATT_DV]
        vt_ref[pl.ds(hd * VT_ROWS + ATT_DV, VT_ROWS - ATT_DV), :] = jnp.ones(
            (VT_ROWS - ATT_DV, tm), BF16)
    cc_conv_then_next_projection(2)

    zc = jnp.dot(h_ref[...], win_ref[:, pl.ds(CC_COL, 2 * CC_CH)],
                 preferred_element_type=F32)
    ccn_ref[...] = zc[:, :CC_CH] * jax.nn.sigmoid(zc[:, CC_CH:])
    cc_conv_then_next_projection(3)

    zs = jnp.dot(h_ref[...], win_ref[:, pl.ds(SC_COL, 3 * SC_CH)],
                 preferred_element_type=F32)
    scn_ref[...] = zs[:, SC_CH:2 * SC_CH] * zs[:, 2 * SC_CH:]
    gaten_ref[...] = zs[:, :SC_CH]
    cc_conv(blocks=CC_BLOCK_SPLIT[4])
    _dw_conv(scx_ref, scw_ref, scy_ref, taps=SC_K, halo=SC_HALO, blocks=range(tm // CONV_ROWS))
    oc_ref[:, CC_CH:] = (gate_ref[...] * scy_ref[...]).astype(BF16)

    u = ccy_ref[...] + ccb_ref[...]
    mu = jnp.mean(u, axis=-1, keepdims=True)
    var = jnp.mean(jnp.square(u - mu), axis=-1, keepdims=True)
    y = (u - mu) * lax.rsqrt(var + EPS) * lng_ref[...] + lnb_ref[...]
    oc_ref[:, :CC_CH] = (y * jax.nn.sigmoid(y)).astype(BF16)

    seq_start = lax.rem(g, tiles_per_seq) == 0
    ccx_ref[pl.ds(0, CC_HALO), :] = jnp.where(seq_start, 0.0, ccx_ref[pl.ds(tm, CC_HALO), :])
    ccx_ref[pl.ds(CC_HALO, tm), :] = ccn_ref[...]
    scx_ref[pl.ds(0, SC_HALO), :] = jnp.where(seq_start, 0.0, scx_ref[pl.ds(tm, SC_HALO), :])
    scx_ref[pl.ds(SC_HALO, tm), :] = scn_ref[...]
    gate_ref[...] = gaten_ref[...]


def _mix_in(x, g, w_in, layer, ccw, ccb, lng, lnb, scw):
    b, t, d = x.shape
    tm = ATT_TILE
    nt = t // tm
    n_tiles = b * nt
    qk_cols = QK_COLS
    v_cols = ATT_HEADS * VT_ROWS
    proj = lambda i: jnp.minimum(i, n_tiles - 1)
    conv = lambda i: jnp.maximum(i - 1, 0)
    row = lambda cols, dt, tile: (
        pl.BlockSpec((None, tm, cols), lambda i: (tile(i) // nt, tile(i) % nt, 0)),
        jax.ShapeDtypeStruct((b, t, cols), dt))
    k_spec, k_shape = row(qk_cols, BF16, proj)
    oc_spec, oc_shape = row(CC_CH + SC_CH, BF16, conv)
    slab = lambda rows: (
        pl.BlockSpec((None, None, rows, tm), lambda i: (proj(i) // nt, proj(i) % nt, 0, 0)),
        jax.ShapeDtypeStruct((b, nt, rows, tm), BF16))
    qt_spec, qt_shape = slab(qk_cols)
    vt_spec, vt_shape = slab(v_cols)
    consts = [ccw, ccb, lng, lnb, scw]
    win = _ResidentWeight(w_in, layer)
    assert win.chunk_rows % LANES == 0
    return pl.pallas_call(
        functools.partial(_mix_in_kernel, tiles_per_seq=nt, w_in=win),
        grid=(n_tiles + 1,),
        in_specs=[pl.BlockSpec((None, tm, d), lambda i: (proj(i) // nt, proj(i) % nt, 0)),
                  _const_spec(g.shape), _HBM]
        + [_const_spec(c.shape) for c in consts],
        out_specs=[qt_spec, k_spec, vt_spec, oc_spec],
        out_shape=[qt_shape, k_shape, vt_shape, oc_shape],
        scratch_shapes=[
            pltpu.VMEM((tm, d), BF16),
            pltpu.VMEM((CC_HALO + tm, CC_CH), F32),
            pltpu.VMEM((SC_HALO + tm, SC_CH), F32),
            pltpu.VMEM((tm, SC_CH), F32),
            pltpu.VMEM((tm, CC_CH), F32),
            pltpu.VMEM((tm, SC_CH), F32),
            pltpu.VMEM((tm, SC_CH), F32),
            pltpu.VMEM((SUBLANES - 1, CC_HALO + tm - SUBLANES, CC_CH), F32),
            pltpu.VMEM((tm, CC_CH), F32),
            pltpu.VMEM((tm, SC_CH), F32),
            pltpu.VMEM((QK_COLS, d), BF16),
            pltpu.VMEM((V_COLS, d), BF16),
            win.scratch(),
        ],
        compiler_params=_params(1),
        name="mix_in",
    )(x, g, w_in, *consts)


def _diff_attn_kernel(lam_ref, subln_ref, qt_ref, k_ref, vt_ref, o_ref,
                      qs_ref, bias_ref, m_ref, acc_ref, *score_refs, lam_init):
    s_refs, cmax_refs = score_refs[:ATT_SCORE_BUFS], score_refs[ATT_SCORE_BUFS:]
    nt, _, tq = qt_ref.shape
    tk = tq
    head = pl.program_id(1)
    slope = jnp.left_shift(1, 2 * (ATT_HEADS - 1 - head)).astype(F32) * (LOG2E / 256.0)

    jj = lax.broadcasted_iota(jnp.int32, (tk, 2 * tq), 0)
    ii = lax.broadcasted_iota(jnp.int32, (tk, 2 * tq), 1)
    ii = jnp.where(ii >= tq, ii - tq, ii)
    bias = jj.astype(F32) * slope
    bias_ref[0] = bias
    bias_ref[1] = jnp.where(jj <= ii, bias, MASK_VALUE)

    for i in range(nt):
        qt = qt_ref[i]
        row = lax.broadcasted_iota(jnp.int32, qt.shape, 0)
        zero = jnp.zeros_like(qt)
        qs_ref[i, :, pl.ds(0, tq)] = jnp.where(row < ATT_DK, qt, zero)
        qs_ref[i, :, pl.ds(tq, tq)] = jnp.where(row >= ATT_DK, qt, zero)

    m_ref[...] = jnp.full(m_ref.shape, MASK_VALUE, F32)
    acc_ref[...] = jnp.zeros(acc_ref.shape, F32)

    def scores(item, s_ref, cmax_ref):
        qi, c = item
        j0 = pl.multiple_of(c * tk, tk)
        s = jnp.dot(k_ref[pl.ds(j0, tk), :], qs_ref[qi], preferred_element_type=F32)
        s = s + bias_ref[(c == qi).astype(jnp.int32)]
        s_ref[...] = s
        cmax_ref[...] = jnp.max(s, axis=0, keepdims=True)

    def accumulate(item, s_ref, cmax_ref):
        qi, c = item
        shift = slope * (c * tk).astype(F32)
        vtc = vt_ref[c]
        for nb in range(2 * tq // ATT_LANES):
            cols = pl.ds(nb * ATT_LANES, ATT_LANES)
            m_prev = m_ref[qi, :, cols] - shift
            m_new = jnp.maximum(m_prev, cmax_ref[:, cols])
            alpha = jnp.exp2(m_prev - m_new)
            p = jnp.exp2(s_ref[:, cols] - m_new).astype(BF16)
            m_ref[qi, :, cols] = m_new + shift
            pv = jnp.dot(vtc, p, preferred_element_type=F32)
            acc_ref[qi, :, cols] = alpha * acc_ref[qi, :, cols] + pv

    def following(item):
        qi, c = item
        last = c == qi
        return qi + last.astype(jnp.int32), jnp.where(last, 0, c + 1)

    n_items = nt * (nt + 1) // 2
    n_bufs = len(s_refs)
    assert n_items % n_bufs == 0
    bufs = list(zip(s_refs, cmax_refs))

    def following_clamped(item):
        qi, c = following(item)
        past_end = qi == nt
        return jnp.where(past_end, 0, qi), jnp.where(past_end, 0, c)

    ahead = [(jnp.int32(0), jnp.int32(0))]
    for _ in range(n_bufs - 2):
        ahead.append(following_clamped(ahead[-1]))
    for item, buf in zip(ahead, bufs):
        scores(item, *buf)

    def trip(_, carry):
        items = list(carry)
        for slot in range(n_bufs):
            nxt = following_clamped(items[-1])
            scores(nxt, *bufs[(slot + n_bufs - 1) % n_bufs])
            accumulate(items[0], *bufs[slot])
            items = items[1:] + [nxt]
        return tuple(items)

    lax.fori_loop(0, n_items // n_bufs, trip, tuple(ahead))

    lam_p = lam_ref[...]
    lam = (jnp.exp(jnp.sum(lam_p[0:1] * lam_p[1:2], axis=-1, keepdims=True))
           - jnp.exp(jnp.sum(lam_p[2:3] * lam_p[3:4], axis=-1, keepdims=True)) + lam_init)
    for i in range(nt):
        acc = acc_ref[i, pl.ds(0, ATT_DV), :]
        inv_l = 1.0 / acc_ref[i, pl.ds(ATT_DV, 1), :]
        o = acc[:, :tq] * inv_l[:, :tq] - lam * (acc[:, tq:] * inv_l[:, tq:])
        inv_rms = lax.rsqrt(jnp.mean(o * o, axis=0, keepdims=True) + EPS)
        o_ref[i] = (o * inv_rms * subln_ref[...] * (1.0 - lam_init)).astype(BF16)


def _diff_attn(lam_p, subln, qt, k, vt, lam_init):
    b, t, _ = k.shape
    tq = ATT_TILE
    nt = t // tq
    per_head = lambda rows: pl.BlockSpec((None, nt, rows, tq), lambda i, h: (i, 0, h, 0))
    return pl.pallas_call(
        functools.partial(_diff_attn_kernel, lam_init=lam_init),
        grid=(b, ATT_HEADS),
        in_specs=[
            _const_spec(lam_p.shape),
            _const_spec(subln.shape),
            per_head(2 * ATT_DK),
            pl.BlockSpec((None, t, 2 * ATT_DK), lambda i, h: (i, 0, h)),
            per_head(VT_ROWS),
        ],
        out_specs=per_head(ATT_DV),
        out_shape=jax.ShapeDtypeStruct((b, nt, ATT_HEADS * ATT_DV, tq), BF16),
        scratch_shapes=[
            pltpu.VMEM((nt, 2 * ATT_DK, 2 * tq), BF16),
            pltpu.VMEM((2, tq, 2 * tq), F32),
            pltpu.VMEM((nt, 1, 2 * tq), F32),
            pltpu.VMEM((nt, VT_ROWS, 2 * tq), F32),
        ]
        + [pltpu.VMEM((tq, 2 * tq), F32)] * ATT_SCORE_BUFS
        + [pltpu.VMEM((1, 2 * tq), F32)] * ATT_SCORE_BUFS,
        compiler_params=_params(2),
        name="diff_attn",
    )(lam_p, subln, qt, k, vt)


_TN = (((0,), (0,)), ((), ()))


def _mix_out_kernel(x_ref, oat_ref, oc_ref, mem_ref, mg_ref, g_ref, wo_hbm, wq_hbm, wxo_hbm, wkv_hbm,
                    o_ref, kv_ref, *scratch, weights):
    wo_ref, wq_ref, wxo_ref, wkv_ref = scratch

    @pl.when((pl.program_id(0) == 0) & (pl.program_id(1) == 0))
    def _():
        for w, hbm, resident in zip(weights, (wo_hbm, wq_hbm, wxo_hbm, wkv_hbm), scratch):
            w.load(hbm, resident)

    @pl.when(pl.program_id(1) == 0)
    def _():
        hm = _rms(mem_ref[...], mg_ref[...]).astype(BF16)
        kv_ref[...] = jnp.dot(hm, wkv_ref[...], preferred_element_type=F32).astype(BF16)

    n_slabs, n_att, _ = oat_ref.shape
    wo_att = wo_ref[pl.ds(0, n_att), :]
    att = [lax.dot_general(oat_ref[s], wo_att, _TN, preferred_element_type=F32)
           for s in range(n_slabs)]
    x = x_ref[...] + jnp.concatenate(att, axis=0)
    x = x + jnp.dot(oc_ref[...], wo_ref[pl.ds(n_att, oc_ref.shape[1]), :],
                    preferred_element_type=F32)

    hq = jnp.dot(_rms(x, g_ref[...]).astype(BF16), wq_ref[...], preferred_element_type=F32)
    hq = (hq * (XA_HD ** -0.5)).astype(BF16)
    kd = XA_HEADS * XA_HD
    heads = []
    for h in range(XA_HEADS):
        sl = slice(h * XA_HD, (h + 1) * XA_HD)
        s = lax.dot_general(hq[:, sl], kv_ref[:, sl], _NT, preferred_element_type=F32)
        p = jnp.exp(s - jnp.max(s, axis=-1, keepdims=True))
        inv = 1.0 / jnp.sum(p, axis=-1, keepdims=True)
        o = jnp.dot(p.astype(BF16), kv_ref[:, kd + h * XA_HD:kd + (h + 1) * XA_HD],
                    preferred_element_type=F32)
        heads.append((o * inv).astype(BF16))
    o = jnp.concatenate(heads, axis=-1)
    o_ref[...] = x + jnp.dot(o, wxo_ref[...], preferred_element_type=F32)


def _mix_out(x, oat, oc, mem, mem_g, g, wo, wq, wxo, wkv, layer):
    b, t, d = x.shape
    tq = oat.shape[3]
    n_slabs = MIX_OUT_TILE // tq
    tm = MIX_OUT_TILE
    m = mem.shape[1]
    row = lambda cols: pl.BlockSpec((None, tm, cols), lambda i, j: (i, j, 0))
    weights = [_ResidentWeight(w, layer) for w in (wo, wq, wxo, wkv)]
    return pl.pallas_call(
        functools.partial(_mix_out_kernel, weights=weights),
        grid=(b, t // tm),
        in_specs=[row(d),
                  pl.BlockSpec((None, n_slabs, oat.shape[2], tq), lambda i, j: (i, j, 0, 0)),
                  row(oc.shape[2]),
                  pl.BlockSpec((None, m, d), lambda i, j: (i, 0, 0)),
                  _const_spec(mem_g.shape), _const_spec(g.shape), _HBM, _HBM, _HBM, _HBM],
        out_specs=row(d),
        out_shape=jax.ShapeDtypeStruct((b, t, d), F32),
        scratch_shapes=[pltpu.VMEM((m, wkv.shape[2]), BF16)]
        + [w.scratch() for w in weights],
        compiler_params=_params(2),
        name="mix_out",
    )(x, oat, oc, mem, mem_g, g, wo, wq, wxo, wkv)


def kernel(x, mem, ffn1_norm, ffn1_w_gate, ffn1_w_up, ffn1_w_down, mix_norm, w_in, lam_q1, lam_k1, lam_q2, lam_k2, diff_subln, cc_dw, cc_dw_b, cc_ln_g, cc_ln_b, sc_dw, w_out, xa_norm, mem_norm, xa_wq, xa_wkv, xa_wo, ffn2_norm, ffn2_w_gate, ffn2_w_up, ffn2_w_down, final_norm):
    b, t, d = x.shape
    vec = lambda v: v.reshape(1, -1).astype(F32)

    for l in range(DEPTH):
        x = _ffn(x.reshape(b * t, d), vec(ffn1_norm[l]), ffn1_w_gate, ffn1_w_up, ffn1_w_down,
                 l).reshape(b, t, d)

        qt, k, vt, oc = _mix_in(
            x, vec(mix_norm[l]), w_in, l,
            cc_dw[l].astype(F32), vec(cc_dw_b[l]), vec(cc_ln_g[l]), vec(cc_ln_b[l]),
            sc_dw[l].astype(F32))

        lam_init = 0.8 - 0.6 * math.exp(-0.3 * l)
        lam_p = jnp.stack([lam_q1[l], lam_k1[l], lam_q2[l], lam_k2[l]]).astype(F32)
        subln = jnp.broadcast_to(diff_subln[l].astype(F32)[:, None], (ATT_DV, ATT_TILE))
        oat = _diff_attn(lam_p, subln, qt, k, vt, lam_init)

        x = _mix_out(x, oat, oc, mem, vec(mem_norm[l]), vec(xa_norm[l]),
                     w_out, xa_wq, xa_wo, xa_wkv, l)

        last = l == DEPTH - 1
        x = _ffn(x.reshape(b * t, d), vec(ffn2_norm[l]), ffn2_w_gate, ffn2_w_up, ffn2_w_down, l,
                 vec(final_norm) if last else None).reshape(b, t, d)
    return x
```

```python
import functools
import math

import jax
import jax.numpy as jnp
from jax import lax
from jax.experimental import pallas as pl
from jax.experimental.pallas import tpu as pltpu

DEPTH = 2
ATT_HEADS = 4
ATT_DV = 128
ATT_DK = 64
CC_CH = 256
CC_K = 31
SC_CH = 256
SC_K = 3
XA_HEADS = 4
XA_HD = 256
EPS = 1e-6

F32 = jnp.float32
BF16 = jnp.bfloat16

QK_COLS = ATT_HEADS * 2 * ATT_DK
V_COLS = ATT_HEADS * ATT_DV
K_COL = QK_COLS
V_COL = 2 * QK_COLS
CC_COL = V_COL + V_COLS
SC_COL = CC_COL + 2 * CC_CH

TOKEN_TILE = 1024
FFN_ROWS = 512
MIX_OUT_TILE = 1024
ATT_TILE = 512
VT_ROWS = ATT_DV + 2 * 8
LOG2E = math.log2(math.e)
ATT_SCORE_BUFS = 4
ATT_LANES = 256
WEIGHT_CHUNK_BYTES = 3 * 512 * 1024
WEIGHT_SLOTS = 6
LANES = 128
SUBLANES = 8
CONV_ROWS = 64
CC_HALO = 32
SC_HALO = 8
VMEM_LIMIT = 56 * 1024 * 1024
MASK_VALUE = -1e30

_NT = (((1,), (1,)), ((), ()))


def _rms(x, g):
    return x * lax.rsqrt(jnp.mean(x * x, axis=-1, keepdims=True) + EPS) * g


def _const_spec(shape):
    nd = len(shape)
    return pl.BlockSpec(shape, lambda *_: (0,) * nd, pipeline_mode=pl.Buffered(1))


_HBM = pl.BlockSpec(memory_space=pl.ANY)


class _ResidentWeight:
    def __init__(self, stacked, layer):
        _, self.rows, self.cols = stacked.shape
        self.layer = layer
        self.chunk_rows = max(r for r in range(2 * SUBLANES, self.rows + 1, 2 * SUBLANES)
                              if self.rows % r == 0 and r * self.cols * 4 <= WEIGHT_CHUNK_BYTES)

    def scratch(self):
        return pltpu.VMEM((self.rows, self.cols), BF16)

    def load(self, hbm_ref, resident_ref, on_chunk=None):
        rc = self.chunk_rows
        n_chunks = self.rows // rc

        def stream(stage_ref, sem_ref):
            def copy(c):
                slot = c % WEIGHT_SLOTS
                return pltpu.make_async_copy(hbm_ref.at[self.layer, pl.ds(c * rc, rc), :],
                                             stage_ref.at[slot], sem_ref.at[slot])

            for c in range(min(WEIGHT_SLOTS - 1, n_chunks)):
                copy(c).start()
            for c in range(n_chunks):
                if c + WEIGHT_SLOTS - 1 < n_chunks:
                    copy(c + WEIGHT_SLOTS - 1).start()
                copy(c).wait()
                chunk = stage_ref[c % WEIGHT_SLOTS]
                resident_ref[pl.ds(c * rc, rc), :] = chunk.astype(BF16)
                if on_chunk is not None:
                    on_chunk(c * rc, chunk)

        pl.run_scoped(stream, pltpu.VMEM((WEIGHT_SLOTS, rc, self.cols), F32),
                      pltpu.SemaphoreType.DMA((WEIGHT_SLOTS,)))


def _params(n_axes):
    return pltpu.CompilerParams(
        dimension_semantics=("arbitrary",) * n_axes, vmem_limit_bytes=VMEM_LIMIT)


def _fold_bits(v):
    bits = pltpu.bitcast(v, jnp.uint32)
    folded = None
    for i in range(0, bits.shape[0], SUBLANES):
        for c in range(0, bits.shape[1], LANES):
            tile = bits[i:i + SUBLANES, c:c + LANES]
            folded = tile if folded is None else folded | tile
    return folded


def _order_after(dst_ref, bits):
    half = jnp.uint32(16)
    zero = pltpu.bitcast(lax.shift_right_logical(lax.shift_right_logical(bits, half), half), F32)
    zero = jnp.concatenate([zero, zero], axis=0).astype(dst_ref.dtype)
    dst_ref[...] = dst_ref[...] + zero


def _ffn_kernel(x_ref, g_ref, wg_hbm, wu_hbm, wd_hbm, *rest, final, weights):
    if final:
        fg_ref, o_ref, wg_ref, wu_ref, wd_ref = rest
    else:
        o_ref, wg_ref, wu_ref, wd_ref = rest

    @pl.when(pl.program_id(0) == 0)
    def _():
        for w, hbm, resident in zip(weights, (wg_hbm, wu_hbm, wd_hbm), (wg_ref, wu_ref, wd_ref)):
            w.load(hbm, resident)

    for r0 in range(0, x_ref.shape[0], FFN_ROWS):
        rows = pl.ds(r0, FFN_ROWS)
        x = x_ref[rows, :]
        h = _rms(x, g_ref[...]).astype(BF16)
        gate = jnp.dot(h, wg_ref[...], preferred_element_type=F32)
        up = jnp.dot(h, wu_ref[...], preferred_element_type=F32)
        act = (gate * jax.nn.sigmoid(gate) * up).astype(BF16)
        y = x + 0.5 * jnp.dot(act, wd_ref[...], preferred_element_type=F32)
        if final:
            y = _rms(y, fg_ref[...])
        o_ref[rows, :] = y


def _ffn(x, g, wg, wu, wd, layer, final_g=None):
    n, d = x.shape
    tm = TOKEN_TILE
    final = final_g is not None
    row_spec = pl.BlockSpec((tm, d), lambda i: (i, 0))
    weights = [_ResidentWeight(w, layer) for w in (wg, wu, wd)]
    in_specs = [row_spec, _const_spec((1, d)), _HBM, _HBM, _HBM]
    args = [x, g, wg, wu, wd]
    if final:
        in_specs.append(_const_spec((1, d)))
        args.append(final_g)
    return pl.pallas_call(
        functools.partial(_ffn_kernel, final=final, weights=weights),
        grid=(n // tm,),
        in_specs=in_specs,
        out_specs=row_spec,
        out_shape=jax.ShapeDtypeStruct((n, d), F32),
        scratch_shapes=[w.scratch() for w in weights],
        compiler_params=_params(1),
        name="ffn_final" if final else "ffn",
    )(*args)


def _shifted_copies(ext_ref, shift_ref):
    rows = shift_ref.shape[1]
    for b in range(1, SUBLANES):
        shift_ref[b - 1] = ext_ref[pl.ds(b, rows), :]


def _dw_conv(ext_ref, w_ref, out_ref, *, taps, halo, blocks, shift_ref=None):
    base = halo - (taps - 1)
    folded = None
    for r in blocks:
        acc = None
        for j in range(taps):
            off = base + j
            if shift_ref is None or off % SUBLANES == 0:
                rows = ext_ref[pl.ds(off + r * CONV_ROWS, CONV_ROWS), :]
            else:
                rows = shift_ref[off % SUBLANES - 1,
                                 pl.ds(off - off % SUBLANES + r * CONV_ROWS, CONV_ROWS), :]
            term = rows * w_ref[pl.ds(j, 1), :]
            acc = term if acc is None else acc + term
        out_ref[pl.ds(r * CONV_ROWS, CONV_ROWS), :] = acc
        folded = _fold_bits(acc) if folded is None else folded | _fold_bits(acc)
    return folded


CC_BLOCK_SPLIT = ((0,), (1,), (2, 3), (4, 5), (6, 7))


def _mix_in_kernel(x_ref, g_ref, win_hbm,
                   ccw_ref, ccb_ref, lng_ref, lnb_ref, scw_ref,
                   qt_ref, k_ref, vt_ref, oc_ref,
                   h_ref, ccx_ref, scx_ref, gate_ref, ccn_ref, scn_ref, gaten_ref, ccs_ref,
                   ccy_ref, scy_ref, wqt_ref, wvt_ref, win_ref,
                   *, tiles_per_seq, w_in):
    tm = x_ref.shape[0]
    g = pl.program_id(0)
    cc_conv = functools.partial(_dw_conv, ccx_ref, ccw_ref, ccy_ref, taps=CC_K, halo=CC_HALO,
                                shift_ref=ccs_ref)

    def cc_conv_then_next_projection(i):
        done = cc_conv(blocks=CC_BLOCK_SPLIT[i])
        _order_after(h_ref.at[pl.ds(0, 2 * SUBLANES), pl.ds(0, LANES)], done)

    @pl.when(g == 0)
    def _():
        ccx_ref[...] = jnp.zeros(ccx_ref.shape, F32)
        scx_ref[...] = jnp.zeros(scx_ref.shape, F32)
        gate_ref[...] = jnp.zeros(gate_ref.shape, F32)

        def transposed_parts(row0, chunk):
            rows = pl.ds(row0, chunk.shape[0])
            wqt_ref[:, rows] = chunk[:, :QK_COLS].T.astype(BF16)
            wvt_ref[:, rows] = chunk[:, V_COL:V_COL + V_COLS].T.astype(BF16)

        w_in.load(win_hbm, win_ref, on_chunk=transposed_parts)

    h_ref[...] = _rms(x_ref[...], g_ref[...]).astype(BF16)
    _shifted_copies(ccx_ref, ccs_ref)

    qt = lax.dot_general(wqt_ref[...], h_ref[...], _NT, preferred_element_type=F32)
    qt_ref[...] = (qt * (LOG2E * ATT_DK ** -0.5)).astype(BF16)
    cc_conv_then_next_projection(0)

    k_ref[...] = jnp.dot(h_ref[...], win_ref[:, pl.ds(K_COL, QK_COLS)],
                         preferred_element_type=F32).astype(BF16)
    cc_conv_then_next_projection(1)

    vt = lax.dot_general(wvt_ref[...], h_ref[...], _NT, preferred_element_type=F32).astype(BF16)
    for hd in range(ATT_HEADS):
        vt_ref[pl.ds(hd * VT_ROWS, ATT_DV), :] = vt[hd * ATT_DV:(hd + 1) * ATT_DV]
        vt_ref[pl.ds(hd * VT_ROWS + ATT_DV, VT_ROWS - ATT_DV), :] = jnp.ones(
            (VT_ROWS - ATT_DV, tm), BF16)
    cc_conv_then_next_projection(2)

    zc = jnp.dot(h_ref[...], win_ref[:, pl.ds(CC_COL, 2 * CC_CH)],
                 preferred_element_type=F32)
    ccn_ref[...] = zc[:, :CC_CH] * jax.nn.sigmoid(zc[:, CC_CH:])
    cc_conv_then_next_projection(3)

    zs = jnp.dot(h_ref[...], win_ref[:, pl.ds(SC_COL, 3 * SC_CH)],
                 preferred_element_type=F32)
    scn_ref[...] = zs[:, SC_CH:2 * SC_CH] * zs[:, 2 * SC_CH:]
    gaten_ref[...] = zs[:, :SC_CH]
    cc_conv(blocks=CC_BLOCK_SPLIT[4])
    _dw_conv(scx_ref, scw_ref, scy_ref, taps=SC_K, halo=SC_HALO, blocks=range(tm // CONV_ROWS))
    oc_ref[:, CC_CH:] = (gate_ref[...] * scy_ref[...]).astype(BF16)

    u = ccy_ref[...] + ccb_ref[...]
    mu = jnp.mean(u, axis=-1, keepdims=True)
    var = jnp.mean(jnp.square(u - mu), axis=-1, keepdims=True)
    y = (u - mu) * lax.rsqrt(var + EPS) * lng_ref[...] + lnb_ref[...]
    oc_ref[:, :CC_CH] = (y * jax.nn.sigmoid(y)).astype(BF16)

    seq_start = lax.rem(g, tiles_per_seq) == 0
    ccx_ref[pl.ds(0, CC_HALO), :] = jnp.where(seq_start, 0.0, ccx_ref[pl.ds(tm, CC_HALO), :])
    ccx_ref[pl.ds(CC_HALO, tm), :] = ccn_ref[...]
    scx_ref[pl.ds(0, SC_HALO), :] = jnp.where(seq_start, 0.0, scx_ref[pl.ds(tm, SC_HALO), :])
    scx_ref[pl.ds(SC_HALO, tm), :] = scn_ref[...]
    gate_ref[...] = gaten_ref[...]


def _mix_in(x, g, w_in, layer, ccw, ccb, lng, lnb, scw):
    b, t, d = x.shape
    tm = ATT_TILE
    nt = t // tm
    n_tiles = b * nt
    qk_cols = QK_COLS
    v_cols = ATT_HEADS * VT_ROWS
    proj = lambda i: jnp.minimum(i, n_tiles - 1)
    conv = lambda i: jnp.maximum(i - 1, 0)
    row = lambda cols, dt, tile: (
        pl.BlockSpec((None, tm, cols), lambda i: (tile(i) // nt, tile(i) % nt, 0)),
        jax.ShapeDtypeStruct((b, t, cols), dt))
    k_spec, k_shape = row(qk_cols, BF16, proj)
    oc_spec, oc_shape = row(CC_CH + SC_CH, BF16, conv)
    slab = lambda rows: (
        pl.BlockSpec((None, None, rows, tm), lambda i: (proj(i) // nt, proj(i) % nt, 0, 0)),
        jax.ShapeDtypeStruct((b, nt, rows, tm), BF16))
    qt_spec, qt_shape = slab(qk_cols)
    vt_spec, vt_shape = slab(v_cols)
    consts = [ccw, ccb, lng, lnb, scw]
    win = _ResidentWeight(w_in, layer)
    assert win.chunk_rows % LANES == 0
    return pl.pallas_call(
        functools.partial(_mix_in_kernel, tiles_per_seq=nt, w_in=win),
        grid=(n_tiles + 1,),
        in_specs=[pl.BlockSpec((None, tm, d), lambda i: (proj(i) // nt, proj(i) % nt, 0)),
                  _const_spec(g.shape), _HBM]
        + [_const_spec(c.shape) for c in consts],
        out_specs=[qt_spec, k_spec, vt_spec, oc_spec],
        out_shape=[qt_shape, k_shape, vt_shape, oc_shape],
        scratch_shapes=[
            pltpu.VMEM((tm, d), BF16),
            pltpu.VMEM((CC_HALO + tm, CC_CH), F32),
            pltpu.VMEM((SC_HALO + tm, SC_CH), F32),
            pltpu.VMEM((tm, SC_CH), F32),
            pltpu.VMEM((tm, CC_CH), F32),
            pltpu.VMEM((tm, SC_CH), F32),
            pltpu.VMEM((tm, SC_CH), F32),
            pltpu.VMEM((SUBLANES - 1, CC_HALO + tm - SUBLANES, CC_CH), F32),
            pltpu.VMEM((tm, CC_CH), F32),
            pltpu.VMEM((tm, SC_CH), F32),
            pltpu.VMEM((QK_COLS, d), BF16),
            pltpu.VMEM((V_COLS, d), BF16),
            win.scratch(),
        ],
        compiler_params=_params(1),
        name="mix_in",
    )(x, g, w_in, *consts)


def _diff_attn_kernel(lam_ref, subln_ref, qt_ref, k_ref, vt_ref, o_ref,
                      qs_ref, bias_ref, m_ref, acc_ref, *score_refs, lam_init):
    s_refs, cmax_refs = score_refs[:ATT_SCORE_BUFS], score_refs[ATT_SCORE_BUFS:]
    nt, _, tq = qt_ref.shape
    tk = tq
    head = pl.program_id(1)
    step = 8 // ATT_HEADS
    assert step * ATT_HEADS == 8
    slope = jnp.left_shift(1, step * (ATT_HEADS - 1 - head)).astype(F32) * (LOG2E / 2 ** 8)

    jj = lax.broadcasted_iota(jnp.int32, (tk, 2 * tq), 0)
    ii = lax.broadcasted_iota(jnp.int32, (tk, 2 * tq), 1)
    ii = jnp.where(ii >= tq, ii - tq, ii)
    bias = jj.astype(F32) * slope
    bias_ref[0] = bias
    bias_ref[1] = jnp.where(jj <= ii, bias, MASK_VALUE)

    for i in range(nt):
        qt = qt_ref[i]
        row = lax.broadcasted_iota(jnp.int32, qt.shape, 0)
        zero = jnp.zeros_like(qt)
        qs_ref[i, :, pl.ds(0, tq)] = jnp.where(row < ATT_DK, qt, zero)
        qs_ref[i, :, pl.ds(tq, tq)] = jnp.where(row >= ATT_DK, qt, zero)

    m_ref[...] = jnp.full(m_ref.shape, MASK_VALUE, F32)
    acc_ref[...] = jnp.zeros(acc_ref.shape, F32)

    def scores(item, s_ref, cmax_ref):
        qi, c = item
        j0 = pl.multiple_of(c * tk, tk)
        s = jnp.dot(k_ref[pl.ds(j0, tk), :], qs_ref[qi], preferred_element_type=F32)
        s = s + bias_ref[(c == qi).astype(jnp.int32)]
        s_ref[...] = s
        cmax_ref[...] = jnp.max(s, axis=0, keepdims=True)

    def accumulate(item, s_ref, cmax_ref):
        qi, c = item
        shift = slope * (c * tk).astype(F32)
        vtc = vt_ref[c]
        for nb in range(2 * tq // ATT_LANES):
            cols = pl.ds(nb * ATT_LANES, ATT_LANES)
            m_prev = m_ref[qi, :, cols] - shift
            m_new = jnp.maximum(m_prev, cmax_ref[:, cols])
            alpha = jnp.exp2(m_prev - m_new)
            p = jnp.exp2(s_ref[:, cols] - m_new).astype(BF16)
            m_ref[qi, :, cols] = m_new + shift
            pv = jnp.dot(vtc, p, preferred_element_type=F32)
            acc_ref[qi, :, cols] = alpha * acc_ref[qi, :, cols] + pv

    def following(item):
        qi, c = item
        last = c == qi
        return qi + last.astype(jnp.int32), jnp.where(last, 0, c + 1)

    n_items = nt * (nt + 1) // 2
    n_bufs = len(s_refs)
    assert n_items % n_bufs == 0
    bufs = list(zip(s_refs, cmax_refs))

    def following_clamped(item):
        qi, c = following(item)
        past_end = qi == nt
        return jnp.where(past_end, 0, qi), jnp.where(past_end, 0, c)

    ahead = [(jnp.int32(0), jnp.int32(0))]
    for _ in range(n_bufs - 2):
        ahead.append(following_clamped(ahead[-1]))
    for item, buf in zip(ahead, bufs):
        scores(item, *buf)

    def trip(_, carry):
        items = list(carry)
        for slot in range(n_bufs):
            nxt = following_clamped(items[-1])
            scores(nxt, *bufs[(slot + n_bufs - 1) % n_bufs])
            accumulate(items[0], *bufs[slot])
            items = items[1:] + [nxt]
        return tuple(items)

    lax.fori_loop(0, n_items // n_bufs, trip, tuple(ahead))

    lam_p = lam_ref[...]
    lam = (jnp.exp(jnp.sum(lam_p[0:1] * lam_p[1:2], axis=-1, keepdims=True))
           - jnp.exp(jnp.sum(lam_p[2:3] * lam_p[3:4], axis=-1, keepdims=True)) + lam_init)
    for i in range(nt):
        acc = acc_ref[i, pl.ds(0, ATT_DV), :]
        inv_l = 1.0 / acc_ref[i, pl.ds(ATT_DV, 1), :]
        o = acc[:, :tq] * inv_l[:, :tq] - lam * (acc[:, tq:] * inv_l[:, tq:])
        inv_rms = lax.rsqrt(jnp.mean(o * o, axis=0, keepdims=True) + EPS)
        o_ref[i] = (o * inv_rms * subln_ref[...] * (1.0 - lam_init)).astype(BF16)


def _diff_attn(lam_p, subln, qt, k, vt, lam_init):
    b, t, _ = k.shape
    tq = ATT_TILE
    nt = t // tq
    per_head = lambda rows: pl.BlockSpec((None, nt, rows, tq), lambda i, h: (i, 0, h, 0))
    return pl.pallas_call(
        functools.partial(_diff_attn_kernel, lam_init=lam_init),
        grid=(b, ATT_HEADS),
        in_specs=[
            _const_spec(lam_p.shape),
            _const_spec(subln.shape),
            per_head(2 * ATT_DK),
            pl.BlockSpec((None, t, 2 * ATT_DK), lambda i, h: (i, 0, h)),
            per_head(VT_ROWS),
        ],
        out_specs=per_head(ATT_DV),
        out_shape=jax.ShapeDtypeStruct((b, nt, ATT_HEADS * ATT_DV, tq), BF16),
        scratch_shapes=[
            pltpu.VMEM((nt, 2 * ATT_DK, 2 * tq), BF16),
            pltpu.VMEM((2, tq, 2 * tq), F32),
            pltpu.VMEM((nt, 1, 2 * tq), F32),
            pltpu.VMEM((nt, VT_ROWS, 2 * tq), F32),
        ]
        + [pltpu.VMEM((tq, 2 * tq), F32)] * ATT_SCORE_BUFS
        + [pltpu.VMEM((1, 2 * tq), F32)] * ATT_SCORE_BUFS,
        compiler_params=_params(2),
        name="diff_attn",
    )(lam_p, subln, qt, k, vt)


_TN = (((0,), (0,)), ((), ()))


def _mix_out_kernel(x_ref, oat_ref, oc_ref, mem_ref, mg_ref, g_ref, wo_hbm, wq_hbm, wxo_hbm, wkv_hbm,
                    o_ref, kv_ref, *scratch, weights):
    wo_ref, wq_ref, wxo_ref, wkv_ref = scratch

    @pl.when((pl.program_id(0) == 0) & (pl.program_id(1) == 0))
    def _():
        for w, hbm, resident in zip(weights, (wo_hbm, wq_hbm, wxo_hbm, wkv_hbm), scratch):
            w.load(hbm, resident)

    @pl.when(pl.program_id(1) == 0)
    def _():
        hm = _rms(mem_ref[...], mg_ref[...]).astype(BF16)
        kv_ref[...] = jnp.dot(hm, wkv_ref[...], preferred_element_type=F32).astype(BF16)

    n_slabs, n_att, _ = oat_ref.shape
    wo_att = wo_ref[pl.ds(0, n_att), :]
    att = [lax.dot_general(oat_ref[s], wo_att, _TN, preferred_element_type=F32)
           for s in range(n_slabs)]
    x = x_ref[...] + jnp.concatenate(att, axis=0)
    x = x + jnp.dot(oc_ref[...], wo_ref[pl.ds(n_att, oc_ref.shape[1]), :],
                    preferred_element_type=F32)

    hq = jnp.dot(_rms(x, g_ref[...]).astype(BF16), wq_ref[...], preferred_element_type=F32)
    hq = (hq * (XA_HD ** -0.5)).astype(BF16)
    kd = XA_HEADS * XA_HD
    heads = []
    for h in range(XA_HEADS):
        sl = slice(h * XA_HD, (h + 1) * XA_HD)
        s = lax.dot_general(hq[:, sl], kv_ref[:, sl], _NT, preferred_element_type=F32)
        p = jnp.exp(s - jnp.max(s, axis=-1, keepdims=True))
        inv = 1.0 / jnp.sum(p, axis=-1, keepdims=True)
        o = jnp.dot(p.astype(BF16), kv_ref[:, kd + h * XA_HD:kd + (h + 1) * XA_HD],
                    preferred_element_type=F32)
        heads.append((o * inv).astype(BF16))
    o = jnp.concatenate(heads, axis=-1)
    o_ref[...] = x + jnp.dot(o, wxo_ref[...], preferred_element_type=F32)


def _mix_out(x, oat, oc, mem, mem_g, g, wo, wq, wxo, wkv, layer):
    b, t, d = x.shape
    tq = oat.shape[3]
    n_slabs = MIX_OUT_TILE // tq
    tm = MIX_OUT_TILE
    m = mem.shape[1]
    row = lambda cols: pl.BlockSpec((None, tm, cols), lambda i, j: (i, j, 0))
    weights = [_ResidentWeight(w, layer) for w in (wo, wq, wxo, wkv)]
    return pl.pallas_call(
        functools.partial(_mix_out_kernel, weights=weights),
        grid=(b, t // tm),
        in_specs=[row(d),
                  pl.BlockSpec((None, n_slabs, oat.shape[2], tq), lambda i, j: (i, j, 0, 0)),
                  row(oc.shape[2]),
                  pl.BlockSpec((None, m, d), lambda i, j: (i, 0, 0)),
                  _const_spec(mem_g.shape), _const_spec(g.shape), _HBM, _HBM, _HBM, _HBM],
        out_specs=row(d),
        out_shape=jax.ShapeDtypeStruct((b, t, d), F32),
        scratch_shapes=[pltpu.VMEM((m, wkv.shape[2]), BF16)]
        + [w.scratch() for w in weights],
        compiler_params=_params(2),
        name="mix_out",
    )(x, oat, oc, mem, mem_g, g, wo, wq, wxo, wkv)


def kernel(x, mem, ffn1_norm, ffn1_w_gate, ffn1_w_up, ffn1_w_down, mix_norm, w_in, lam_q1, lam_k1, lam_q2, lam_k2, diff_subln, cc_dw, cc_dw_b, cc_ln_g, cc_ln_b, sc_dw, w_out, xa_norm, mem_norm, xa_wq, xa_wkv, xa_wo, ffn2_norm, ffn2_w_gate, ffn2_w_up, ffn2_w_down, final_norm):
    b, t, d = x.shape
    vec = lambda v: v.reshape(1, -1).astype(F32)

    for l in range(DEPTH):
        x = _ffn(x.reshape(b * t, d), vec(ffn1_norm[l]), ffn1_w_gate, ffn1_w_up, ffn1_w_down,
                 l).reshape(b, t, d)

        qt, k, vt, oc = _mix_in(
            x, vec(mix_norm[l]), w_in, l,
            cc_dw[l].astype(F32), vec(cc_dw_b[l]), vec(cc_ln_g[l]), vec(cc_ln_b[l]),
            sc_dw[l].astype(F32))

        lam_init = 0.8 - 0.6 * math.exp(-0.3 * l)
        lam_p = jnp.stack([lam_q1[l], lam_k1[l], lam_q2[l], lam_k2[l]]).astype(F32)
        subln = jnp.broadcast_to(diff_subln[l].astype(F32)[:, None], (ATT_DV, ATT_TILE))
        oat = _diff_attn(lam_p, subln, qt, k, vt, lam_init)

        x = _mix_out(x, oat, oc, mem, vec(mem_norm[l]), vec(xa_norm[l]),
                     w_out, xa_wq, xa_wo, xa_wkv, l)

        last = l == DEPTH - 1
        x = _ffn(x.reshape(b * t, d), vec(ffn2_norm[l]), ffn2_w_gate, ffn2_w_up, ffn2_w_down, l,
                 vec(final_norm) if last else None).reshape(b, t, d)
    return x
```

```python
import functools
import math

import jax
import jax.numpy as jnp
from jax import lax
from jax.experimental import pallas as pl
from jax.experimental.pallas import tpu as pltpu

DEPTH = 2
ATT_HEADS = 4
ATT_DV = 128
ATT_DK = 64
CC_CH = 256
CC_K = 31
SC_CH = 256
SC_K = 3
XA_HEADS = 4
XA_HD = 256
EPS = 1e-6

F32 = jnp.float32
BF16 = jnp.bfloat16

QK_COLS = ATT_HEADS * 2 * ATT_DK
V_COLS = ATT_HEADS * ATT_DV
K_COL = QK_COLS
V_COL = 2 * QK_COLS
CC_COL = V_COL + V_COLS
SC_COL = CC_COL + 2 * CC_CH

TOKEN_TILE = 1024
FFN_ROWS = 512
MIX_OUT_TILE = 1024
ATT_TILE = 512
VT_ROWS = ATT_DV + 2 * 8
LOG2E = math.log2(math.e)
ATT_SCORE_BUFS = 4
ATT_LANES = 256
WEIGHT_CHUNK_BYTES = 3 * 512 * 1024
WEIGHT_SLOTS = 6
LANES = 128
SUBLANES = 8
CONV_ROWS = 32
CC_HALO = 32
SC_HALO = 8
VMEM_LIMIT = 56 * 1024 * 1024
MASK_VALUE = -1e30

_NT = (((1,), (1,)), ((), ()))


def _rms(x, g):
    return x * lax.rsqrt(jnp.mean(x * x, axis=-1, keepdims=True) + EPS) * g


def _const_spec(shape):
    nd = len(shape)
    return pl.BlockSpec(shape, lambda *_: (0,) * nd, pipeline_mode=pl.Buffered(1))


_HBM = pl.BlockSpec(memory_space=pl.ANY)


class _ResidentWeight:
    def __init__(self, stacked, layer):
        _, self.rows, self.cols = stacked.shape
        self.layer = layer
        self.chunk_rows = max(r for r in range(2 * SUBLANES, self.rows + 1, 2 * SUBLANES)
                              if self.rows % r == 0 and r * self.cols * 4 <= WEIGHT_CHUNK_BYTES)

    def scratch(self):
        return pltpu.VMEM((self.rows, self.cols), BF16)

    def load(self, hbm_ref, resident_ref, on_chunk=None):
        rc = self.chunk_rows
        n_chunks = self.rows // rc

        def stream(stage_ref, sem_ref):
            def copy(c):
                slot = c % WEIGHT_SLOTS
                return pltpu.make_async_copy(hbm_ref.at[self.layer, pl.ds(c * rc, rc), :],
                                             stage_ref.at[slot], sem_ref.at[slot])

            for c in range(min(WEIGHT_SLOTS - 1, n_chunks)):
                copy(c).start()
            for c in range(n_chunks):
                if c + WEIGHT_SLOTS - 1 < n_chunks:
                    copy(c + WEIGHT_SLOTS - 1).start()
                copy(c).wait()
                chunk = stage_ref[c % WEIGHT_SLOTS]
                resident_ref[pl.ds(c * rc, rc), :] = chunk.astype(BF16)
                if on_chunk is not None:
                    on_chunk(c * rc, chunk)

        pl.run_scoped(stream, pltpu.VMEM((WEIGHT_SLOTS, rc, self.cols), F32),
                      pltpu.SemaphoreType.DMA((WEIGHT_SLOTS,)))


def _params(n_axes):
    return pltpu.CompilerParams(
        dimension_semantics=("arbitrary",) * n_axes, vmem_limit_bytes=VMEM_LIMIT)


def _fold_bits(v):
    bits = pltpu.bitcast(v, jnp.uint32)
    folded = None
    for i in range(0, bits.shape[0], SUBLANES):
        for c in range(0, bits.shape[1], LANES):
            tile = bits[i:i + SUBLANES, c:c + LANES]
            folded = tile if folded is None else folded | tile
    return folded


def _order_after(dst_ref, bits):
    half = jnp.uint32(16)
    zero = pltpu.bitcast(lax.shift_right_logical(lax.shift_right_logical(bits, half), half), F32)
    zero = jnp.concatenate([zero, zero], axis=0).astype(dst_ref.dtype)
    dst_ref[...] = dst_ref[...] + zero


def _ffn_kernel(x_ref, g_ref, wg_hbm, wu_hbm, wd_hbm, *rest, final, weights):
    if final:
        fg_ref, o_ref, wg_ref, wu_ref, wd_ref = rest
    else:
        o_ref, wg_ref, wu_ref, wd_ref = rest

    @pl.when(pl.program_id(0) == 0)
    def _():
        for w, hbm, resident in zip(weights, (wg_hbm, wu_hbm, wd_hbm), (wg_ref, wu_ref, wd_ref)):
            w.load(hbm, resident)

    for r0 in range(0, x_ref.shape[0], FFN_ROWS):
        rows = pl.ds(r0, FFN_ROWS)
        x = x_ref[rows, :]
        h = _rms(x, g_ref[...]).astype(BF16)
        gate = jnp.dot(h, wg_ref[...], preferred_element_type=F32)
        up = jnp.dot(h, wu_ref[...], preferred_element_type=F32)
        act = (gate * jax.nn.sigmoid(gate) * up).astype(BF16)
        y = x + 0.5 * jnp.dot(act, wd_ref[...], preferred_element_type=F32)
        if final:
            y = _rms(y, fg_ref[...])
        o_ref[rows, :] = y


def _ffn(x, g, wg, wu, wd, layer, final_g=None):
    n, d = x.shape
    tm = TOKEN_TILE
    final = final_g is not None
    row_spec = pl.BlockSpec((tm, d), lambda i: (i, 0))
    weights = [_ResidentWeight(w, layer) for w in (wg, wu, wd)]
    in_specs = [row_spec, _const_spec((1, d)), _HBM, _HBM, _HBM]
    args = [x, g, wg, wu, wd]
    if final:
        in_specs.append(_const_spec((1, d)))
        args.append(final_g)
    return pl.pallas_call(
        functools.partial(_ffn_kernel, final=final, weights=weights),
        grid=(n // tm,),
        in_specs=in_specs,
        out_specs=row_spec,
        out_shape=jax.ShapeDtypeStruct((n, d), F32),
        scratch_shapes=[w.scratch() for w in weights],
        compiler_params=_params(1),
        name="ffn_final" if final else "ffn",
    )(*args)


def _shifted_copies(ext_ref, shift_ref):
    rows = shift_ref.shape[1]
    for b in range(1, SUBLANES):
        shift_ref[b - 1] = ext_ref[pl.ds(b, rows), :]


def _dw_conv(ext_ref, w_ref, out_ref, *, taps, halo, blocks, shift_ref=None):
    base = halo - (taps - 1)
    folded = None
    for r in blocks:
        acc = None
        for j in range(taps):
            off = base + j
            if shift_ref is None or off % SUBLANES == 0:
                rows = ext_ref[pl.ds(off + r * CONV_ROWS, CONV_ROWS), :]
            else:
                rows = shift_ref[off % SUBLANES - 1,
                                 pl.ds(off - off % SUBLANES + r * CONV_ROWS, CONV_ROWS), :]
            term = rows * w_ref[pl.ds(j, 1), :]
            acc = term if acc is None else acc + term
        out_ref[pl.ds(r * CONV_ROWS, CONV_ROWS), :] = acc
        folded = _fold_bits(acc) if folded is None else folded | _fold_bits(acc)
    return folded


CC_BLOCK_SPLIT = ((0, 1), (2, 3), (4, 5, 6, 7), (8, 9, 10, 11), (12, 13, 14, 15))


def _mix_in_kernel(x_ref, g_ref, win_hbm,
                   ccw_ref, ccb_ref, lng_ref, lnb_ref, scw_ref,
                   qt_ref, k_ref, vt_ref, oc_ref,
                   h_ref, ccx_ref, scx_ref, gate_ref, ccn_ref, scn_ref, gaten_ref, ccs_ref,
                   ccy_ref, scy_ref, wqt_ref, wvt_ref, win_ref,
                   *, tiles_per_seq, w_in):
    tm = x_ref.shape[0]
    g = pl.program_id(0)
    cc_conv = functools.partial(_dw_conv, ccx_ref, ccw_ref, ccy_ref, taps=CC_K, halo=CC_HALO,
                                shift_ref=ccs_ref)

    def cc_conv_then_next_projection(i):
        done = cc_conv(blocks=CC_BLOCK_SPLIT[i])
        _order_after(h_ref.at[pl.ds(0, 2 * SUBLANES), pl.ds(0, LANES)], done)

    @pl.when(g == 0)
    def _():
        ccx_ref[...] = jnp.zeros(ccx_ref.shape, F32)
        scx_ref[...] = jnp.zeros(scx_ref.shape, F32)
        gate_ref[...] = jnp.zeros(gate_ref.shape, F32)

        def transposed_parts(row0, chunk):
            rows = pl.ds(row0, chunk.shape[0])
            wqt_ref[:, rows] = chunk[:, :QK_COLS].T.astype(BF16)
            wvt_ref[:, rows] = chunk[:, V_COL:V_COL + V_COLS].T.astype(BF16)

        w_in.load(win_hbm, win_ref, on_chunk=transposed_parts)

    h_ref[...] = _rms(x_ref[...], g_ref[...]).astype(BF16)
    _shifted_copies(ccx_ref, ccs_ref)

    qt = lax.dot_general(wqt_ref[...], h_ref[...], _NT, preferred_element_type=F32)
    qt_ref[...] = (qt * (LOG2E * ATT_DK ** -0.5)).astype(BF16)
    cc_conv_then_next_projection(0)

    k_ref[...] = jnp.dot(h_ref[...], win_ref[:, pl.ds(K_COL, QK_COLS)],
                         preferred_element_type=F32).astype(BF16)
    cc_conv_then_next_projection(1)

    vt = lax.dot_general(wvt_ref[...], h_ref[...], _NT, preferred_element_type=F32).astype(BF16)
    for hd in range(ATT_HEADS):
        vt_ref[pl.ds(hd * VT_ROWS, ATT_DV), :] = vt[hd * ATT_DV:(hd + 1) * ATT_DV]
        vt_ref[pl.ds(hd * VT_ROWS + ATT_DV, VT_ROWS - ATT_DV), :] = jnp.ones(
            (VT_ROWS - ATT_DV, tm), BF16)
    cc_conv_then_next_projection(2)

    zc = jnp.dot(h_ref[...], win_ref[:, pl.ds(CC_COL, 2 * CC_CH)],
                 preferred_element_type=F32)
    ccn_ref[...] = zc[:, :CC_CH] * jax.nn.sigmoid(zc[:, CC_CH:])
    cc_conv_then_next_projection(3)

    zs = jnp.dot(h_ref[...], win_ref[:, pl.ds(SC_COL, 3 * SC_CH)],
                 preferred_element_type=F32)
    scn_ref[...] = zs[:, SC_CH:2 * SC_CH] * zs[:, 2 * SC_CH:]
    gaten_ref[...] = zs[:, :SC_CH]
    cc_conv(blocks=CC_BLOCK_SPLIT[4])
    _dw_conv(scx_ref, scw_ref, scy_ref, taps=SC_K, halo=SC_HALO, blocks=range(tm // CONV_ROWS))
    oc_ref[:, CC_CH:] = (gate_ref[...] * scy_ref[...]).astype(BF16)

    u = ccy_ref[...] + ccb_ref[...]
    mu = jnp.mean(u, axis=-1, keepdims=True)
    var = jnp.mean(jnp.square(u - mu), axis=-1, keepdims=True)
    y = (u - mu) * lax.rsqrt(var + EPS) * lng_ref[...] + lnb_ref[...]
    oc_ref[:, :CC_CH] = (y * jax.nn.sigmoid(y)).astype(BF16)

    seq_start = lax.rem(g, tiles_per_seq) == 0
    ccx_ref[pl.ds(0, CC_HALO), :] = jnp.where(seq_start, 0.0, ccx_ref[pl.ds(tm, CC_HALO), :])
    ccx_ref[pl.ds(CC_HALO, tm), :] = ccn_ref[...]
    scx_ref[pl.ds(0, SC_HALO), :] = jnp.where(seq_start, 0.0, scx_ref[pl.ds(tm, SC_HALO), :])
    scx_ref[pl.ds(SC_HALO, tm), :] = scn_ref[...]
    gate_ref[...] = gaten_ref[...]


def _mix_in(x, g, w_in, layer, ccw, ccb, lng, lnb, scw):
    b, t, d = x.shape
    tm = ATT_TILE
    nt = t // tm
    n_tiles = b * nt
    qk_cols = QK_COLS
    v_cols = ATT_HEADS * VT_ROWS
    proj = lambda i: jnp.minimum(i, n_tiles - 1)
    conv = lambda i: jnp.maximum(i - 1, 0)
    row = lambda cols, dt, tile: (
        pl.BlockSpec((None, tm, cols), lambda i: (tile(i) // nt, tile(i) % nt, 0)),
        jax.ShapeDtypeStruct((b, t, cols), dt))
    k_spec, k_shape = row(qk_cols, BF16, proj)
    oc_spec, oc_shape = row(CC_CH + SC_CH, BF16, conv)
    slab = lambda rows: (
        pl.BlockSpec((None, None, rows, tm), lambda i: (proj(i) // nt, proj(i) % nt, 0, 0)),
        jax.ShapeDtypeStruct((b, nt, rows, tm), BF16))
    qt_spec, qt_shape = slab(qk_cols)
    vt_spec, vt_shape = slab(v_cols)
    consts = [ccw, ccb, lng, lnb, scw]
    win = _ResidentWeight(w_in, layer)
    assert win.chunk_rows % LANES == 0
    return pl.pallas_call(
        functools.partial(_mix_in_kernel, tiles_per_seq=nt, w_in=win),
        grid=(n_tiles + 1,),
        in_specs=[pl.BlockSpec((None, tm, d), lambda i: (proj(i) // nt, proj(i) % nt, 0)),
                  _const_spec(g.shape), _HBM]
        + [_const_spec(c.shape) for c in consts],
        out_specs=[qt_spec, k_spec, vt_spec, oc_spec],
        out_shape=[qt_shape, k_shape, vt_shape, oc_shape],
        scratch_shapes=[
            pltpu.VMEM((tm, d), BF16),
            pltpu.VMEM((CC_HALO + tm, CC_CH), F32),
            pltpu.VMEM((SC_HALO + tm, SC_CH), F32),
            pltpu.VMEM((tm, SC_CH), F32),
            pltpu.VMEM((tm, CC_CH), F32),
            pltpu.VMEM((tm, SC_CH), F32),
            pltpu.VMEM((tm, SC_CH), F32),
            pltpu.VMEM((SUBLANES - 1, CC_HALO + tm - SUBLANES, CC_CH), F32),
            pltpu.VMEM((tm, CC_CH), F32),
            pltpu.VMEM((tm, SC_CH), F32),
            pltpu.VMEM((QK_COLS, d), BF16),
            pltpu.VMEM((V_COLS, d), BF16),
            win.scratch(),
        ],
        compiler_params=_params(1),
        name="mix_in",
    )(x, g, w_in, *consts)


def _diff_attn_kernel(lam_ref, subln_ref, qt_ref, k_ref, vt_ref, o_ref,
                      qs_ref, bias_ref, m_ref, acc_ref, *score_refs, lam_init):
    s_refs, cmax_refs = score_refs[:ATT_SCORE_BUFS], score_refs[ATT_SCORE_BUFS:]
    nt, _, tq = qt_ref.shape
    tk = tq
    head = pl.program_id(1)
    step = 8 // ATT_HEADS
    assert step * ATT_HEADS == 8
    slope = jnp.left_shift(1, step * (ATT_HEADS - 1 - head)).astype(F32) * (LOG2E / 2 ** 8)

    jj = lax.broadcasted_iota(jnp.int32, (tk, 2 * tq), 0)
    ii = lax.broadcasted_iota(jnp.int32, (tk, 2 * tq), 1)
    ii = jnp.where(ii >= tq, ii - tq, ii)
    bias = jj.astype(F32) * slope
    bias_ref[0] = bias
    bias_ref[1] = jnp.where(jj <= ii, bias, MASK_VALUE)

    for i in range(nt):
        qt = qt_ref[i]
        row = lax.broadcasted_iota(jnp.int32, qt.shape, 0)
        zero = jnp.zeros_like(qt)
        qs_ref[i, :, pl.ds(0, tq)] = jnp.where(row < ATT_DK, qt, zero)
        qs_ref[i, :, pl.ds(tq, tq)] = jnp.where(row >= ATT_DK, qt, zero)

    m_ref[...] = jnp.full(m_ref.shape, MASK_VALUE, F32)
    acc_ref[...] = jnp.zeros(acc_ref.shape, F32)

    def scores(item, s_ref, cmax_ref):
        qi, c = item
        j0 = pl.multiple_of(c * tk, tk)
        s = jnp.dot(k_ref[pl.ds(j0, tk), :], qs_ref[qi], preferred_element_type=F32)
        s = s + bias_ref[(c == qi).astype(jnp.int32)]
        s_ref[...] = s
        cmax_ref[...] = jnp.max(s, axis=0, keepdims=True)

    def accumulate(item, s_ref, cmax_ref):
        qi, c = item
        shift = slope * (c * tk).astype(F32)
        vtc = vt_ref[c]
        for nb in range(2 * tq // ATT_LANES):
            cols = pl.ds(nb * ATT_LANES, ATT_LANES)
            m_prev = m_ref[qi, :, cols] - shift
            m_new = jnp.maximum(m_prev, cmax_ref[:, cols])
            alpha = jnp.exp2(m_prev - m_new)
            p = jnp.exp2(s_ref[:, cols] - m_new).astype(BF16)
            m_ref[qi, :, cols] = m_new + shift
            pv = jnp.dot(vtc, p, preferred_element_type=F32)
            acc_ref[qi, :, cols] = alpha * acc_ref[qi, :, cols] + pv

    def following(item):
        qi, c = item
        last = c == qi
        return qi + last.astype(jnp.int32), jnp.where(last, 0, c + 1)

    n_items = nt * (nt + 1) // 2
    n_bufs = len(s_refs)
    assert n_items % n_bufs == 0
    bufs = list(zip(s_refs, cmax_refs))

    def following_clamped(item):
        qi, c = following(item)
        past_end = qi == nt
        return jnp.where(past_end, 0, qi), jnp.where(past_end, 0, c)

    ahead = [(jnp.int32(0), jnp.int32(0))]
    for _ in range(n_bufs - 2):
        ahead.append(following_clamped(ahead[-1]))
    for item, buf in zip(ahead, bufs):
        scores(item, *buf)

    def trip(_, carry):
        items = list(carry)
        for slot in range(n_bufs):
            nxt = following_clamped(items[-1])
            scores(nxt, *bufs[(slot + n_bufs - 1) % n_bufs])
            accumulate(items[0], *bufs[slot])
            items = items[1:] + [nxt]
        return tuple(items)

    lax.fori_loop(0, n_items // n_bufs, trip, tuple(ahead))

    lam_p = lam_ref[...]
    lam = (jnp.exp(jnp.sum(lam_p[0:1] * lam_p[1:2], axis=-1, keepdims=True))
           - jnp.exp(jnp.sum(lam_p[2:3] * lam_p[3:4], axis=-1, keepdims=True)) + lam_init)
    for i in range(nt):
        acc = acc_ref[i, pl.ds(0, ATT_DV), :]
        inv_l = 1.0 / acc_ref[i, pl.ds(ATT_DV, 1), :]
        o = acc[:, :tq] * inv_l[:, :tq] - lam * (acc[:, tq:] * inv_l[:, tq:])
        inv_rms = lax.rsqrt(jnp.mean(o * o, axis=0, keepdims=True) + EPS)
        o_ref[i] = (o * inv_rms * subln_ref[...] * (1.0 - lam_init)).astype(BF16)


def _diff_attn(lam_p, subln, qt, k, vt, lam_init):
    b, t, _ = k.shape
    tq = ATT_TILE
    nt = t // tq
    per_head = lambda rows: pl.BlockSpec((None, nt, rows, tq), lambda i, h: (i, 0, h, 0))
    return pl.pallas_call(
        functools.partial(_diff_attn_kernel, lam_init=lam_init),
        grid=(b, ATT_HEADS),
        in_specs=[
            _const_spec(lam_p.shape),
            _const_spec(subln.shape),
            per_head(2 * ATT_DK),
            pl.BlockSpec((None, t, 2 * ATT_DK), lambda i, h: (i, 0, h)),
            per_head(VT_ROWS),
        ],
        out_specs=per_head(ATT_DV),
        out_shape=jax.ShapeDtypeStruct((b, nt, ATT_HEADS * ATT_DV, tq), BF16),
        scratch_shapes=[
            pltpu.VMEM((nt, 2 * ATT_DK, 2 * tq), BF16),
            pltpu.VMEM((2, tq, 2 * tq), F32),
            pltpu.VMEM((nt, 1, 2 * tq), F32),
            pltpu.VMEM((nt, VT_ROWS, 2 * tq), F32),
        ]
        + [pltpu.VMEM((tq, 2 * tq), F32)] * ATT_SCORE_BUFS
        + [pltpu.VMEM((1, 2 * tq), F32)] * ATT_SCORE_BUFS,
        compiler_params=_params(2),
        name="diff_attn",
    )(lam_p, subln, qt, k, vt)


_TN = (((0,), (0,)), ((), ()))


def _mix_out_kernel(x_ref, oat_ref, oc_ref, mem_ref, mg_ref, g_ref, wo_hbm, wq_hbm, wxo_hbm, wkv_hbm,
                    o_ref, kv_ref, *scratch, weights):
    wo_ref, wq_ref, wxo_ref, wkv_ref = scratch

    @pl.when((pl.program_id(0) == 0) & (pl.program_id(1) == 0))
    def _():
        for w, hbm, resident in zip(weights, (wo_hbm, wq_hbm, wxo_hbm, wkv_hbm), scratch):
            w.load(hbm, resident)

    @pl.when(pl.program_id(1) == 0)
    def _():
        hm = _rms(mem_ref[...], mg_ref[...]).astype(BF16)
        kv_ref[...] = jnp.dot(hm, wkv_ref[...], preferred_element_type=F32).astype(BF16)

    n_slabs, n_att, _ = oat_ref.shape
    wo_att = wo_ref[pl.ds(0, n_att), :]
    att = [lax.dot_general(oat_ref[s], wo_att, _TN, preferred_element_type=F32)
           for s in range(n_slabs)]
    x = x_ref[...] + jnp.concatenate(att, axis=0)
    x = x + jnp.dot(oc_ref[...], wo_ref[pl.ds(n_att, oc_ref.shape[1]), :],
                    preferred_element_type=F32)

    hq = jnp.dot(_rms(x, g_ref[...]).astype(BF16), wq_ref[...], preferred_element_type=F32)
    hq = (hq * (XA_HD ** -0.5)).astype(BF16)
    kd = XA_HEADS * XA_HD
    heads = []
    for h in range(XA_HEADS):
        sl = slice(h * XA_HD, (h + 1) * XA_HD)
        s = lax.dot_general(hq[:, sl], kv_ref[:, sl], _NT, preferred_element_type=F32)
        p = jnp.exp(s - jnp.max(s, axis=-1, keepdims=True))
        inv = 1.0 / jnp.sum(p, axis=-1, keepdims=True)
        o = jnp.dot(p.astype(BF16), kv_ref[:, kd + h * XA_HD:kd + (h + 1) * XA_HD],
                    preferred_element_type=F32)
        heads.append((o * inv).astype(BF16))
    o = jnp.concatenate(heads, axis=-1)
    o_ref[...] = x + jnp.dot(o, wxo_ref[...], preferred_element_type=F32)


def _mix_out(x, oat, oc, mem, mem_g, g, wo, wq, wxo, wkv, layer):
    b, t, d = x.shape
    tq = oat.shape[3]
    n_slabs = MIX_OUT_TILE // tq
    tm = MIX_OUT_TILE
    m = mem.shape[1]
    row = lambda cols: pl.BlockSpec((None, tm, cols), lambda i, j: (i, j, 0))
    weights = [_ResidentWeight(w, layer) for w in (wo, wq, wxo, wkv)]
    return pl.pallas_call(
        functools.partial(_mix_out_kernel, weights=weights),
        grid=(b, t // tm),
        in_specs=[row(d),
                  pl.BlockSpec((None, n_slabs, oat.shape[2], tq), lambda i, j: (i, j, 0, 0)),
                  row(oc.shape[2]),
                  pl.BlockSpec((None, m, d), lambda i, j: (i, 0, 0)),
                  _const_spec(mem_g.shape), _const_spec(g.shape), _HBM, _HBM, _HBM, _HBM],
        out_specs=row(d),
        out_shape=jax.ShapeDtypeStruct((b, t, d), F32),
        scratch_shapes=[pltpu.VMEM((m, wkv.shape[2]), BF16)]
        + [w.scratch() for w in weights],
        compiler_params=_params(2),
        name="mix_out",
    )(x, oat, oc, mem, mem_g, g, wo, wq, wxo, wkv)


def kernel(x, mem, ffn1_norm, ffn1_w_gate, ffn1_w_up, ffn1_w_down, mix_norm, w_in, lam_q1, lam_k1, lam_q2, lam_k2, diff_subln, cc_dw, cc_dw_b, cc_ln_g, cc_ln_b, sc_dw, w_out, xa_norm, mem_norm, xa_wq, xa_wkv, xa_wo, ffn2_norm, ffn2_w_gate, ffn2_w_up, ffn2_w_down, final_norm):
    b, t, d = x.shape
    vec = lambda v: v.reshape(1, -1).astype(F32)

    for l in range(DEPTH):
        x = _ffn(x.reshape(b * t, d), vec(ffn1_norm[l]), ffn1_w_gate, ffn1_w_up, ffn1_w_down,
                 l).reshape(b, t, d)

        qt, k, vt, oc = _mix_in(
            x, vec(mix_norm[l]), w_in, l,
            cc_dw[l].astype(F32), vec(cc_dw_b[l]), vec(cc_ln_g[l]), vec(cc_ln_b[l]),
            sc_dw[l].astype(F32))

        lam_init = 0.8 - 0.6 * math.exp(-0.3 * l)
        lam_p = jnp.stack([lam_q1[l], lam_k1[l], lam_q2[l], lam_k2[l]]).astype(F32)
        subln = jnp.broadcast_to(diff_subln[l].astype(F32)[:, None], (ATT_DV, ATT_TILE))
        oat = _diff_attn(lam_p, subln, qt, k, vt, lam_init)

        x = _mix_out(x, oat, oc, mem, vec(mem_norm[l]), vec(xa_norm[l]),
                     w_out, xa_wq, xa_wo, xa_wkv, l)

        last = l == DEPTH - 1
        x = _ffn(x.reshape(b * t, d), vec(ffn2_norm[l]), ffn2_w_gate, ffn2_w_up, ffn2_w_down, l,
                 vec(final_norm) if last else None).reshape(b, t, d)
    return x
```

```python
import functools
import math

import jax
import jax.numpy as jnp
from jax import lax
from jax.experimental import pallas as pl
from jax.experimental.pallas import tpu as pltpu

DEPTH = 2
ATT_HEADS = 4
ATT_DV = 128
ATT_DK = 64
CC_CH = 256
CC_K = 31
SC_CH = 256
SC_K = 3
XA_HEADS = 4
XA_HD = 256
EPS = 1e-6

F32 = jnp.float32
BF16 = jnp.bfloat16

QK_COLS = ATT_HEADS * 2 * ATT_DK
V_COLS = ATT_HEADS * ATT_DV
K_COL = QK_COLS
V_COL = 2 * QK_COLS
CC_COL = V_COL + V_COLS
SC_COL = CC_COL + 2 * CC_CH

TOKEN_TILE = 1024
FFN_ROWS = 512
MIX_OUT_TILE = 1024
ATT_TILE = 512
VT_ROWS = ATT_DV + 2 * 8
LOG2E = math.log2(math.e)
ATT_SCORE_BUFS = 4
ATT_LANES = 256
WEIGHT_CHUNK_BYTES = 3 * 512 * 1024
WEIGHT_SLOTS = 6
LANES = 128
SUBLANES = 8
CONV_ROWS = 64
CC_HALO = 32
SC_HALO = 16
VMEM_LIMIT = 56 * 1024 * 1024
MASK_VALUE = -1e30

_NT = (((1,), (1,)), ((), ()))


def _rms(x, g):
    return x * lax.rsqrt(jnp.mean(x * x, axis=-1, keepdims=True) + EPS) * g


def _const_spec(shape):
    nd = len(shape)
    return pl.BlockSpec(shape, lambda *_: (0,) * nd, pipeline_mode=pl.Buffered(1))


_HBM = pl.BlockSpec(memory_space=pl.ANY)


class _ResidentWeight:
    def __init__(self, stacked, layer):
        _, self.rows, self.cols = stacked.shape
        self.layer = layer
        self.chunk_rows = max(r for r in range(2 * SUBLANES, self.rows + 1, 2 * SUBLANES)
                              if self.rows % r == 0 and r * self.cols * 4 <= WEIGHT_CHUNK_BYTES)

    def scratch(self):
        return pltpu.VMEM((self.rows, self.cols), BF16)

    def load(self, hbm_ref, resident_ref, on_chunk=None):
        rc = self.chunk_rows
        n_chunks = self.rows // rc

        def stream(stage_ref, sem_ref):
            def copy(c):
                slot = c % WEIGHT_SLOTS
                return pltpu.make_async_copy(hbm_ref.at[self.layer, pl.ds(c * rc, rc), :],
                                             stage_ref.at[slot], sem_ref.at[slot])

            for c in range(min(WEIGHT_SLOTS - 1, n_chunks)):
                copy(c).start()
            for c in range(n_chunks):
                if c + WEIGHT_SLOTS - 1 < n_chunks:
                    copy(c + WEIGHT_SLOTS - 1).start()
                copy(c).wait()
                chunk = stage_ref[c % WEIGHT_SLOTS]
                resident_ref[pl.ds(c * rc, rc), :] = chunk.astype(BF16)
                if on_chunk is not None:
                    on_chunk(c * rc, chunk)

        pl.run_scoped(stream, pltpu.VMEM((WEIGHT_SLOTS, rc, self.cols), F32),
                      pltpu.SemaphoreType.DMA((WEIGHT_SLOTS,)))


def _params(n_axes):
    return pltpu.CompilerParams(
        dimension_semantics=("arbitrary",) * n_axes, vmem_limit_bytes=VMEM_LIMIT)


def _fold_bits(v):
    bits = pltpu.bitcast(v, jnp.uint32)
    folded = None
    for i in range(0, bits.shape[0], SUBLANES):
        for c in range(0, bits.shape[1], LANES):
            tile = bits[i:i + SUBLANES, c:c + LANES]
            folded = tile if folded is None else folded | tile
    return folded


def _order_after(dst_ref, bits):
    half = jnp.uint32(16)
    zero = pltpu.bitcast(lax.shift_right_logical(lax.shift_right_logical(bits, half), half), F32)
    zero = jnp.concatenate([zero, zero], axis=0).astype(dst_ref.dtype)
    dst_ref[...] = dst_ref[...] + zero


def _ffn_kernel(x_ref, g_ref, wg_hbm, wu_hbm, wd_hbm, *rest, final, weights):
    if final:
        fg_ref, o_ref, wg_ref, wu_ref, wd_ref = rest
    else:
        o_ref, wg_ref, wu_ref, wd_ref = rest

    @pl.when(pl.program_id(0) == 0)
    def _():
        for w, hbm, resident in zip(weights, (wg_hbm, wu_hbm, wd_hbm), (wg_ref, wu_ref, wd_ref)):
            w.load(hbm, resident)

    for r0 in range(0, x_ref.shape[0], FFN_ROWS):
        rows = pl.ds(r0, FFN_ROWS)
        x = x_ref[rows, :]
        h = _rms(x, g_ref[...]).astype(BF16)
        gate = jnp.dot(h, wg_ref[...], preferred_element_type=F32)
        up = jnp.dot(h, wu_ref[...], preferred_element_type=F32)
        act = (gate * jax.nn.sigmoid(gate) * up).astype(BF16)
        y = x + 0.5 * jnp.dot(act, wd_ref[...], preferred_element_type=F32)
        if final:
            y = _rms(y, fg_ref[...])
        o_ref[rows, :] = y


def _ffn(x, g, wg, wu, wd, layer, final_g=None):
    n, d = x.shape
    tm = TOKEN_TILE
    final = final_g is not None
    row_spec = pl.BlockSpec((tm, d), lambda i: (i, 0))
    weights = [_ResidentWeight(w, layer) for w in (wg, wu, wd)]
    in_specs = [row_spec, _const_spec((1, d)), _HBM, _HBM, _HBM]
    args = [x, g, wg, wu, wd]
    if final:
        in_specs.append(_const_spec((1, d)))
        args.append(final_g)
    return pl.pallas_call(
        functools.partial(_ffn_kernel, final=final, weights=weights),
        grid=(n // tm,),
        in_specs=in_specs,
        out_specs=row_spec,
        out_shape=jax.ShapeDtypeStruct((n, d), F32),
        scratch_shapes=[w.scratch() for w in weights],
        compiler_params=_params(1),
        name="ffn_final" if final else "ffn",
    )(*args)


def _shifted_copies(ext_ref, shift_ref):
    rows = ext_ref.shape[0] - SUBLANES
    for b in range(1, SUBLANES):
        shift_ref[b - 1, pl.ds(0, rows), :] = ext_ref[pl.ds(b, rows), :]


def _dw_conv(ext_ref, w_ref, out_ref, *, taps, halo, blocks, shift_ref=None):
    base = halo - (taps - 1)
    folded = None
    for r in blocks:
        acc = None
        for j in range(taps):
            off = base + j
            if shift_ref is None or off % SUBLANES == 0:
                rows = ext_ref[pl.ds(off + r * CONV_ROWS, CONV_ROWS), :]
            else:
                rows = shift_ref[off % SUBLANES - 1,
                                 pl.ds(off - off % SUBLANES + r * CONV_ROWS, CONV_ROWS), :]
            term = rows * w_ref[pl.ds(j, 1), :]
            acc = term if acc is None else acc + term
        out_ref[pl.ds(r * CONV_ROWS, CONV_ROWS), :] = acc
        folded = _fold_bits(acc) if folded is None else folded | _fold_bits(acc)
    return folded


CC_BLOCK_SPLIT = ((0,), (1,), (2, 3), (4, 5), (6, 7))


def _mix_in_kernel(x_ref, g_ref, win_hbm,
                   ccw_ref, ccb_ref, lng_ref, lnb_ref, scw_ref,
                   qt_ref, k_ref, vt_ref, oc_ref,
                   h_ref, ccx_ref, scx_ref, gate_ref, ccn_ref, scn_ref, gaten_ref, ccs_ref,
                   ccy_ref, scy_ref, wqt_ref, wvt_ref, win_ref,
                   *, tiles_per_seq, w_in):
    tm = x_ref.shape[0]
    g = pl.program_id(0)
    cc_conv = functools.partial(_dw_conv, ccx_ref, ccw_ref, ccy_ref, taps=CC_K, halo=CC_HALO,
                                shift_ref=ccs_ref)

    def cc_conv_then_next_projection(i):
        done = cc_conv(blocks=CC_BLOCK_SPLIT[i])
        _order_after(h_ref.at[pl.ds(0, 2 * SUBLANES), pl.ds(0, LANES)], done)

    @pl.when(g == 0)
    def _():
        ccx_ref[...] = jnp.zeros(ccx_ref.shape, F32)
        scx_ref[...] = jnp.zeros(scx_ref.shape, F32)
        gate_ref[...] = jnp.zeros(gate_ref.shape, F32)

        def transposed_parts(row0, chunk):
            rows = pl.ds(row0, chunk.shape[0])
            wqt_ref[:, rows] = chunk[:, :QK_COLS].T.astype(BF16)
            wvt_ref[:, rows] = chunk[:, V_COL:V_COL + V_COLS].T.astype(BF16)

        w_in.load(win_hbm, win_ref, on_chunk=transposed_parts)

    h_ref[...] = _rms(x_ref[...], g_ref[...]).astype(BF16)
    _shifted_copies(ccx_ref, ccs_ref)

    qt = lax.dot_general(wqt_ref[...], h_ref[...], _NT, preferred_element_type=F32)
    qt_ref[...] = (qt * (LOG2E * ATT_DK ** -0.5)).astype(BF16)
    cc_conv_then_next_projection(0)

    k_ref[...] = jnp.dot(h_ref[...], win_ref[:, pl.ds(K_COL, QK_COLS)],
                         preferred_element_type=F32).astype(BF16)
    cc_conv_then_next_projection(1)

    vt = lax.dot_general(wvt_ref[...], h_ref[...], _NT, preferred_element_type=F32).astype(BF16)
    for hd in range(ATT_HEADS):
        vt_ref[pl.ds(hd * VT_ROWS, ATT_DV), :] = vt[hd * ATT_DV:(hd + 1) * ATT_DV]
        vt_ref[pl.ds(hd * VT_ROWS + ATT_DV, VT_ROWS - ATT_DV), :] = jnp.ones(
            (VT_ROWS - ATT_DV, tm), BF16)
    cc_conv_then_next_projection(2)

    zc = jnp.dot(h_ref[...], win_ref[:, pl.ds(CC_COL, 2 * CC_CH)],
                 preferred_element_type=F32)
    ccn_ref[...] = zc[:, :CC_CH] * jax.nn.sigmoid(zc[:, CC_CH:])
    cc_conv_then_next_projection(3)

    zs = jnp.dot(h_ref[...], win_ref[:, pl.ds(SC_COL, 3 * SC_CH)],
                 preferred_element_type=F32)
    scn_ref[...] = zs[:, SC_CH:2 * SC_CH] * zs[:, 2 * SC_CH:]
    gaten_ref[...] = zs[:, :SC_CH]
    cc_conv(blocks=CC_BLOCK_SPLIT[4])
    _dw_conv(scx_ref, scw_ref, scy_ref, taps=SC_K, halo=SC_HALO, blocks=range(tm // CONV_ROWS))
    oc_ref[:, CC_CH:] = (gate_ref[...] * scy_ref[...]).astype(BF16)

    u = ccy_ref[...] + ccb_ref[...]
    mu = jnp.mean(u, axis=-1, keepdims=True)
    var = jnp.mean(jnp.square(u - mu), axis=-1, keepdims=True)
    y = (u - mu) * lax.rsqrt(var + EPS) * lng_ref[...] + lnb_ref[...]
    oc_ref[:, :CC_CH] = (y * jax.nn.sigmoid(y)).astype(BF16)

    seq_start = lax.rem(g, tiles_per_seq) == 0
    ccx_ref[pl.ds(0, CC_HALO), :] = jnp.where(seq_start, 0.0, ccx_ref[pl.ds(tm, CC_HALO), :])
    ccx_ref[pl.ds(CC_HALO, tm), :] = ccn_ref[...]
    scx_ref[pl.ds(0, SC_HALO), :] = jnp.where(seq_start, 0.0, scx_ref[pl.ds(tm, SC_HALO), :])
    scx_ref[pl.ds(SC_HALO, tm), :] = scn_ref[...]
    gate_ref[...] = gaten_ref[...]


def _mix_in(x, g, w_in, layer, ccw, ccb, lng, lnb, scw):
    b, t, d = x.shape
    tm = ATT_TILE
    nt = t // tm
    n_tiles = b * nt
    qk_cols = QK_COLS
    v_cols = ATT_HEADS * VT_ROWS
    proj = lambda i: jnp.minimum(i, n_tiles - 1)
    conv = lambda i: jnp.maximum(i - 1, 0)
    row = lambda cols, dt, tile: (
        pl.BlockSpec((None, tm, cols), lambda i: (tile(i) // nt, tile(i) % nt, 0)),
        jax.ShapeDtypeStruct((b, t, cols), dt))
    k_spec, k_shape = row(qk_cols, BF16, proj)
    oc_spec, oc_shape = row(CC_CH + SC_CH, BF16, conv)
    slab = lambda rows: (
        pl.BlockSpec((None, None, rows, tm), lambda i: (proj(i) // nt, proj(i) % nt, 0, 0)),
        jax.ShapeDtypeStruct((b, nt, rows, tm), BF16))
    qt_spec, qt_shape = slab(qk_cols)
    vt_spec, vt_shape = slab(v_cols)
    consts = [ccw, ccb, lng, lnb, scw]
    win = _ResidentWeight(w_in, layer)
    assert win.chunk_rows % LANES == 0
    return pl.pallas_call(
        functools.partial(_mix_in_kernel, tiles_per_seq=nt, w_in=win),
        grid=(n_tiles + 1,),
        in_specs=[pl.BlockSpec((None, tm, d), lambda i: (proj(i) // nt, proj(i) % nt, 0)),
                  _const_spec(g.shape), _HBM]
        + [_const_spec(c.shape) for c in consts],
        out_specs=[qt_spec, k_spec, vt_spec, oc_spec],
        out_shape=[qt_shape, k_shape, vt_shape, oc_shape],
        scratch_shapes=[
            pltpu.VMEM((tm, d), BF16),
            pltpu.VMEM((CC_HALO + tm, CC_CH), F32),
            pltpu.VMEM((SC_HALO + tm, SC_CH), F32),
            pltpu.VMEM((tm, SC_CH), F32),
            pltpu.VMEM((tm, CC_CH), F32),
            pltpu.VMEM((tm, SC_CH), F32),
            pltpu.VMEM((tm, SC_CH), F32),
            pltpu.VMEM((SUBLANES - 1, CC_HALO + tm, CC_CH), F32),
            pltpu.VMEM((tm, CC_CH), F32),
            pltpu.VMEM((tm, SC_CH), F32),
            pltpu.VMEM((QK_COLS, d), BF16),
            pltpu.VMEM((V_COLS, d), BF16),
            win.scratch(),
        ],
        compiler_params=_params(1),
        name="mix_in",
    )(x, g, w_in, *consts)


def _diff_attn_kernel(lam_ref, subln_ref, qt_ref, k_ref, vt_ref, o_ref,
                      qs_ref, bias_ref, m_ref, acc_ref, *score_refs, lam_init):
    s_refs, cmax_refs = score_refs[:ATT_SCORE_BUFS], score_refs[ATT_SCORE_BUFS:]
    nt, _, tq = qt_ref.shape
    tk = tq
    head = pl.program_id(1)
    step = 8 // ATT_HEADS
    assert step * ATT_HEADS == 8
    slope = jnp.left_shift(1, step * (ATT_HEADS - 1 - head)).astype(F32) * (LOG2E / 2 ** 8)

    jj = lax.broadcasted_iota(jnp.int32, (tk, 2 * tq), 0)
    ii = lax.broadcasted_iota(jnp.int32, (tk, 2 * tq), 1)
    ii = jnp.where(ii >= tq, ii - tq, ii)
    bias = jj.astype(F32) * slope
    bias_ref[0] = bias
    bias_ref[1] = jnp.where(jj <= ii, bias, MASK_VALUE)

    for i in range(nt):
        qt = qt_ref[i]
        row = lax.broadcasted_iota(jnp.int32, qt.shape, 0)
        zero = jnp.zeros_like(qt)
        qs_ref[i, :, pl.ds(0, tq)] = jnp.where(row < ATT_DK, qt, zero)
        qs_ref[i, :, pl.ds(tq, tq)] = jnp.where(row >= ATT_DK, qt, zero)

    m_ref[...] = jnp.full(m_ref.shape, MASK_VALUE, F32)
    acc_ref[...] = jnp.zeros(acc_ref.shape, F32)

    def scores(item, s_ref, cmax_ref):
        qi, c = item
        j0 = pl.multiple_of(c * tk, tk)
        s = jnp.dot(k_ref[pl.ds(j0, tk), :], qs_ref[qi], preferred_element_type=F32)
        s = s + bias_ref[(c == qi).astype(jnp.int32)]
        s_ref[...] = s
        cmax_ref[...] = jnp.max(s, axis=0, keepdims=True)

    def accumulate(item, s_ref, cmax_ref):
        qi, c = item
        shift = slope * (c * tk).astype(F32)
        vtc = vt_ref[c]
        for nb in range(2 * tq // ATT_LANES):
            cols = pl.ds(nb * ATT_LANES, ATT_LANES)
            m_prev = m_ref[qi, :, cols] - shift
            m_new = jnp.maximum(m_prev, cmax_ref[:, cols])
            alpha = jnp.exp2(m_prev - m_new)
            p = jnp.exp2(s_ref[:, cols] - m_new).astype(BF16)
            m_ref[qi, :, cols] = m_new + shift
            pv = jnp.dot(vtc, p, preferred_element_type=F32)
            acc_ref[qi, :, cols] = alpha * acc_ref[qi, :, cols] + pv

    def following(item):
        qi, c = item
        last = c == qi
        return qi + last.astype(jnp.int32), jnp.where(last, 0, c + 1)

    n_items = nt * (nt + 1) // 2
    n_bufs = len(s_refs)
    assert n_items % n_bufs == 0
    bufs = list(zip(s_refs, cmax_refs))

    def following_clamped(item):
        qi, c = following(item)
        past_end = qi == nt
        return jnp.where(past_end, 0, qi), jnp.where(past_end, 0, c)

    ahead = [(jnp.int32(0), jnp.int32(0))]
    for _ in range(n_bufs - 2):
        ahead.append(following_clamped(ahead[-1]))
    for item, buf in zip(ahead, bufs):
        scores(item, *buf)

    def trip(_, carry):
        items = list(carry)
        for slot in range(n_bufs):
            nxt = following_clamped(items[-1])
            scores(nxt, *bufs[(slot + n_bufs - 1) % n_bufs])
            accumulate(items[0], *bufs[slot])
            items = items[1:] + [nxt]
        return tuple(items)

    lax.fori_loop(0, n_items // n_bufs, trip, tuple(ahead))

    lam_p = lam_ref[...]
    lam = (jnp.exp(jnp.sum(lam_p[0:1] * lam_p[1:2], axis=-1, keepdims=True))
           - jnp.exp(jnp.sum(lam_p[2:3] * lam_p[3:4], axis=-1, keepdims=True)) + lam_init)
    for i in range(nt):
        acc = acc_ref[i, pl.ds(0, ATT_DV), :]
        inv_l = 1.0 / acc_ref[i, pl.ds(ATT_DV, 1), :]
        o = acc[:, :tq] * inv_l[:, :tq] - lam * (acc[:, tq:] * inv_l[:, tq:])
        inv_rms = lax.rsqrt(jnp.mean(o * o, axis=0, keepdims=True) + EPS)
        o_ref[i] = (o * inv_rms * subln_ref[...] * (1.0 - lam_init)).astype(BF16)


def _diff_attn(lam_p, subln, qt, k, vt, lam_init):
    b, t, _ = k.shape
    tq = ATT_TILE
    nt = t // tq
    per_head = lambda rows: pl.BlockSpec((None, nt, rows, tq), lambda i, h: (i, 0, h, 0))
    return pl.pallas_call(
        functools.partial(_diff_attn_kernel, lam_init=lam_init),
        grid=(b, ATT_HEADS),
        in_specs=[
            _const_spec(lam_p.shape),
            _const_spec(subln.shape),
            per_head(2 * ATT_DK),
            pl.BlockSpec((None, t, 2 * ATT_DK), lambda i, h: (i, 0, h)),
            per_head(VT_ROWS),
        ],
        out_specs=per_head(ATT_DV),
        out_shape=jax.ShapeDtypeStruct((b, nt, ATT_HEADS * ATT_DV, tq), BF16),
        scratch_shapes=[
            pltpu.VMEM((nt, 2 * ATT_DK, 2 * tq), BF16),
            pltpu.VMEM((2, tq, 2 * tq), F32),
            pltpu.VMEM((nt, 1, 2 * tq), F32),
            pltpu.VMEM((nt, VT_ROWS, 2 * tq), F32),
        ]
        + [pltpu.VMEM((tq, 2 * tq), F32)] * ATT_SCORE_BUFS
        + [pltpu.VMEM((1, 2 * tq), F32)] * ATT_SCORE_BUFS,
        compiler_params=_params(2),
        name="diff_attn",
    )(lam_p, subln, qt, k, vt)


_TN = (((0,), (0,)), ((), ()))


def _mix_out_kernel(x_ref, oat_ref, oc_ref, mem_ref, mg_ref, g_ref, wo_hbm, wq_hbm, wxo_hbm, wkv_hbm,
                    o_ref, kv_ref, *scratch, weights):
    wo_ref, wq_ref, wxo_ref, wkv_ref = scratch

    @pl.when((pl.program_id(0) == 0) & (pl.program_id(1) == 0))
    def _():
        for w, hbm, resident in zip(weights, (wo_hbm, wq_hbm, wxo_hbm, wkv_hbm), scratch):
            w.load(hbm, resident)

    @pl.when(pl.program_id(1) == 0)
    def _():
        hm = _rms(mem_ref[...], mg_ref[...]).astype(BF16)
        kv_ref[...] = jnp.dot(hm, wkv_ref[...], preferred_element_type=F32).astype(BF16)

    n_slabs, n_att, _ = oat_ref.shape
    wo_att = wo_ref[pl.ds(0, n_att), :]
    att = [lax.dot_general(oat_ref[s], wo_att, _TN, preferred_element_type=F32)
           for s in range(n_slabs)]
    x = x_ref[...] + jnp.concatenate(att, axis=0)
    x = x + jnp.dot(oc_ref[...], wo_ref[pl.ds(n_att, oc_ref.shape[1]), :],
                    preferred_element_type=F32)

    hq = jnp.dot(_rms(x, g_ref[...]).astype(BF16), wq_ref[...], preferred_element_type=F32)
    hq = (hq * (XA_HD ** -0.5)).astype(BF16)
    kd = XA_HEADS * XA_HD
    heads = []
    for h in range(XA_HEADS):
        sl = slice(h * XA_HD, (h + 1) * XA_HD)
        s = lax.dot_general(hq[:, sl], kv_ref[:, sl], _NT, preferred_element_type=F32)
        p = jnp.exp(s - jnp.max(s, axis=-1, keepdims=True))
        inv = 1.0 / jnp.sum(p, axis=-1, keepdims=True)
        o = jnp.dot(p.astype(BF16), kv_ref[:, kd + h * XA_HD:kd + (h + 1) * XA_HD],
                    preferred_element_type=F32)
        heads.append((o * inv).astype(BF16))
    o = jnp.concatenate(heads, axis=-1)
    o_ref[...] = x + jnp.dot(o, wxo_ref[...], preferred_element_type=F32)


def _mix_out(x, oat, oc, mem, mem_g, g, wo, wq, wxo, wkv, layer):
    b, t, d = x.shape
    tq = oat.shape[3]
    n_slabs = MIX_OUT_TILE // tq
    tm = MIX_OUT_TILE
    m = mem.shape[1]
    row = lambda cols: pl.BlockSpec((None, tm, cols), lambda i, j: (i, j, 0))
    weights = [_ResidentWeight(w, layer) for w in (wo, wq, wxo, wkv)]
    return pl.pallas_call(
        functools.partial(_mix_out_kernel, weights=weights),
        grid=(b, t // tm),
        in_specs=[row(d),
                  pl.BlockSpec((None, n_slabs, oat.shape[2], tq), lambda i, j: (i, j, 0, 0)),
                  row(oc.shape[2]),
                  pl.BlockSpec((None, m, d), lambda i, j: (i, 0, 0)),
                  _const_spec(mem_g.shape), _const_spec(g.shape), _HBM, _HBM, _HBM, _HBM],
        out_specs=row(d),
        out_shape=jax.ShapeDtypeStruct((b, t, d), F32),
        scratch_shapes=[pltpu.VMEM((m, wkv.shape[2]), BF16)]
        + [w.scratch() for w in weights],
        compiler_params=_params(2),
        name="mix_out",
    )(x, oat, oc, mem, mem_g, g, wo, wq, wxo, wkv)


def kernel(x, mem, ffn1_norm, ffn1_w_gate, ffn1_w_up, ffn1_w_down, mix_norm, w_in, lam_q1, lam_k1, lam_q2, lam_k2, diff_subln, cc_dw, cc_dw_b, cc_ln_g, cc_ln_b, sc_dw, w_out, xa_norm, mem_norm, xa_wq, xa_wkv, xa_wo, ffn2_norm, ffn2_w_gate, ffn2_w_up, ffn2_w_down, final_norm):
    b, t, d = x.shape
    vec = lambda v: v.reshape(1, -1).astype(F32)

    for l in range(DEPTH):
        x = _ffn(x.reshape(b * t, d), vec(ffn1_norm[l]), ffn1_w_gate, ffn1_w_up, ffn1_w_down,
                 l).reshape(b, t, d)

        qt, k, vt, oc = _mix_in(
            x, vec(mix_norm[l]), w_in, l,
            cc_dw[l].astype(F32), vec(cc_dw_b[l]), vec(cc_ln_g[l]), vec(cc_ln_b[l]),
            sc_dw[l].astype(F32))

        lam_init = 0.8 - 0.6 * math.exp(-0.3 * l)
        lam_p = jnp.stack([lam_q1[l], lam_k1[l], lam_q2[l], lam_k2[l]]).astype(F32)
        subln = jnp.broadcast_to(diff_subln[l].astype(F32)[:, None], (ATT_DV, ATT_TILE))
        oat = _diff_attn(lam_p, subln, qt, k, vt, lam_init)

        x = _mix_out(x, oat, oc, mem, vec(mem_norm[l]), vec(xa_norm[l]),
                     w_out, xa_wq, xa_wo, xa_wkv, l)

        last = l == DEPTH - 1
        x = _ffn(x.reshape(b * t, d), vec(ffn2_norm[l]), ffn2_w_gate, ffn2_w_up, ffn2_w_down, l,
                 vec(final_norm) if last else None).reshape(b, t, d)
    return x
```

```python
import functools
import math

import jax
import jax.numpy as jnp
from jax import lax
from jax.experimental import pallas as pl
from jax.experimental.pallas import tpu as pltpu

DEPTH = 2
ATT_HEADS = 4
ATT_DV = 128
ATT_DK = 64
CC_CH = 256
CC_K = 31
SC_CH = 256
SC_K = 3
XA_HEADS = 4
XA_HD = 256
EPS = 1e-6

F32 = jnp.float32
BF16 = jnp.bfloat16

QK_COLS = ATT_HEADS * 2 * ATT_DK
V_COLS = ATT_HEADS * ATT_DV
K_COL = QK_COLS
V_COL = 2 * QK_COLS
CC_COL = V_COL + V_COLS
SC_COL = CC_COL + 2 * CC_CH

TOKEN_TILE = 1024
FFN_ROWS = 512
MIX_OUT_TILE = 1024
ATT_TILE = 512
VT_ROWS = ATT_DV + 2 * 8
LOG2E = math.log2(math.e)
ATT_SCORE_BUFS = 3
ATT_LANES = 256
WEIGHT_CHUNK_BYTES = 3 * 512 * 1024
WEIGHT_SLOTS = 6
LANES = 128
SUBLANES = 8
CONV_ROWS = 64
CC_HALO = 32
SC_HALO = 16
VMEM_LIMIT = 56 * 1024 * 1024
MASK_VALUE = -1e30

_NT = (((1,), (1,)), ((), ()))


def _rms(x, g):
    return x * lax.rsqrt(jnp.mean(x * x, axis=-1, keepdims=True) + EPS) * g


def _const_spec(shape):
    nd = len(shape)
    return pl.BlockSpec(shape, lambda *_: (0,) * nd, pipeline_mode=pl.Buffered(1))


_HBM = pl.BlockSpec(memory_space=pl.ANY)


class _ResidentWeight:
    def __init__(self, stacked, layer):
        _, self.rows, self.cols = stacked.shape
        self.layer = layer
        self.chunk_rows = max(r for r in range(2 * SUBLANES, self.rows + 1, 2 * SUBLANES)
                              if self.rows % r == 0 and r * self.cols * 4 <= WEIGHT_CHUNK_BYTES)

    def scratch(self):
        return pltpu.VMEM((self.rows, self.cols), BF16)

    def load(self, hbm_ref, resident_ref, on_chunk=None):
        rc = self.chunk_rows
        n_chunks = self.rows // rc

        def stream(stage_ref, sem_ref):
            def copy(c):
                slot = c % WEIGHT_SLOTS
                return pltpu.make_async_copy(hbm_ref.at[self.layer, pl.ds(c * rc, rc), :],
                                             stage_ref.at[slot], sem_ref.at[slot])

            for c in range(min(WEIGHT_SLOTS - 1, n_chunks)):
                copy(c).start()
            for c in range(n_chunks):
                if c + WEIGHT_SLOTS - 1 < n_chunks:
                    copy(c + WEIGHT_SLOTS - 1).start()
                copy(c).wait()
                chunk = stage_ref[c % WEIGHT_SLOTS]
                resident_ref[pl.ds(c * rc, rc), :] = chunk.astype(BF16)
                if on_chunk is not None:
                    on_chunk(c * rc, chunk)

        pl.run_scoped(stream, pltpu.VMEM((WEIGHT_SLOTS, rc, self.cols), F32),
                      pltpu.SemaphoreType.DMA((WEIGHT_SLOTS,)))


def _params(n_axes):
    return pltpu.CompilerParams(
        dimension_semantics=("arbitrary",) * n_axes, vmem_limit_bytes=VMEM_LIMIT)


def _fold_bits(v):
    bits = pltpu.bitcast(v, jnp.uint32)
    folded = None
    for i in range(0, bits.shape[0], SUBLANES):
        for c in range(0, bits.shape[1], LANES):
            tile = bits[i:i + SUBLANES, c:c + LANES]
            folded = tile if folded is None else folded | tile
    return folded


def _order_after(dst_ref, bits):
    half = jnp.uint32(16)
    zero = pltpu.bitcast(lax.shift_right_logical(lax.shift_right_logical(bits, half), half), F32)
    zero = jnp.concatenate([zero, zero], axis=0).astype(dst_ref.dtype)
    dst_ref[...] = dst_ref[...] + zero


def _ffn_kernel(x_ref, g_ref, wg_hbm, wu_hbm, wd_hbm, *rest, final, weights):
    if final:
        fg_ref, o_ref, wg_ref, wu_ref, wd_ref = rest
    else:
        o_ref, wg_ref, wu_ref, wd_ref = rest

    @pl.when(pl.program_id(0) == 0)
    def _():
        for w, hbm, resident in zip(weights, (wg_hbm, wu_hbm, wd_hbm), (wg_ref, wu_ref, wd_ref)):
            w.load(hbm, resident)

    for r0 in range(0, x_ref.shape[0], FFN_ROWS):
        rows = pl.ds(r0, FFN_ROWS)
        x = x_ref[rows, :]
        h = _rms(x, g_ref[...]).astype(BF16)
        gate = jnp.dot(h, wg_ref[...], preferred_element_type=F32)
        up = jnp.dot(h, wu_ref[...], preferred_element_type=F32)
        act = (gate * jax.nn.sigmoid(gate) * up).astype(BF16)
        y = x + 0.5 * jnp.dot(act, wd_ref[...], preferred_element_type=F32)
        if final:
            y = _rms(y, fg_ref[...])
        o_ref[rows, :] = y


def _ffn(x, g, wg, wu, wd, layer, final_g=None):
    n, d = x.shape
    tm = TOKEN_TILE
    final = final_g is not None
    row_spec = pl.BlockSpec((tm, d), lambda i: (i, 0))
    weights = [_ResidentWeight(w, layer) for w in (wg, wu, wd)]
    in_specs = [row_spec, _const_spec((1, d)), _HBM, _HBM, _HBM]
    args = [x, g, wg, wu, wd]
    if final:
        in_specs.append(_const_spec((1, d)))
        args.append(final_g)
    return pl.pallas_call(
        functools.partial(_ffn_kernel, final=final, weights=weights),
        grid=(n // tm,),
        in_specs=in_specs,
        out_specs=row_spec,
        out_shape=jax.ShapeDtypeStruct((n, d), F32),
        scratch_shapes=[w.scratch() for w in weights],
        compiler_params=_params(1),
        name="ffn_final" if final else "ffn",
    )(*args)


def _shifted_copies(ext_ref, shift_ref):
    rows = ext_ref.shape[0] - SUBLANES
    for b in range(1, SUBLANES):
        shift_ref[b - 1, pl.ds(0, rows), :] = ext_ref[pl.ds(b, rows), :]


def _dw_conv(ext_ref, w_ref, out_ref, *, taps, halo, blocks, shift_ref=None):
    base = halo - (taps - 1)
    folded = None
    for r in blocks:
        acc = None
        for j in range(taps):
            off = base + j
            if shift_ref is None or off % SUBLANES == 0:
                rows = ext_ref[pl.ds(off + r * CONV_ROWS, CONV_ROWS), :]
            else:
                rows = shift_ref[off % SUBLANES - 1,
                                 pl.ds(off - off % SUBLANES + r * CONV_ROWS, CONV_ROWS), :]
            term = rows * w_ref[pl.ds(j, 1), :]
            acc = term if acc is None else acc + term
        out_ref[pl.ds(r * CONV_ROWS, CONV_ROWS), :] = acc
        folded = _fold_bits(acc) if folded is None else folded | _fold_bits(acc)
    return folded


CC_BLOCK_SPLIT = ((0,), (1,), (2, 3), (4, 5), (6, 7))


def _mix_in_kernel(x_ref, g_ref, win_hbm,
                   ccw_ref, ccb_ref, lng_ref, lnb_ref, scw_ref,
                   qt_ref, k_ref, vt_ref, oc_ref,
                   h_ref, ccx_ref, scx_ref, gate_ref, ccn_ref, scn_ref, gaten_ref, ccs_ref,
                   ccy_ref, scy_ref, wqt_ref, wvt_ref, win_ref,
                   *, tiles_per_seq, w_in):
    tm = x_ref.shape[0]
    g = pl.program_id(0)
    cc_conv = functools.partial(_dw_conv, ccx_ref, ccw_ref, ccy_ref, taps=CC_K, halo=CC_HALO,
                                shift_ref=ccs_ref)

    def cc_conv_then_next_projection(i):
        done = cc_conv(blocks=CC_BLOCK_SPLIT[i])
        _order_after(h_ref.at[pl.ds(0, 2 * SUBLANES), pl.ds(0, LANES)], done)

    @pl.when(g == 0)
    def _():
        ccx_ref[...] = jnp.zeros(ccx_ref.shape, F32)
        scx_ref[...] = jnp.zeros(scx_ref.shape, F32)
        gate_ref[...] = jnp.zeros(gate_ref.shape, F32)

        def transposed_parts(row0, chunk):
            rows = pl.ds(row0, chunk.shape[0])
            wqt_ref[:, rows] = chunk[:, :QK_COLS].T.astype(BF16)
            wvt_ref[:, rows] = chunk[:, V_COL:V_COL + V_COLS].T.astype(BF16)

        w_in.load(win_hbm, win_ref, on_chunk=transposed_parts)

    h_ref[...] = _rms(x_ref[...], g_ref[...]).astype(BF16)
    _shifted_copies(ccx_ref, ccs_ref)

    qt = lax.dot_general(wqt_ref[...], h_ref[...], _NT, preferred_element_type=F32)
    qt_ref[...] = (qt * (LOG2E * ATT_DK ** -0.5)).astype(BF16)
    cc_conv_then_next_projection(0)

    k_ref[...] = jnp.dot(h_ref[...], win_ref[:, pl.ds(K_COL, QK_COLS)],
                         preferred_element_type=F32).astype(BF16)
    cc_conv_then_next_projection(1)

    vt = lax.dot_general(wvt_ref[...], h_ref[...], _NT, preferred_element_type=F32).astype(BF16)
    for hd in range(ATT_HEADS):
        vt_ref[pl.ds(hd * VT_ROWS, ATT_DV), :] = vt[hd * ATT_DV:(hd + 1) * ATT_DV]
        vt_ref[pl.ds(hd * VT_ROWS + ATT_DV, VT_ROWS - ATT_DV), :] = jnp.ones(
            (VT_ROWS - ATT_DV, tm), BF16)
    cc_conv_then_next_projection(2)

    zc = jnp.dot(h_ref[...], win_ref[:, pl.ds(CC_COL, 2 * CC_CH)],
                 preferred_element_type=F32)
    ccn_ref[...] = zc[:, :CC_CH] * jax.nn.sigmoid(zc[:, CC_CH:])
    cc_conv_then_next_projection(3)

    zs = jnp.dot(h_ref[...], win_ref[:, pl.ds(SC_COL, 3 * SC_CH)],
                 preferred_element_type=F32)
    scn_ref[...] = zs[:, SC_CH:2 * SC_CH] * zs[:, 2 * SC_CH:]
    gaten_ref[...] = zs[:, :SC_CH]
    cc_conv(blocks=CC_BLOCK_SPLIT[4])
    _dw_conv(scx_ref, scw_ref, scy_ref, taps=SC_K, halo=SC_HALO, blocks=range(tm // CONV_ROWS))
    oc_ref[:, CC_CH:] = (gate_ref[...] * scy_ref[...]).astype(BF16)

    u = ccy_ref[...] + ccb_ref[...]
    mu = jnp.mean(u, axis=-1, keepdims=True)
    var = jnp.mean(jnp.square(u - mu), axis=-1, keepdims=True)
    y = (u - mu) * lax.rsqrt(var + EPS) * lng_ref[...] + lnb_ref[...]
    oc_ref[:, :CC_CH] = (y * jax.nn.sigmoid(y)).astype(BF16)

    seq_start = lax.rem(g, tiles_per_seq) == 0
    ccx_ref[pl.ds(0, CC_HALO), :] = jnp.where(seq_start, 0.0, ccx_ref[pl.ds(tm, CC_HALO), :])
    ccx_ref[pl.ds(CC_HALO, tm), :] = ccn_ref[...]
    scx_ref[pl.ds(0, SC_HALO), :] = jnp.where(seq_start, 0.0, scx_ref[pl.ds(tm, SC_HALO), :])
    scx_ref[pl.ds(SC_HALO, tm), :] = scn_ref[...]
    gate_ref[...] = gaten_ref[...]


def _mix_in(x, g, w_in, layer, ccw, ccb, lng, lnb, scw):
    b, t, d = x.shape
    tm = ATT_TILE
    nt = t // tm
    n_tiles = b * nt
    qk_cols = QK_COLS
    v_cols = ATT_HEADS * VT_ROWS
    proj = lambda i: jnp.minimum(i, n_tiles - 1)
    conv = lambda i: jnp.maximum(i - 1, 0)
    row = lambda cols, dt, tile: (
        pl.BlockSpec((None, tm, cols), lambda i: (tile(i) // nt, tile(i) % nt, 0)),
        jax.ShapeDtypeStruct((b, t, cols), dt))
    k_spec, k_shape = row(qk_cols, BF16, proj)
    oc_spec, oc_shape = row(CC_CH + SC_CH, BF16, conv)
    slab = lambda rows: (
        pl.BlockSpec((None, None, rows, tm), lambda i: (proj(i) // nt, proj(i) % nt, 0, 0)),
        jax.ShapeDtypeStruct((b, nt, rows, tm), BF16))
    qt_spec, qt_shape = slab(qk_cols)
    vt_spec, vt_shape = slab(v_cols)
    consts = [ccw, ccb, lng, lnb, scw]
    win = _ResidentWeight(w_in, layer)
    assert win.chunk_rows % LANES == 0
    return pl.pallas_call(
        functools.partial(_mix_in_kernel, tiles_per_seq=nt, w_in=win),
        grid=(n_tiles + 1,),
        in_specs=[pl.BlockSpec((None, tm, d), lambda i: (proj(i) // nt, proj(i) % nt, 0)),
                  _const_spec(g.shape), _HBM]
        + [_const_spec(c.shape) for c in consts],
        out_specs=[qt_spec, k_spec, vt_spec, oc_spec],
        out_shape=[qt_shape, k_shape, vt_shape, oc_shape],
        scratch_shapes=[
            pltpu.VMEM((tm, d), BF16),
            pltpu.VMEM((CC_HALO + tm, CC_CH), F32),
            pltpu.VMEM((SC_HALO + tm, SC_CH), F32),
            pltpu.VMEM((tm, SC_CH), F32),
            pltpu.VMEM((tm, CC_CH), F32),
            pltpu.VMEM((tm, SC_CH), F32),
            pltpu.VMEM((tm, SC_CH), F32),
            pltpu.VMEM((SUBLANES - 1, CC_HALO + tm, CC_CH), F32),
            pltpu.VMEM((tm, CC_CH), F32),
            pltpu.VMEM((tm, SC_CH), F32),
            pltpu.VMEM((QK_COLS, d), BF16),
            pltpu.VMEM((V_COLS, d), BF16),
            win.scratch(),
        ],
        compiler_params=_params(1),
        name="mix_in",
    )(x, g, w_in, *consts)


def _diff_attn_kernel(lam_ref, subln_ref, qt_ref, k_ref, vt_ref, o_ref,
                      qs_ref, bias_ref, m_ref, acc_ref, *score_refs, lam_init):
    s_refs, cmax_refs = score_refs[:ATT_SCORE_BUFS], score_refs[ATT_SCORE_BUFS:]
    nt, _, tq = qt_ref.shape
    tk = tq
    head = pl.program_id(1)
    step = 8 // ATT_HEADS
    assert step * ATT_HEADS == 8
    slope = jnp.left_shift(1, step * (ATT_HEADS - 1 - head)).astype(F32) * (LOG2E / 2 ** 8)

    jj = lax.broadcasted_iota(jnp.int32, (tk, 2 * tq), 0)
    ii = lax.broadcasted_iota(jnp.int32, (tk, 2 * tq), 1)
    ii = jnp.where(ii >= tq, ii - tq, ii)
    bias = jj.astype(F32) * slope
    bias_ref[0] = bias
    bias_ref[1] = jnp.where(jj <= ii, bias, MASK_VALUE)

    for i in range(nt):
        qt = qt_ref[i]
        row = lax.broadcasted_iota(jnp.int32, qt.shape, 0)
        zero = jnp.zeros_like(qt)
        qs_ref[i, :, pl.ds(0, tq)] = jnp.where(row < ATT_DK, qt, zero)
        qs_ref[i, :, pl.ds(tq, tq)] = jnp.where(row >= ATT_DK, qt, zero)

    m_ref[...] = jnp.full(m_ref.shape, MASK_VALUE, F32)
    acc_ref[...] = jnp.zeros(acc_ref.shape, F32)

    def scores(item, s_ref, cmax_ref):
        qi, c = item
        j0 = pl.multiple_of(c * tk, tk)
        s = jnp.dot(k_ref[pl.ds(j0, tk), :], qs_ref[qi], preferred_element_type=F32)
        s = s + bias_ref[(c == qi).astype(jnp.int32)]
        s_ref[...] = s
        cmax_ref[...] = jnp.max(s, axis=0, keepdims=True)

    def accumulate(item, s_ref, cmax_ref):
        qi, c = item
        shift = slope * (c * tk).astype(F32)
        vtc = vt_ref[c]
        for nb in range(2 * tq // ATT_LANES):
            cols = pl.ds(nb * ATT_LANES, ATT_LANES)
            m_prev = m_ref[qi, :, cols] - shift
            m_new = jnp.maximum(m_prev, cmax_ref[:, cols])
            alpha = jnp.exp2(m_prev - m_new)
            p = jnp.exp2(s_ref[:, cols] - m_new).astype(BF16)
            m_ref[qi, :, cols] = m_new + shift
            pv = jnp.dot(vtc, p, preferred_element_type=F32)
            acc_ref[qi, :, cols] = alpha * acc_ref[qi, :, cols] + pv

    def following(item):
        qi, c = item
        last = c == qi
        return qi + last.astype(jnp.int32), jnp.where(last, 0, c + 1)

    n_items = nt * (nt + 1) // 2
    n_bufs = len(s_refs)
    assert n_items % n_bufs == 0
    bufs = list(zip(s_refs, cmax_refs))

    def following_clamped(item):
        qi, c = following(item)
        past_end = qi == nt
        return jnp.where(past_end, 0, qi), jnp.where(past_end, 0, c)

    ahead = [(jnp.int32(0), jnp.int32(0))]
    for _ in range(n_bufs - 2):
        ahead.append(following_clamped(ahead[-1]))
    for item, buf in zip(ahead, bufs):
        scores(item, *buf)

    def trip(_, carry):
        items = list(carry)
        for slot in range(n_bufs):
            nxt = following_clamped(items[-1])
            scores(nxt, *bufs[(slot + n_bufs - 1) % n_bufs])
            accumulate(items[0], *bufs[slot])
            items = items[1:] + [nxt]
        return tuple(items)

    lax.fori_loop(0, n_items // n_bufs, trip, tuple(ahead))

    lam_p = lam_ref[...]
    lam = (jnp.exp(jnp.sum(lam_p[0:1] * lam_p[1:2], axis=-1, keepdims=True))
           - jnp.exp(jnp.sum(lam_p[2:3] * lam_p[3:4], axis=-1, keepdims=True)) + lam_init)
    for i in range(nt):
        acc = acc_ref[i, pl.ds(0, ATT_DV), :]
        inv_l = 1.0 / acc_ref[i, pl.ds(ATT_DV, 1), :]
        o = acc[:, :tq] * inv_l[:, :tq] - lam * (acc[:, tq:] * inv_l[:, tq:])
        inv_rms = lax.rsqrt(jnp.mean(o * o, axis=0, keepdims=True) + EPS)
        o_ref[i] = (o * inv_rms * subln_ref[...] * (1.0 - lam_init)).astype(BF16)


def _diff_attn(lam_p, subln, qt, k, vt, lam_init):
    b, t, _ = k.shape
    tq = ATT_TILE
    nt = t // tq
    per_head = lambda rows: pl.BlockSpec((None, nt, rows, tq), lambda i, h: (i, 0, h, 0))
    return pl.pallas_call(
        functools.partial(_diff_attn_kernel, lam_init=lam_init),
        grid=(b, ATT_HEADS),
        in_specs=[
            _const_spec(lam_p.shape),
            _const_spec(subln.shape),
            per_head(2 * ATT_DK),
            pl.BlockSpec((None, t, 2 * ATT_DK), lambda i, h: (i, 0, h)),
            per_head(VT_ROWS),
        ],
        out_specs=per_head(ATT_DV),
        out_shape=jax.ShapeDtypeStruct((b, nt, ATT_HEADS * ATT_DV, tq), BF16),
        scratch_shapes=[
            pltpu.VMEM((nt, 2 * ATT_DK, 2 * tq), BF16),
            pltpu.VMEM((2, tq, 2 * tq), F32),
            pltpu.VMEM((nt, 1, 2 * tq), F32),
            pltpu.VMEM((nt, VT_ROWS, 2 * tq), F32),
        ]
        + [pltpu.VMEM((tq, 2 * tq), F32)] * ATT_SCORE_BUFS
        + [pltpu.VMEM((1, 2 * tq), F32)] * ATT_SCORE_BUFS,
        compiler_params=_params(2),
        name="diff_attn",
    )(lam_p, subln, qt, k, vt)


_TN = (((0,), (0,)), ((), ()))


def _mix_out_kernel(x_ref, oat_ref, oc_ref, mem_ref, mg_ref, g_ref, wo_hbm, wq_hbm, wxo_hbm, wkv_hbm,
                    o_ref, kv_ref, *scratch, weights):
    wo_ref, wq_ref, wxo_ref, wkv_ref = scratch

    @pl.when((pl.program_id(0) == 0) & (pl.program_id(1) == 0))
    def _():
        for w, hbm, resident in zip(weights, (wo_hbm, wq_hbm, wxo_hbm, wkv_hbm), scratch):
            w.load(hbm, resident)

    @pl.when(pl.program_id(1) == 0)
    def _():
        hm = _rms(mem_ref[...], mg_ref[...]).astype(BF16)
        kv_ref[...] = jnp.dot(hm, wkv_ref[...], preferred_element_type=F32).astype(BF16)

    n_slabs, n_att, _ = oat_ref.shape
    wo_att = wo_ref[pl.ds(0, n_att), :]
    att = [lax.dot_general(oat_ref[s], wo_att, _TN, preferred_element_type=F32)
           for s in range(n_slabs)]
    x = x_ref[...] + jnp.concatenate(att, axis=0)
    x = x + jnp.dot(oc_ref[...], wo_ref[pl.ds(n_att, oc_ref.shape[1]), :],
                    preferred_element_type=F32)

    hq = jnp.dot(_rms(x, g_ref[...]).astype(BF16), wq_ref[...], preferred_element_type=F32)
    hq = (hq * (XA_HD ** -0.5)).astype(BF16)
    kd = XA_HEADS * XA_HD
    heads = []
    for h in range(XA_HEADS):
        sl = slice(h * XA_HD, (h + 1) * XA_HD)
        s = lax.dot_general(hq[:, sl], kv_ref[:, sl], _NT, preferred_element_type=F32)
        p = jnp.exp(s - jnp.max(s, axis=-1, keepdims=True))
        inv = 1.0 / jnp.sum(p, axis=-1, keepdims=True)
        o = jnp.dot(p.astype(BF16), kv_ref[:, kd + h * XA_HD:kd + (h + 1) * XA_HD],
                    preferred_element_type=F32)
        heads.append((o * inv).astype(BF16))
    o = jnp.concatenate(heads, axis=-1)
    o_ref[...] = x + jnp.dot(o, wxo_ref[...], preferred_element_type=F32)


def _mix_out(x, oat, oc, mem, mem_g, g, wo, wq, wxo, wkv, layer):
    b, t, d = x.shape
    tq = oat.shape[3]
    n_slabs = MIX_OUT_TILE // tq
    tm = MIX_OUT_TILE
    m = mem.shape[1]
    row = lambda cols: pl.BlockSpec((None, tm, cols), lambda i, j: (i, j, 0))
    weights = [_ResidentWeight(w, layer) for w in (wo, wq, wxo, wkv)]
    return pl.pallas_call(
        functools.partial(_mix_out_kernel, weights=weights),
        grid=(b, t // tm),
        in_specs=[row(d),
                  pl.BlockSpec((None, n_slabs, oat.shape[2], tq), lambda i, j: (i, j, 0, 0)),
                  row(oc.shape[2]),
                  pl.BlockSpec((None, m, d), lambda i, j: (i, 0, 0)),
                  _const_spec(mem_g.shape), _const_spec(g.shape), _HBM, _HBM, _HBM, _HBM],
        out_specs=row(d),
        out_shape=jax.ShapeDtypeStruct((b, t, d), F32),
        scratch_shapes=[pltpu.VMEM((m, wkv.shape[2]), BF16)]
        + [w.scratch() for w in weights],
        compiler_params=_params(2),
        name="mix_out",
    )(x, oat, oc, mem, mem_g, g, wo, wq, wxo, wkv)


def kernel(x, mem, ffn1_norm, ffn1_w_gate, ffn1_w_up, ffn1_w_down, mix_norm, w_in, lam_q1, lam_k1, lam_q2, lam_k2, diff_subln, cc_dw, cc_dw_b, cc_ln_g, cc_ln_b, sc_dw, w_out, xa_norm, mem_norm, xa_wq, xa_wkv, xa_wo, ffn2_norm, ffn2_w_gate, ffn2_w_up, ffn2_w_down, final_norm):
    b, t, d = x.shape
    vec = lambda v: v.reshape(1, -1).astype(F32)

    for l in range(DEPTH):
        x = _ffn(x.reshape(b * t, d), vec(ffn1_norm[l]), ffn1_w_gate, ffn1_w_up, ffn1_w_down,
                 l).reshape(b, t, d)

        qt, k, vt, oc = _mix_in(
            x, vec(mix_norm[l]), w_in, l,
            cc_dw[l].astype(F32), vec(cc_dw_b[l]), vec(cc_ln_g[l]), vec(cc_ln_b[l]),
            sc_dw[l].astype(F32))

        lam_init = 0.8 - 0.6 * math.exp(-0.3 * l)
        lam_p = jnp.stack([lam_q1[l], lam_k1[l], lam_q2[l], lam_k2[l]]).astype(F32)
        subln = jnp.broadcast_to(diff_subln[l].astype(F32)[:, None], (ATT_DV, ATT_TILE))
        oat = _diff_attn(lam_p, subln, qt, k, vt, lam_init)

        x = _mix_out(x, oat, oc, mem, vec(mem_norm[l]), vec(xa_norm[l]),
                     w_out, xa_wq, xa_wo, xa_wkv, l)

        last = l == DEPTH - 1
        x = _ffn(x.reshape(b * t, d), vec(ffn2_norm[l]), ffn2_w_gate, ffn2_w_up, ffn2_w_down, l,
                 vec(final_norm) if last else None).reshape(b, t, d)
    return x
```
